```python
import math
import jax, jax.numpy as jnp
from jax import lax
import numpy as np

D_MODEL = 4096
BATCH = 4
SEQ = 4096
DEPTH = 1

N_HEADS_A = 16
HEAD_DIM_A = 128
N_IDX_HEADS = 32
IDX_DIM = 128
TOPK_MAX = 256
Q_BLOCK = 128
N_BUCKETS = 32
MAX_DISTANCE = 128
GMLP_WIDTH = 2048
N_GROUPS_B = 8
GROUP_DIM_B = GMLP_WIDTH // N_GROUPS_B
CHUNK = 128
D_FF = ((-(-8 * D_MODEL // 3) + 255) // 256) * 256
D_PLE = 256
EPS = 1e-6

A_WIDTH = N_HEADS_A * HEAD_DIM_A
SPLIT_SIZES = (A_WIDTH, HEAD_DIM_A, HEAD_DIM_A, N_IDX_HEADS * IDX_DIM, IDX_DIM, N_IDX_HEADS, 2 * GMLP_WIDTH, D_MODEL, D_MODEL)
IN_COLS = sum(SPLIT_SIZES)

kernel_name = "hybrid_dsa_gmlp_gated_block"


def rmsnorm(x, g):
    xf = x.astype(jnp.float32)
    y = xf * lax.rsqrt(jnp.mean(xf * xf, axis=-1, keepdims=True) + EPS)
    return (y * g.astype(jnp.float32)).astype(x.dtype)


def layernorm(x, g, b):
    xf = x.astype(jnp.float32)
    mu = jnp.mean(xf, axis=-1, keepdims=True)
    var = jnp.mean(jnp.square(xf - mu), axis=-1, keepdims=True)
    y = (xf - mu) * lax.rsqrt(var + EPS)
    return (y * g.astype(jnp.float32) + b.astype(jnp.float32)).astype(x.dtype)


def t5_bucket(n):
    max_exact = N_BUCKETS // 2
    n = jnp.maximum(n, 0)
    nf = jnp.maximum(n, 1).astype(jnp.float32)
    large = max_exact + (jnp.log(nf / max_exact) / math.log(MAX_DISTANCE / max_exact) * (N_BUCKETS - max_exact)).astype(jnp.int32)
    large = jnp.minimum(large, N_BUCKETS - 1)
    return jnp.where(n < max_exact, n, large)


def dsa_attention(q, k, v, q_idx, k_idx, w_idx, rel_bias):
    B, S = q.shape[0], q.shape[1]
    k_sel = min(TOPK_MAX, S // 4)
    n_blk = S // Q_BLOCK
    idx_scale = IDX_DIM ** -0.5 * N_IDX_HEADS ** -0.5
    attn_scale = HEAD_DIM_A ** -0.5
    key_pos = jnp.arange(S, dtype=jnp.int32)
    k_idx_f = k_idx.astype(jnp.float32)
    gather = jax.vmap(lambda t, i: t[i])

    def blk(a):
        return jnp.swapaxes(a.reshape((B, n_blk, Q_BLOCK) + a.shape[2:]), 0, 1)

    def one_block(args):
        j, qb, qib, wb = args
        q_pos = j * Q_BLOCK + jnp.arange(Q_BLOCK, dtype=jnp.int32)
        dots = jnp.einsum('bqhd,bsd->bqhs', qib.astype(jnp.float32), k_idx_f)
        score = jnp.einsum('bqhs,bqh->bqs', jax.nn.relu(dots), wb.astype(jnp.float32)) * idx_scale
        causal = key_pos[None, :] <= q_pos[:, None]
        score = jnp.where(causal[None], score, -jnp.inf)
        _, idx = lax.top_k(score, k_sel)
        ks = gather(k, idx)
        vs = gather(v, idx)
        valid = idx <= q_pos[None, :, None]
        bias = rel_bias[t5_bucket(q_pos[None, :, None] - idx)]
        logits = jnp.einsum('bqhd,bqkd->bhqk', qb, ks).astype(jnp.float32) * attn_scale
        logits = logits + jnp.transpose(bias, (0, 3, 1, 2)).astype(jnp.float32)
        logits = jnp.where(valid[:, None], logits, -jnp.inf)
        probs = jax.nn.softmax(logits, axis=-1).astype(vs.dtype)
        return jnp.einsum('bhqk,bqkd->bqhd', probs, vs)

    out = lax.map(one_block, (jnp.arange(n_blk, dtype=jnp.int32), blk(q), blk(q_idx), blk(w_idx)))
    return jnp.swapaxes(out, 0, 1).reshape(B, S, N_HEADS_A * HEAD_DIM_A)


def chunked_sgu(uv, ln_g, ln_b, w_s, b_s):
    B, S = uv.shape[0], uv.shape[1]
    uv = jax.nn.gelu(uv)
    u, v = uv[..., :GMLP_WIDTH], uv[..., GMLP_WIDTH:]
    v = layernorm(v, ln_g, ln_b)
    vg = v.reshape(B, S // CHUNK, CHUNK, N_GROUPS_B, GROUP_DIM_B)
    tril = jnp.tril(jnp.ones((CHUNK, CHUNK), dtype=bool))
    ws = jnp.where(tril[None], w_s, jnp.zeros_like(w_s))
    mixed = jnp.einsum('gts,bcsgd->bctgd', ws, vg) + jnp.transpose(b_s)[None, None, :, :, None]
    return u * mixed.reshape(B, S, GMLP_WIDTH)


def setup_inputs(seed: int = 0) -> dict:
    key = jax.random.key(seed)
    ks = jax.random.split(key, 24)
    nrm = lambda k, shape, s: jax.random.normal(k, shape, jnp.float32) * s
    L = DEPTH
    return {
        "x": nrm(ks[0], (BATCH, SEQ, D_MODEL), 1.0),
        "p": nrm(ks[1], (DEPTH, BATCH, SEQ, D_PLE), 1.0),
        "w_in": nrm(ks[2], (L, D_MODEL, IN_COLS), D_MODEL ** -0.5),
        "q_norm_g": 1.0 + nrm(ks[3], (L, HEAD_DIM_A), 0.02),
        "k_norm_g": 1.0 + nrm(ks[4], (L, HEAD_DIM_A), 0.02),
        "rel_bias": nrm(ks[5], (N_BUCKETS, N_HEADS_A), 0.5),
        "sgu_ln_g": 1.0 + nrm(ks[6], (L, GMLP_WIDTH), 0.02),
        "sgu_ln_b": nrm(ks[7], (L, GMLP_WIDTH), 0.02),
        "sgu_w": nrm(ks[8], (L, N_GROUPS_B, CHUNK, CHUNK), CHUNK ** -0.5),
        "sgu_b": 1.0 + nrm(ks[9], (L, N_GROUPS_B, CHUNK), 0.02),
        "w_branch_a": nrm(ks[10], (L, A_WIDTH, D_MODEL), A_WIDTH ** -0.5),
        "w_branch_b": nrm(ks[11], (L, GMLP_WIDTH, D_MODEL), GMLP_WIDTH ** -0.5),
        "w_out": nrm(ks[12], (L, D_MODEL, D_MODEL), D_MODEL ** -0.5),
        "norm_mix_g": 1.0 + nrm(ks[13], (L, D_MODEL), 0.02),
        "norm_ffn_g": 1.0 + nrm(ks[14], (L, D_MODEL), 0.02),
        "w_gate_ffn": nrm(ks[15], (L, D_MODEL, D_FF), D_MODEL ** -0.5),
        "w_up_ffn": nrm(ks[16], (L, D_MODEL, D_FF), D_MODEL ** -0.5),
        "w_down_ffn": nrm(ks[17], (L, D_FF, D_MODEL), D_FF ** -0.5),
        "w_ple": nrm(ks[18], (L, D_PLE, D_MODEL), D_PLE ** -0.5),
        "ple_norm_g": 1.0 + nrm(ks[19], (L, D_MODEL), 0.02),
        "w_ple_gate": nrm(ks[20], (L, D_MODEL, D_MODEL), D_MODEL ** -0.5),
        "norm_ple_g": 1.0 + nrm(ks[21], (L, D_MODEL), 0.02),
    }


def reference(x, p, w_in, q_norm_g, k_norm_g, rel_bias, sgu_ln_g, sgu_ln_b, sgu_w, sgu_b, w_branch_a, w_branch_b, w_out, norm_mix_g, norm_ffn_g, w_gate_ffn, w_up_ffn, w_down_ffn, w_ple, ple_norm_g, w_ple_gate, norm_ple_g):
    B, S = x.shape[0], x.shape[1]
    offsets = []
    acc = 0
    for sz in SPLIT_SIZES[:-1]:
        acc += sz
        offsets.append(acc)
    for i in range(DEPTH):
        h = rmsnorm(x, norm_mix_g[i])
        proj = h @ w_in[i]
        q, k, v, q_idx, k_idx, w_idx, uv, g_a, g_b = jnp.split(proj, offsets, axis=-1)
        q = rmsnorm(q.reshape(B, S, N_HEADS_A, HEAD_DIM_A), q_norm_g[i])
        k = rmsnorm(k, k_norm_g[i])
        q_idx = q_idx.reshape(B, S, N_IDX_HEADS, IDX_DIM)
        y_a = dsa_attention(q, k, v, q_idx, k_idx, w_idx, rel_bias)
        y_b = chunked_sgu(uv, sgu_ln_g[i], sgu_ln_b[i], sgu_w[i], sgu_b[i])
        merged = jax.nn.sigmoid(g_a) * (y_a @ w_branch_a[i]) + jax.nn.sigmoid(g_b) * (y_b @ w_branch_b[i])
        x = x + merged @ w_out[i]
        h2 = rmsnorm(x, norm_ffn_g[i])
        x = x + (jax.nn.silu(h2 @ w_gate_ffn[i]) * (h2 @ w_up_ffn[i])) @ w_down_ffn[i]
        pe = rmsnorm(p[i] @ w_ple[i], ple_norm_g[i])
        gate = jax.nn.sigmoid(rmsnorm(x, norm_ple_g[i]) @ w_ple_gate[i])
        x = x + gate * pe
    return x
```

```python
import functools
import math

import numpy as np
import jax
import jax.numpy as jnp
from jax import lax
from jax.experimental import pallas as pl
from jax.experimental.pallas import tpu as pltpu

F32 = jnp.float32
BF16 = jnp.bfloat16
I32 = jnp.int32

N_HEADS = 16
HEAD_DIM = 128
N_IDX_HEADS = 32
IDX_DIM = 128
TOPK_MAX = 256
Q_BLOCK = 128
N_BUCKETS = 32
MAX_DISTANCE = 128
GMLP_WIDTH = 2048
N_GROUPS = 8
GROUP_DIM = GMLP_WIDTH // N_GROUPS
CHUNK = 128
EPS = 1e-6

LANES = 128
VMEM_LIMIT = 56 * 1024 * 1024
NEG = -1e30
INT_MIN = -2 ** 31


def _params(sem):
    return pltpu.CompilerParams(dimension_semantics=sem, vmem_limit_bytes=VMEM_LIMIT)


def _rmsnorm_kernel(x_ref, g_ref, o_ref):
    x = x_ref[...]
    ms = jnp.mean(x * x, axis=-1, keepdims=True)
    o_ref[...] = (x * lax.rsqrt(ms + EPS) * g_ref[...]).astype(o_ref.dtype)


def _rmsnorm(x, g, tm=256):
    m, d = x.shape
    return pl.pallas_call(
        _rmsnorm_kernel,
        grid=(m // tm,),
        in_specs=[pl.BlockSpec((tm, d), lambda i: (i, 0)),
                  pl.BlockSpec((1, d), lambda i: (0, 0))],
        out_specs=pl.BlockSpec((tm, d), lambda i: (i, 0)),
        out_shape=jax.ShapeDtypeStruct((m, d), BF16),
        compiler_params=_params(("parallel",)),
        name="rmsnorm",
    )(x, g.reshape(1, d))


def _mm_kernel(*refs, n_w, n_x, epi):
    a_ref = refs[0]
    w_refs = refs[1:1 + n_w]
    x_refs = refs[1 + n_w:1 + n_w + n_x]
    o_ref = refs[1 + n_w + n_x]
    a = a_ref[...]
    accs = [jnp.dot(a, w[...], preferred_element_type=F32) for w in w_refs]
    o_ref[...] = epi(accs, [x[...] for x in x_refs]).astype(o_ref.dtype)


def _matmul(a, ws, extras, epi, out_dtype, tm, tn, name):
    m, k = a.shape
    n = ws[0].shape[1]
    kern = functools.partial(_mm_kernel, n_w=len(ws), n_x=len(extras), epi=epi)
    return pl.pallas_call(
        kern,
        grid=(m // tm, n // tn),
        in_specs=([pl.BlockSpec((tm, k), lambda i, j: (i, 0))]
                  + [pl.BlockSpec((k, tn), lambda i, j: (0, j)) for _ in ws]
                  + [pl.BlockSpec((tm, tn), lambda i, j: (i, j)) for _ in extras]),
        out_specs=pl.BlockSpec((tm, tn), lambda i, j: (i, j)),
        out_shape=jax.ShapeDtypeStruct((m, n), out_dtype),
        compiler_params=_params(("parallel", "arbitrary")),
        name=name,
    )(a, *ws, *extras)


def _mm_ksplit_kernel(a_ref, w_ref, r_ref, o_ref):
    part = jnp.dot(a_ref[...], w_ref[...], preferred_element_type=F32)

    @pl.when(pl.program_id(2) == 0)
    def _():
        o_ref[...] = r_ref[...] + part

    @pl.when(pl.program_id(2) != 0)
    def _():
        o_ref[...] += part


def _matmul_ksplit_residual(a, w, res, tm, tn, tk, name):
    m, k = a.shape
    n = w.shape[1]
    return pl.pallas_call(
        _mm_ksplit_kernel,
        grid=(m // tm, n // tn, k // tk),
        in_specs=[pl.BlockSpec((tm, tk), lambda i, j, kk: (i, kk)),
                  pl.BlockSpec((tk, tn), lambda i, j, kk: (kk, j)),
                  pl.BlockSpec((tm, tn), lambda i, j, kk: (i, j))],
        out_specs=pl.BlockSpec((tm, tn), lambda i, j, kk: (i, j)),
        out_shape=jax.ShapeDtypeStruct((m, n), F32),
        compiler_params=_params(("parallel", "arbitrary", "arbitrary")),
        name=name,
    )(a, w, res)


def _qproj_kernel(a_ref, w_ref, g_ref, o_ref, *, heads):
    acc = jnp.dot(a_ref[...], w_ref[...], preferred_element_type=F32)
    g = g_ref[...]
    for hh in range(heads):
        blk = acc[:, hh * HEAD_DIM:(hh + 1) * HEAD_DIM]
        ms = jnp.mean(blk * blk, axis=-1, keepdims=True)
        o_ref[hh] = (blk * lax.rsqrt(ms + EPS) * g).astype(o_ref.dtype)


def _qproj(h, w_q, g, tm=1024, tn=1024):
    m, k = h.shape
    heads = tn // HEAD_DIM
    return pl.pallas_call(
        functools.partial(_qproj_kernel, heads=heads),
        grid=(m // tm, (N_HEADS * HEAD_DIM) // tn),
        in_specs=[pl.BlockSpec((tm, k), lambda i, j: (i, 0)),
                  pl.BlockSpec((k, tn), lambda i, j: (0, j)),
                  pl.BlockSpec((1, HEAD_DIM), lambda i, j: (0, 0))],
        out_specs=pl.BlockSpec((heads, tm, HEAD_DIM), lambda i, j: (j, i, 0)),
        out_shape=jax.ShapeDtypeStruct((N_HEADS, m, HEAD_DIM), BF16),
        compiler_params=_params(("parallel", "arbitrary")),
        name="q_proj",
    )(h, w_q, g.reshape(1, HEAD_DIM))


def _small_kernel(a_ref, w_ref, g_ref, k_ref, v_ref, ki_ref, wi_ref):
    acc = jnp.dot(a_ref[...], w_ref[...], preferred_element_type=F32)
    kk = acc[:, 0:LANES]
    ms = jnp.mean(kk * kk, axis=-1, keepdims=True)
    k_ref[...] = (kk * lax.rsqrt(ms + EPS) * g_ref[...]).astype(k_ref.dtype)
    v_ref[...] = acc[:, LANES:2 * LANES].astype(v_ref.dtype)
    ki_ref[...] = acc[:, 2 * LANES:3 * LANES].astype(ki_ref.dtype)
    wi_ref[...] = acc[:, 3 * LANES:4 * LANES]


def _small_proj(h, w_small, g, tm=1024):
    m, k = h.shape
    row = lambda i: (i, 0)
    return pl.pallas_call(
        _small_kernel,
        grid=(m // tm,),
        in_specs=[pl.BlockSpec((tm, k), row),
                  pl.BlockSpec((k, 4 * LANES), lambda i: (0, 0)),
                  pl.BlockSpec((1, LANES), lambda i: (0, 0))],
        out_specs=[pl.BlockSpec((tm, LANES), row)] * 4,
        out_shape=[jax.ShapeDtypeStruct((m, LANES), BF16)] * 3
                  + [jax.ShapeDtypeStruct((m, LANES), F32)],
        compiler_params=_params(("parallel",)),
        name="kv_idx_proj",
    )(h, w_small, g.reshape(1, LANES))


def _attn_kernel(qi_ref, kidx_ref, w_ref, q_ref, k_ref, v_ref, btab_ref, o_ref,
                 key_ref, m_ref, l_ref, acc_ref, *, k_sel, idx_scale, attn_scale):
    j = pl.program_id(1)
    nt = (((1,), (1,)), ((), ()))
    sc = 2 * LANES
    n_chunks = (j + 2) // 2

    w = w_ref[...]
    row = lax.broadcasted_iota(I32, (Q_BLOCK, sc), 0)
    col = lax.broadcasted_iota(I32, (Q_BLOCK, sc), 1)

    def score_chunk(c, carry):
        start = pl.multiple_of(c * sc, sc)
        ks = kidx_ref[pl.ds(start, sc), :]
        acc = jnp.zeros((Q_BLOCK, sc), F32)
        for h in range(N_IDX_HEADS):
            d = lax.dot_general(qi_ref[:, h * IDX_DIM:(h + 1) * IDX_DIM], ks, nt,
                                preferred_element_type=F32)
            acc = acc + jnp.maximum(d, 0.0) * w[:, h:h + 1]
        s = acc * idx_scale
        causal = (c * sc + col) <= (j * Q_BLOCK + row)
        s = jnp.where(causal, s, -jnp.inf)
        bits = lax.bitcast_convert_type(s, I32)
        key_ref[:, pl.ds(start, sc)] = bits ^ ((bits >> 31) & 0x7FFFFFFF)
        return carry

    lax.fori_loop(0, n_chunks, score_chunk, 0)

    def bit_body(i, tu):
        cu = tu | jnp.left_shift(jnp.int32(1), 31 - i)
        cs = cu ^ INT_MIN

        def cnt_body(c, a):
            start = pl.multiple_of(c * sc, sc)
            kk = key_ref[:, pl.ds(start, sc)]
            return (a + jnp.where(kk[:, :LANES] >= cs, 1, 0)
                    + jnp.where(kk[:, LANES:] >= cs, 1, 0))

        a = lax.fori_loop(0, n_chunks, cnt_body, jnp.zeros((Q_BLOCK, LANES), I32))
        cnt = jnp.sum(a, axis=1, keepdims=True)
        return jnp.where(cnt >= k_sel, cu, tu)

    tu = lax.fori_loop(0, 32, bit_body, jnp.zeros((Q_BLOCK, 1), I32))
    thr = tu ^ INT_MIN

    m_ref[...] = jnp.full(m_ref.shape, NEG, F32)
    l_ref[...] = jnp.zeros(l_ref.shape, F32)
    acc_ref[...] = jnp.zeros(acc_ref.shape, F32)
    q2 = q_ref[...].reshape(N_HEADS * Q_BLOCK, HEAD_DIM)

    def tile_body(t, carry):
        start = pl.multiple_of(t * LANES, LANES)
        kt = k_ref[pl.ds(start, LANES), :]
        vt = v_ref[pl.ds(start, LANES), :]
        s = lax.dot_general(q2, kt, nt, preferred_element_type=F32) * attn_scale
        sel = key_ref[:, pl.ds(start, LANES)] >= thr
        maskb = jnp.where(sel, 0.0, NEG)
        ti = jnp.clip(t - (j - 2), 0, 2)
        for h in range(N_HEADS):
            sh = s[h * Q_BLOCK:(h + 1) * Q_BLOCK] + btab_ref[ti, h] + maskb
            m_old = m_ref[h]
            m_new = jnp.maximum(m_old, jnp.max(sh, axis=1, keepdims=True))
            alpha = jnp.exp(m_old - m_new)
            p = jnp.exp(sh - m_new)
            l_ref[h] = alpha * l_ref[h] + jnp.sum(p, axis=1, keepdims=True)
            acc_ref[h] = alpha * acc_ref[h] + jnp.dot(p.astype(BF16), vt,
                                                      preferred_element_type=F32)
            m_ref[h] = m_new
        return carry

    lax.fori_loop(0, j + 1, tile_body, 0)

    for h in range(N_HEADS):
        o_ref[:, h * HEAD_DIM:(h + 1) * HEAD_DIM] = (acc_ref[h] / l_ref[h]).astype(o_ref.dtype)


def _t5_bucket_static(n):
    max_exact = N_BUCKETS // 2
    nf = np.maximum(n, 1).astype(np.float32)
    large = max_exact + (np.log(nf / np.float32(max_exact)) / np.float32(math.log(MAX_DISTANCE / max_exact))
                         * np.float32(N_BUCKETS - max_exact)).astype(np.int32)
    large = np.minimum(large, N_BUCKETS - 1)
    return np.where(n < max_exact, n, large)


def _bias_tables(rel_bias, seq):
    buckets = _t5_bucket_static(np.arange(seq, dtype=np.int32))
    far = buckets[Q_BLOCK + 1:]
    assert (far == far[0]).all()
    ql = np.arange(Q_BLOCK)[:, None]
    c = np.arange(2 * LANES)[None, :]
    dist = Q_BLOCK + ql - c
    near_idx = buckets[np.clip(dist, 0, seq - 1)]
    near = jnp.transpose(rel_bias[near_idx], (2, 0, 1)).astype(F32)
    near = jnp.where(jnp.asarray(dist >= 0)[None], near, NEG)
    far_t = jnp.broadcast_to(rel_bias[int(far[0])].astype(F32)[:, None, None],
                             (N_HEADS, Q_BLOCK, LANES))
    return jnp.stack([far_t, near[:, :, :LANES], near[:, :, LANES:]])


def _attention(qi, kidx, widx, q_hm, k, v, btab, batch, seq):
    m = batch * seq
    n_blk = seq // Q_BLOCK
    k_sel = min(TOPK_MAX, seq // 4)
    kern = functools.partial(
        _attn_kernel, k_sel=k_sel,
        idx_scale=IDX_DIM ** -0.5 * N_IDX_HEADS ** -0.5,
        attn_scale=HEAD_DIM ** -0.5)
    qrow = lambda b, j: (b * n_blk + j, 0)
    brow = lambda b, j: (b, 0)
    return pl.pallas_call(
        kern,
        grid=(batch, n_blk),
        in_specs=[pl.BlockSpec((Q_BLOCK, N_IDX_HEADS * IDX_DIM), qrow),
                  pl.BlockSpec((seq, IDX_DIM), brow),
                  pl.BlockSpec((Q_BLOCK, LANES), qrow),
                  pl.BlockSpec((N_HEADS, Q_BLOCK, HEAD_DIM), lambda b, j: (0, b * n_blk + j, 0)),
                  pl.BlockSpec((seq, HEAD_DIM), brow),
                  pl.BlockSpec((seq, HEAD_DIM), brow),
                  pl.BlockSpec((3, N_HEADS, Q_BLOCK, LANES), lambda b, j: (0, 0, 0, 0))],
        out_specs=pl.BlockSpec((Q_BLOCK, N_HEADS * HEAD_DIM), qrow),
        out_shape=jax.ShapeDtypeStruct((m, N_HEADS * HEAD_DIM), BF16),
        scratch_shapes=[pltpu.VMEM((Q_BLOCK, seq), I32),
                        pltpu.VMEM((N_HEADS, Q_BLOCK, 1), F32),
                        pltpu.VMEM((N_HEADS, Q_BLOCK, 1), F32),
                        pltpu.VMEM((N_HEADS, Q_BLOCK, HEAD_DIM), F32)],
        compiler_params=_params(("parallel", "arbitrary")),
        name="dsa_attention",
    )(qi, kidx, widx, q_hm, k, v, btab)


def _sgu_kernel(uv_ref, g_ref, b_ref, ws_ref, bs_ref, o_ref):
    uv = jax.nn.gelu(uv_ref[...])
    u = uv[:, :GMLP_WIDTH]
    v = uv[:, GMLP_WIDTH:]
    mu = jnp.mean(v, axis=-1, keepdims=True)
    vc = v - mu
    var = jnp.mean(vc * vc, axis=-1, keepdims=True)
    vn = (vc * lax.rsqrt(var + EPS) * g_ref[...] + b_ref[...]).astype(BF16)
    r = lax.broadcasted_iota(I32, (CHUNK, CHUNK), 0)
    c = lax.broadcasted_iota(I32, (CHUNK, CHUNK), 1)
    tril = c <= r
    bs = bs_ref[...]
    for g in range(N_GROUPS):
        ws = jnp.where(tril, ws_ref[g], 0.0).astype(BF16)
        mixed = jnp.dot(ws, vn[:, g * GROUP_DIM:(g + 1) * GROUP_DIM],
                        preferred_element_type=F32) + bs[:, g:g + 1]
        o_ref[:, g * GROUP_DIM:(g + 1) * GROUP_DIM] = (
            u[:, g * GROUP_DIM:(g + 1) * GROUP_DIM] * mixed).astype(o_ref.dtype)


def _sgu(uv, ln_g, ln_b, w_s, b_s):
    m = uv.shape[0]
    bs_t = jnp.zeros((CHUNK, LANES), F32).at[:, :N_GROUPS].set(jnp.transpose(b_s))
    return pl.pallas_call(
        _sgu_kernel,
        grid=(m // CHUNK,),
        in_specs=[pl.BlockSpec((CHUNK, 2 * GMLP_WIDTH), lambda i: (i, 0)),
                  pl.BlockSpec((1, GMLP_WIDTH), lambda i: (0, 0)),
                  pl.BlockSpec((1, GMLP_WIDTH), lambda i: (0, 0)),
                  pl.BlockSpec((N_GROUPS, CHUNK, CHUNK), lambda i: (0, 0, 0)),
                  pl.BlockSpec((CHUNK, LANES), lambda i: (0, 0))],
        out_specs=pl.BlockSpec((CHUNK, GMLP_WIDTH), lambda i: (i, 0)),
        out_shape=jax.ShapeDtypeStruct((m, GMLP_WIDTH), BF16),
        compiler_params=_params(("parallel",)),
        name="chunked_sgu",
    )(uv, ln_g.reshape(1, -1), ln_b.reshape(1, -1), w_s, bs_t)


def _merge_kernel(h_ref, ya_ref, yb_ref, wga_ref, wgb_ref, wa_ref, wb_ref, o_ref):
    h = h_ref[...]
    ga = jax.nn.sigmoid(jnp.dot(h, wga_ref[...], preferred_element_type=F32))
    a = jnp.dot(ya_ref[...], wa_ref[...], preferred_element_type=F32)
    out = ga * a
    gb = jax.nn.sigmoid(jnp.dot(h, wgb_ref[...], preferred_element_type=F32))
    b = jnp.dot(yb_ref[...], wb_ref[...], preferred_element_type=F32)
    o_ref[...] = (out + gb * b).astype(o_ref.dtype)


def _merge(h, ya, yb, wga, wgb, wa, wb, tm=512, tn=512):
    m, d = h.shape
    n = wga.shape[1]
    ka, kb = ya.shape[1], yb.shape[1]
    row = lambda i, j: (i, 0)
    colw = lambda i, j: (0, j)
    return pl.pallas_call(
        _merge_kernel,
        grid=(m // tm, n // tn),
        in_specs=[pl.BlockSpec((tm, d), row), pl.BlockSpec((tm, ka), row), pl.BlockSpec((tm, kb), row),
                  pl.BlockSpec((d, tn), colw), pl.BlockSpec((d, tn), colw),
                  pl.BlockSpec((ka, tn), colw), pl.BlockSpec((kb, tn), colw)],
        out_specs=pl.BlockSpec((tm, tn), lambda i, j: (i, j)),
        out_shape=jax.ShapeDtypeStruct((m, n), BF16),
        compiler_params=_params(("parallel", "arbitrary")),
        name="branch_merge",
    )(h, ya, yb, wga, wgb, wa, wb)


def _ple_kernel(p_ref, w_ref, g_ref, o_ref):
    acc = jnp.dot(p_ref[...].astype(BF16), w_ref[...], preferred_element_type=F32)
    ms = jnp.mean(acc * acc, axis=-1, keepdims=True)
    o_ref[...] = (acc * lax.rsqrt(ms + EPS) * g_ref[...]).astype(o_ref.dtype)


def _ple(p, w, g, tm=256):
    m, dp = p.shape
    d = w.shape[1]
    return pl.pallas_call(
        _ple_kernel,
        grid=(m // tm,),
        in_specs=[pl.BlockSpec((tm, dp), lambda i: (i, 0)),
                  pl.BlockSpec((dp, d), lambda i: (0, 0)),
                  pl.BlockSpec((1, d), lambda i: (0, 0))],
        out_specs=pl.BlockSpec((tm, d), lambda i: (i, 0)),
        out_shape=jax.ShapeDtypeStruct((m, d), F32),
        compiler_params=_params(("parallel",)),
        name="ple_embed",
    )(p, w, g.reshape(1, d))


def kernel(x, p, w_in, q_norm_g, k_norm_g, rel_bias, sgu_ln_g, sgu_ln_b, sgu_w, sgu_b, w_branch_a, w_branch_b, w_out, norm_mix_g, norm_ffn_g, w_gate_ffn, w_up_ffn, w_down_ffn, w_ple, ple_norm_g, w_ple_gate, norm_ple_g):
    batch, seq, d_model = x.shape
    depth = w_in.shape[0]
    m = batch * seq
    a_width = N_HEADS * HEAD_DIM
    sizes = (a_width, HEAD_DIM, HEAD_DIM, N_IDX_HEADS * IDX_DIM, IDX_DIM, N_IDX_HEADS,
             2 * GMLP_WIDTH, d_model, d_model)
    offs = np.concatenate([[0], np.cumsum(sizes)])
    btab = _bias_tables(rel_bias, seq)

    xf = x.reshape(m, d_model)
    for i in range(depth):
        seg = lambda s: w_in[i][:, int(offs[s]):int(offs[s + 1])].astype(BF16)
        w_q, w_k, w_v, w_qi, w_ki, w_wi, w_uv, w_ga, w_gb = [seg(s) for s in range(9)]
        w_small = jnp.concatenate(
            [w_k, w_v, w_ki, jnp.pad(w_wi, ((0, 0), (0, LANES - N_IDX_HEADS)))], axis=1)

        h = _rmsnorm(xf, norm_mix_g[i])
        q_hm = _qproj(h, w_q, q_norm_g[i])
        k_, v_, kidx, widx = _small_proj(h, w_small, k_norm_g[i])
        qi = _matmul(h, [w_qi], [], lambda a, e: a[0], BF16, 1024, 1024, "q_idx_proj")
        uv = _matmul(h, [w_uv], [], lambda a, e: a[0], F32, 1024, 1024, "uv_proj")
        y_a = _attention(qi, kidx, widx, q_hm, k_, v_, btab, batch, seq)
        y_b = _sgu(uv, sgu_ln_g[i], sgu_ln_b[i], sgu_w[i], sgu_b[i])
        merged = _merge(h, y_a, y_b, w_ga, w_gb,
                        w_branch_a[i].astype(BF16), w_branch_b[i].astype(BF16))
        x1 = _matmul(merged, [w_out[i].astype(BF16)], [xf],
                     lambda a, e: e[0] + a[0], F32, 512, 1024, "out_proj")

        h2 = _rmsnorm(x1, norm_ffn_g[i])
        t = _matmul(h2, [w_gate_ffn[i].astype(BF16), w_up_ffn[i].astype(BF16)], [],
                    lambda a, e: jax.nn.silu(a[0]) * a[1], BF16, 1024, 256, "ffn_gate_up")
        d_ff = t.shape[1]
        x2 = _matmul_ksplit_residual(t, w_down_ffn[i].astype(BF16), x1, 512, 1024, d_ff // 2, "ffn_down")

        pe = _ple(p[i].reshape(m, -1), w_ple[i].astype(BF16), ple_norm_g[i])
        h3 = _rmsnorm(x2, norm_ple_g[i])
        xf = _matmul(h3, [w_ple_gate[i].astype(BF16)], [x2, pe],
                     lambda a, e: e[0] + jax.nn.sigmoid(a[0]) * e[1], F32, 512, 1024, "ple_gate")
    return xf.reshape(batch, seq, d_model)
```

```python
import functools
import math

import numpy as np
import jax
import jax.numpy as jnp
from jax import lax
from jax.experimental import pallas as pl
from jax.experimental.pallas import tpu as pltpu

F32 = jnp.float32
BF16 = jnp.bfloat16
I32 = jnp.int32

N_HEADS = 16
HEAD_DIM = 128
N_IDX_HEADS = 32
IDX_DIM = 128
TOPK_MAX = 256
Q_BLOCK = 128
N_BUCKETS = 32
MAX_DISTANCE = 128
GMLP_WIDTH = 2048
N_GROUPS = 8
GROUP_DIM = GMLP_WIDTH // N_GROUPS
CHUNK = 128
EPS = 1e-6

LANES = 128
SUBLANES = 8
VMEM_LIMIT = 56 * 1024 * 1024
NEG = -1e30
INT_MIN = -2 ** 31
LOG2E = math.log2(math.e)

SCORE_CHUNK = 2 * LANES
FAR_CHUNK = 4 * LANES
NEAR_KEYS = 2 * Q_BLOCK


def _params(sem):
    return pltpu.CompilerParams(dimension_semantics=sem, vmem_limit_bytes=VMEM_LIMIT)


def _rmsnorm_kernel(x_ref, g_ref, o_ref):
    x = x_ref[...]
    ms = jnp.mean(x * x, axis=-1, keepdims=True)
    o_ref[...] = (x * lax.rsqrt(ms + EPS) * g_ref[...]).astype(o_ref.dtype)


def _rmsnorm(x, g, tm=256):
    m, d = x.shape
    return pl.pallas_call(
        _rmsnorm_kernel,
        grid=(m // tm,),
        in_specs=[pl.BlockSpec((tm, d), lambda i: (i, 0)),
                  pl.BlockSpec((1, d), lambda i: (0, 0))],
        out_specs=pl.BlockSpec((tm, d), lambda i: (i, 0)),
        out_shape=jax.ShapeDtypeStruct((m, d), BF16),
        compiler_params=_params(("parallel",)),
        name="rmsnorm",
    )(x, g.reshape(1, d))


def _mm_kernel(*refs, n_w, n_x, epi):
    a_ref = refs[0]
    w_refs = refs[1:1 + n_w]
    x_refs = refs[1 + n_w:1 + n_w + n_x]
    o_ref = refs[1 + n_w + n_x]
    a = a_ref[...]
    accs = [jnp.dot(a, w[...], preferred_element_type=F32) for w in w_refs]
    o_ref[...] = epi(accs, [x[...] for x in x_refs]).astype(o_ref.dtype)


def _matmul(a, ws, extras, epi, out_dtype, tm, tn, name):
    m, k = a.shape
    n = ws[0].shape[1]
    kern = functools.partial(_mm_kernel, n_w=len(ws), n_x=len(extras), epi=epi)
    return pl.pallas_call(
        kern,
        grid=(m // tm, n // tn),
        in_specs=([pl.BlockSpec((tm, k), lambda i, j: (i, 0))]
                  + [pl.BlockSpec((k, tn), lambda i, j: (0, j)) for _ in ws]
                  + [pl.BlockSpec((tm, tn), lambda i, j: (i, j)) for _ in extras]),
        out_specs=pl.BlockSpec((tm, tn), lambda i, j: (i, j)),
        out_shape=jax.ShapeDtypeStruct((m, n), out_dtype),
        compiler_params=_params(("parallel", "arbitrary")),
        name=name,
    )(a, *ws, *extras)


def _mm_ksplit_kernel(a_ref, w_ref, r_ref, o_ref):
    part = jnp.dot(a_ref[...], w_ref[...], preferred_element_type=F32)

    @pl.when(pl.program_id(2) == 0)
    def _():
        o_ref[...] = r_ref[...] + part

    @pl.when(pl.program_id(2) != 0)
    def _():
        o_ref[...] += part


def _matmul_ksplit_residual(a, w, res, tm, tn, tk, name):
    m, k = a.shape
    n = w.shape[1]
    return pl.pallas_call(
        _mm_ksplit_kernel,
        grid=(m // tm, n // tn, k // tk),
        in_specs=[pl.BlockSpec((tm, tk), lambda i, j, kk: (i, kk)),
                  pl.BlockSpec((tk, tn), lambda i, j, kk: (kk, j)),
                  pl.BlockSpec((tm, tn), lambda i, j, kk: (i, j))],
        out_specs=pl.BlockSpec((tm, tn), lambda i, j, kk: (i, j)),
        out_shape=jax.ShapeDtypeStruct((m, n), F32),
        compiler_params=_params(("parallel", "arbitrary", "arbitrary")),
        name=name,
    )(a, w, res)


def _headproj_kernel(*refs, heads, norm):
    a_ref, w_ref = refs[0], refs[1]
    o_ref = refs[-1]
    acc = jnp.dot(a_ref[...], w_ref[...], preferred_element_type=F32)
    for hh in range(heads):
        blk = acc[:, hh * LANES:(hh + 1) * LANES]
        if norm:
            ms = jnp.mean(blk * blk, axis=-1, keepdims=True)
            blk = blk * lax.rsqrt(ms + EPS) * refs[2][...]
        o_ref[hh] = blk.astype(o_ref.dtype)


def _headproj(h, w, g, name, tm=1024, tn=1024):
    m, k = h.shape
    n_heads = w.shape[1] // LANES
    heads = tn // LANES
    norm = g is not None
    extra_specs = [pl.BlockSpec((1, LANES), lambda i, j: (0, 0))] if norm else []
    extra_args = [g.reshape(1, LANES)] if norm else []
    return pl.pallas_call(
        functools.partial(_headproj_kernel, heads=heads, norm=norm),
        grid=(m // tm, w.shape[1] // tn),
        in_specs=[pl.BlockSpec((tm, k), lambda i, j: (i, 0)),
                  pl.BlockSpec((k, tn), lambda i, j: (0, j))] + extra_specs,
        out_specs=pl.BlockSpec((heads, tm, LANES), lambda i, j: (j, i, 0)),
        out_shape=jax.ShapeDtypeStruct((n_heads, m, LANES), BF16),
        compiler_params=_params(("parallel", "arbitrary")),
        name=name,
    )(h, w, *extra_args)


def _small_kernel(a_ref, w_ref, g_ref, k_ref, v_ref, ki_ref, wi_ref):
    acc = jnp.dot(a_ref[...], w_ref[...], preferred_element_type=F32)
    kk = acc[:, 0:LANES]
    ms = jnp.mean(kk * kk, axis=-1, keepdims=True)
    k_ref[...] = (kk * lax.rsqrt(ms + EPS) * g_ref[...]).astype(k_ref.dtype)
    v_ref[...] = acc[:, LANES:2 * LANES].astype(v_ref.dtype)
    ki_ref[...] = acc[:, 2 * LANES:3 * LANES].astype(ki_ref.dtype)
    wi_ref[...] = acc[:, 3 * LANES:4 * LANES]


def _small_proj(h, w_small, g, tm=1024):
    m, k = h.shape
    row = lambda i: (i, 0)
    return pl.pallas_call(
        _small_kernel,
        grid=(m // tm,),
        in_specs=[pl.BlockSpec((tm, k), row),
                  pl.BlockSpec((k, 4 * LANES), lambda i: (0, 0)),
                  pl.BlockSpec((1, LANES), lambda i: (0, 0))],
        out_specs=[pl.BlockSpec((tm, LANES), row)] * 4,
        out_shape=[jax.ShapeDtypeStruct((m, LANES), BF16)] * 3
                  + [jax.ShapeDtypeStruct((m, LANES), F32)],
        compiler_params=_params(("parallel",)),
        name="kv_idx_proj",
    )(h, w_small, g.reshape(1, LANES))


def _attn_kernel(qi_ref, kidx_ref, wt_ref, q_ref, k_ref, vt_ref, near_ref, cvec_ref, o_ref,
                 key_ref, m_ref, l_ref, acc_ref, p_ref, alpha_ref,
                 *, k_sel, idx_scale, logit_scale):
    j = pl.program_id(1)
    nt = (((1,), (1,)), ((), ()))
    n_chunks = (j + 2) // 2

    key_row = lax.broadcasted_iota(I32, (SCORE_CHUNK, Q_BLOCK), 0)
    q_col = lax.broadcasted_iota(I32, (SCORE_CHUNK, Q_BLOCK), 1)

    def score_chunk(c, carry):
        start = pl.multiple_of(c * SCORE_CHUNK, SCORE_CHUNK)
        kc = kidx_ref[pl.ds(start, SCORE_CHUNK), :]
        acc = jnp.zeros((SCORE_CHUNK, Q_BLOCK), F32)
        for hp in range(N_IDX_HEADS // 2):
            qpair = qi_ref[2 * hp:2 * hp + 2].reshape(2 * Q_BLOCK, IDX_DIM)
            d = lax.dot_general(kc, qpair, nt, preferred_element_type=F32)
            acc = acc + jnp.maximum(d[:, :Q_BLOCK], 0.0) * wt_ref[2 * hp:2 * hp + 1, :]
            acc = acc + jnp.maximum(d[:, Q_BLOCK:], 0.0) * wt_ref[2 * hp + 1:2 * hp + 2, :]
        s = acc * idx_scale
        causal = (start + key_row) <= (j * Q_BLOCK + q_col)
        s = jnp.where(causal, s, -jnp.inf)
        bits = lax.bitcast_convert_type(s, I32)
        key_ref[pl.ds(start, SCORE_CHUNK), :] = bits ^ ((bits >> 31) & 0x7FFFFFFF)
        return carry

    lax.fori_loop(0, n_chunks, score_chunk, 0)

    def bit_body(i, tu):
        cu = tu | jnp.left_shift(jnp.int32(1), 31 - i)
        cs = cu ^ INT_MIN

        def cnt_body(c, a):
            start = pl.multiple_of(c * SCORE_CHUNK, SCORE_CHUNK)
            ge = jnp.where(key_ref[pl.ds(start, SCORE_CHUNK), :] >= cs, 1, 0)
            return a + jnp.sum(ge.reshape(SCORE_CHUNK // SUBLANES, SUBLANES, Q_BLOCK), axis=0)

        a = lax.fori_loop(0, n_chunks, cnt_body, jnp.zeros((SUBLANES, Q_BLOCK), I32))
        cnt = jnp.sum(a, axis=0, keepdims=True)
        return jnp.where(cnt >= k_sel, cu, tu)

    tu = lax.fori_loop(0, 32, bit_body, jnp.zeros((1, Q_BLOCK), I32))
    thr = tu ^ INT_MIN

    m_ref[...] = jnp.full(m_ref.shape, NEG, F32)
    l_ref[...] = jnp.zeros(l_ref.shape, F32)
    acc_ref[...] = jnp.zeros(acc_ref.shape, F32)
    q2 = q_ref[...].reshape(N_HEADS * Q_BLOCK, HEAD_DIM)

    def attend(start, n_keys, maskb, bias_of_head):
        kt = k_ref[pl.ds(start, n_keys), :]
        s = lax.dot_general(kt, q2, nt, preferred_element_type=F32)
        for h in range(N_HEADS):
            hs = slice(h * Q_BLOCK, (h + 1) * Q_BLOCK)
            t = s[:, hs] * logit_scale + bias_of_head(h) + maskb
            m_old = m_ref[:, hs]
            m_new = jnp.maximum(m_old, jnp.max(t, axis=0, keepdims=True))
            p = jnp.exp2(t - m_new)
            alpha = jnp.exp2(m_old - m_new)
            l_ref[:, hs] = alpha * l_ref[:, hs] + jnp.sum(p, axis=0, keepdims=True)
            m_ref[:, hs] = m_new
            alpha_ref[:, hs] = alpha
            p_ref[0:n_keys, hs] = p.astype(BF16)
        pv = jnp.dot(vt_ref[:, pl.ds(start, n_keys)], p_ref[0:n_keys, :],
                     preferred_element_type=F32)
        acc_ref[...] = acc_ref[...] * alpha_ref[...] + pv

    far_end = (j - 1) * Q_BLOCK
    far_row = lax.broadcasted_iota(I32, (FAR_CHUNK, Q_BLOCK), 0)

    def far_body(c, carry):
        lo = c * FAR_CHUNK
        start = pl.multiple_of(jnp.minimum(lo, jnp.maximum(far_end - FAR_CHUNK, 0)), LANES)
        pos = start + far_row
        sel = (key_ref[pl.ds(start, FAR_CHUNK), :] >= thr) & (pos >= lo) & (pos < far_end)
        maskb = jnp.where(sel, 0.0, NEG)
        attend(start, FAR_CHUNK, maskb, lambda h: cvec_ref[:, h * Q_BLOCK:(h + 1) * Q_BLOCK])
        return carry

    lax.fori_loop(0, (jnp.maximum(far_end, 0) + FAR_CHUNK - 1) // FAR_CHUNK, far_body, 0)

    near_start = pl.multiple_of(jnp.maximum(j - 1, 0) * Q_BLOCK, LANES)
    first = jnp.where(j == 0, 1, 0)
    sel = key_ref[pl.ds(near_start, NEAR_KEYS), :] >= thr
    attend(near_start, NEAR_KEYS, jnp.where(sel, 0.0, NEG),
           lambda h: near_ref[first, :, h * Q_BLOCK:(h + 1) * Q_BLOCK])

    inv_l = 1.0 / l_ref[...]
    for h in range(N_HEADS):
        hs = slice(h * Q_BLOCK, (h + 1) * Q_BLOCK)
        o_ref[:, h * HEAD_DIM:(h + 1) * HEAD_DIM] = (
            acc_ref[:, hs] * inv_l[:, hs]).T.astype(o_ref.dtype)


def _t5_bucket_static(n):
    max_exact = N_BUCKETS // 2
    nf = np.maximum(n, 1).astype(np.float32)
    large = max_exact + (np.log(nf / np.float32(max_exact)) / np.float32(math.log(MAX_DISTANCE / max_exact))
                         * np.float32(N_BUCKETS - max_exact)).astype(np.int32)
    large = np.minimum(large, N_BUCKETS - 1)
    return np.where(n < max_exact, n, large)


def _bias_tables(rel_bias, seq):
    buckets = _t5_bucket_static(np.arange(seq, dtype=np.int32))
    far = buckets[Q_BLOCK + 1:]
    assert (far == far[0]).all()
    c = np.arange(NEAR_KEYS)[:, None]
    ql = np.arange(Q_BLOCK)[None, :]
    bias2 = rel_bias.astype(F32) * LOG2E

    def table(dist):
        idx = buckets[np.clip(dist, 0, seq - 1)]
        t = jnp.transpose(bias2[idx], (0, 2, 1))
        t = jnp.where(jnp.asarray(dist >= 0)[:, None, :], t, NEG)
        return t.reshape(NEAR_KEYS, N_HEADS * Q_BLOCK)

    near = jnp.stack([table(Q_BLOCK + ql - c), table(ql - c)])
    cvec = jnp.repeat(bias2[int(far[0])], Q_BLOCK).reshape(1, N_HEADS * Q_BLOCK)
    return near, cvec


def _attention(qi_hm, kidx, w_t, q_hm, k, v_t, near, cvec, batch, seq):
    m = batch * seq
    n_blk = seq // Q_BLOCK
    k_sel = min(TOPK_MAX, seq // 4)
    hq = N_HEADS * Q_BLOCK
    kern = functools.partial(
        _attn_kernel, k_sel=k_sel,
        idx_scale=IDX_DIM ** -0.5 * N_IDX_HEADS ** -0.5,
        logit_scale=HEAD_DIM ** -0.5 * LOG2E)
    blk = lambda b, j: (b * n_blk + j)
    return pl.pallas_call(
        kern,
        grid=(batch, n_blk),
        in_specs=[pl.BlockSpec((N_IDX_HEADS, Q_BLOCK, IDX_DIM), lambda b, j: (0, blk(b, j), 0)),
                  pl.BlockSpec((seq, IDX_DIM), lambda b, j: (b, 0)),
                  pl.BlockSpec((N_IDX_HEADS, Q_BLOCK), lambda b, j: (0, blk(b, j))),
                  pl.BlockSpec((N_HEADS, Q_BLOCK, HEAD_DIM), lambda b, j: (0, blk(b, j), 0)),
                  pl.BlockSpec((seq, HEAD_DIM), lambda b, j: (b, 0)),
                  pl.BlockSpec((HEAD_DIM, seq), lambda b, j: (0, b)),
                  pl.BlockSpec((2, NEAR_KEYS, hq), lambda b, j: (0, 0, 0)),
                  pl.BlockSpec((1, hq), lambda b, j: (0, 0))],
        out_specs=pl.BlockSpec((Q_BLOCK, N_HEADS * HEAD_DIM), lambda b, j: (blk(b, j), 0)),
        out_shape=jax.ShapeDtypeStruct((m, N_HEADS * HEAD_DIM), BF16),
        scratch_shapes=[pltpu.VMEM((seq, Q_BLOCK), I32),
                        pltpu.VMEM((1, hq), F32),
                        pltpu.VMEM((1, hq), F32),
                        pltpu.VMEM((HEAD_DIM, hq), F32),
                        pltpu.VMEM((FAR_CHUNK, hq), BF16),
                        pltpu.VMEM((1, hq), F32)],
        compiler_params=_params(("parallel", "arbitrary")),
        name="dsa_attention",
    )(qi_hm, kidx, w_t, q_hm, k, v_t, near, cvec)


def _sgu_kernel(uv_ref, g_ref, b_ref, ws_ref, bs_ref, o_ref):
    uv = jax.nn.gelu(uv_ref[...])
    u = uv[:, :GMLP_WIDTH]
    v = uv[:, GMLP_WIDTH:]
    mu = jnp.mean(v, axis=-1, keepdims=True)
    vc = v - mu
    var = jnp.mean(vc * vc, axis=-1, keepdims=True)
    vn = (vc * lax.rsqrt(var + EPS) * g_ref[...] + b_ref[...]).astype(BF16)
    r = lax.broadcasted_iota(I32, (CHUNK, CHUNK), 0)
    c = lax.broadcasted_iota(I32, (CHUNK, CHUNK), 1)
    tril = c <= r
    bs = bs_ref[...]
    for g in range(N_GROUPS):
        ws = jnp.where(tril, ws_ref[g], 0.0).astype(BF16)
        mixed = jnp.dot(ws, vn[:, g * GROUP_DIM:(g + 1) * GROUP_DIM],
                        preferred_element_type=F32) + bs[:, g:g + 1]
        o_ref[:, g * GROUP_DIM:(g + 1) * GROUP_DIM] = (
            u[:, g * GROUP_DIM:(g + 1) * GROUP_DIM] * mixed).astype(o_ref.dtype)


def _sgu(uv, ln_g, ln_b, w_s, b_s):
    m = uv.shape[0]
    bs_t = jnp.zeros((CHUNK, LANES), F32).at[:, :N_GROUPS].set(jnp.transpose(b_s))
    return pl.pallas_call(
        _sgu_kernel,
        grid=(m // CHUNK,),
        in_specs=[pl.BlockSpec((CHUNK, 2 * GMLP_WIDTH), lambda i: (i, 0)),
                  pl.BlockSpec((1, GMLP_WIDTH), lambda i: (0, 0)),
                  pl.BlockSpec((1, GMLP_WIDTH), lambda i: (0, 0)),
                  pl.BlockSpec((N_GROUPS, CHUNK, CHUNK), lambda i: (0, 0, 0)),
                  pl.BlockSpec((CHUNK, LANES), lambda i: (0, 0))],
        out_specs=pl.BlockSpec((CHUNK, GMLP_WIDTH), lambda i: (i, 0)),
        out_shape=jax.ShapeDtypeStruct((m, GMLP_WIDTH), BF16),
        compiler_params=_params(("parallel",)),
        name="chunked_sgu",
    )(uv, ln_g.reshape(1, -1), ln_b.reshape(1, -1), w_s, bs_t)


def _merge_kernel(h_ref, ya_ref, yb_ref, wga_ref, wgb_ref, wa_ref, wb_ref, o_ref):
    h = h_ref[...]
    ga = jax.nn.sigmoid(jnp.dot(h, wga_ref[...], preferred_element_type=F32))
    a = jnp.dot(ya_ref[...], wa_ref[...], preferred_element_type=F32)
    out = ga * a
    gb = jax.nn.sigmoid(jnp.dot(h, wgb_ref[...], preferred_element_type=F32))
    b = jnp.dot(yb_ref[...], wb_ref[...], preferred_element_type=F32)
    o_ref[...] = (out + gb * b).astype(o_ref.dtype)


def _merge(h, ya, yb, wga, wgb, wa, wb, tm=512, tn=512):
    m, d = h.shape
    n = wga.shape[1]
    ka, kb = ya.shape[1], yb.shape[1]
    row = lambda i, j: (i, 0)
    colw = lambda i, j: (0, j)
    return pl.pallas_call(
        _merge_kernel,
        grid=(m // tm, n // tn),
        in_specs=[pl.BlockSpec((tm, d), row), pl.BlockSpec((tm, ka), row), pl.BlockSpec((tm, kb), row),
                  pl.BlockSpec((d, tn), colw), pl.BlockSpec((d, tn), colw),
                  pl.BlockSpec((ka, tn), colw), pl.BlockSpec((kb, tn), colw)],
        out_specs=pl.BlockSpec((tm, tn), lambda i, j: (i, j)),
        out_shape=jax.ShapeDtypeStruct((m, n), BF16),
        compiler_params=_params(("parallel", "arbitrary")),
        name="branch_merge",
    )(h, ya, yb, wga, wgb, wa, wb)


def _ple_kernel(p_ref, w_ref, g_ref, o_ref):
    acc = jnp.dot(p_ref[...].astype(BF16), w_ref[...], preferred_element_type=F32)
    ms = jnp.mean(acc * acc, axis=-1, keepdims=True)
    o_ref[...] = (acc * lax.rsqrt(ms + EPS) * g_ref[...]).astype(o_ref.dtype)


def _ple(p, w, g, tm=256):
    m, dp = p.shape
    d = w.shape[1]
    return pl.pallas_call(
        _ple_kernel,
        grid=(m // tm,),
        in_specs=[pl.BlockSpec((tm, dp), lambda i: (i, 0)),
                  pl.BlockSpec((dp, d), lambda i: (0, 0)),
                  pl.BlockSpec((1, d), lambda i: (0, 0))],
        out_specs=pl.BlockSpec((tm, d), lambda i: (i, 0)),
        out_shape=jax.ShapeDtypeStruct((m, d), F32),
        compiler_params=_params(("parallel",)),
        name="ple_embed",
    )(p, w, g.reshape(1, d))


def kernel(x, p, w_in, q_norm_g, k_norm_g, rel_bias, sgu_ln_g, sgu_ln_b, sgu_w, sgu_b, w_branch_a, w_branch_b, w_out, norm_mix_g, norm_ffn_g, w_gate_ffn, w_up_ffn, w_down_ffn, w_ple, ple_norm_g, w_ple_gate, norm_ple_g):
    batch, seq, d_model = x.shape
    depth = w_in.shape[0]
    m = batch * seq
    a_width = N_HEADS * HEAD_DIM
    sizes = (a_width, HEAD_DIM, HEAD_DIM, N_IDX_HEADS * IDX_DIM, IDX_DIM, N_IDX_HEADS,
             2 * GMLP_WIDTH, d_model, d_model)
    offs = np.concatenate([[0], np.cumsum(sizes)])
    near, cvec = _bias_tables(rel_bias, seq)

    xf = x.reshape(m, d_model)
    for i in range(depth):
        seg = lambda s: w_in[i][:, int(offs[s]):int(offs[s + 1])].astype(BF16)
        w_q, w_k, w_v, w_qi, w_ki, w_wi, w_uv, w_ga, w_gb = [seg(s) for s in range(9)]
        w_small = jnp.concatenate(
            [w_k, w_v, w_ki, jnp.pad(w_wi, ((0, 0), (0, LANES - N_IDX_HEADS)))], axis=1)

        h = _rmsnorm(xf, norm_mix_g[i])
        q_hm = _headproj(h, w_q, q_norm_g[i], "q_proj")
        qi_hm = _headproj(h, w_qi, None, "q_idx_proj")
        k_, v_, kidx, widx = _small_proj(h, w_small, k_norm_g[i])
        uv = _matmul(h, [w_uv], [], lambda a, e: a[0], F32, 1024, 1024, "uv_proj")
        y_a = _attention(qi_hm, kidx, jnp.transpose(widx[:, :N_IDX_HEADS]), q_hm, k_,
                         jnp.transpose(v_), near, cvec, batch, seq)
        y_b = _sgu(uv, sgu_ln_g[i], sgu_ln_b[i], sgu_w[i], sgu_b[i])
        merged = _merge(h, y_a, y_b, w_ga, w_gb,
                        w_branch_a[i].astype(BF16), w_branch_b[i].astype(BF16))
        x1 = _matmul(merged, [w_out[i].astype(BF16)], [xf],
                     lambda a, e: e[0] + a[0], F32, 512, 1024, "out_proj")

        h2 = _rmsnorm(x1, norm_ffn_g[i])
        t = _matmul(h2, [w_gate_ffn[i].astype(BF16), w_up_ffn[i].astype(BF16)], [],
                    lambda a, e: jax.nn.silu(a[0]) * a[1], BF16, 1024, 256, "ffn_gate_up")
        d_ff = t.shape[1]
        x2 = _matmul_ksplit_residual(t, w_down_ffn[i].astype(BF16), x1, 512, 1024, d_ff // 2, "ffn_down")

        pe = _ple(p[i].reshape(m, -1), w_ple[i].astype(BF16), ple_norm_g[i])
        h3 = _rmsnorm(x2, norm_ple_g[i])
        xf = _matmul(h3, [w_ple_gate[i].astype(BF16)], [x2, pe],
                     lambda a, e: e[0] + jax.nn.sigmoid(a[0]) * e[1], F32, 512, 1024, "ple_gate")
    return xf.reshape(batch, seq, d_model)
```

```python
import functools
import math

import numpy as np
import jax
import jax.numpy as jnp
from jax import lax
from jax.experimental import pallas as pl
from jax.experimental.pallas import tpu as pltpu

F32 = jnp.float32
BF16 = jnp.bfloat16
I32 = jnp.int32

N_HEADS = 16
HEAD_DIM = 128
N_IDX_HEADS = 32
IDX_DIM = 128
TOPK_MAX = 256
Q_BLOCK = 128
N_BUCKETS = 32
MAX_DISTANCE = 128
GMLP_WIDTH = 2048
N_GROUPS = 8
GROUP_DIM = GMLP_WIDTH // N_GROUPS
CHUNK = 128
EPS = 1e-6

LANES = 128
SUBLANES = 8
VMEM_LIMIT = 56 * 1024 * 1024
NEG = -(2.0 ** 100)
INT_MIN = -2 ** 31
KEY_NEG_INF = -2139095041
LOG2E = math.log2(math.e)

SCORE_CHUNK = 2 * LANES
COUNT_CHUNK = 2 * SCORE_CHUNK
FAR_CHUNK = 4 * LANES
NEAR_KEYS = 2 * Q_BLOCK
HEAD_GROUP = 16


def _params(sem):
    return pltpu.CompilerParams(dimension_semantics=sem, vmem_limit_bytes=VMEM_LIMIT)


def _rmsnorm_kernel(x_ref, g_ref, o_ref):
    x = x_ref[...]
    ms = jnp.mean(x * x, axis=-1, keepdims=True)
    o_ref[...] = (x * lax.rsqrt(ms + EPS) * g_ref[...]).astype(o_ref.dtype)


def _rmsnorm(x, g, tm=256):
    m, d = x.shape
    return pl.pallas_call(
        _rmsnorm_kernel,
        grid=(m // tm,),
        in_specs=[pl.BlockSpec((tm, d), lambda i: (i, 0)),
                  pl.BlockSpec((1, d), lambda i: (0, 0))],
        out_specs=pl.BlockSpec((tm, d), lambda i: (i, 0)),
        out_shape=jax.ShapeDtypeStruct((m, d), BF16),
        compiler_params=_params(("parallel",)),
        name="rmsnorm",
    )(x, g.reshape(1, d))


def _mm_kernel(*refs, n_w, n_x, epi):
    a_ref = refs[0]
    w_refs = refs[1:1 + n_w]
    x_refs = refs[1 + n_w:1 + n_w + n_x]
    o_ref = refs[1 + n_w + n_x]
    a = a_ref[...]
    accs = [jnp.dot(a, w[...], preferred_element_type=F32) for w in w_refs]
    o_ref[...] = epi(accs, [x[...] for x in x_refs]).astype(o_ref.dtype)


def _matmul(a, ws, extras, epi, out_dtype, tm, tn, name):
    m, k = a.shape
    n = ws[0].shape[1]
    kern = functools.partial(_mm_kernel, n_w=len(ws), n_x=len(extras), epi=epi)
    return pl.pallas_call(
        kern,
        grid=(m // tm, n // tn),
        in_specs=([pl.BlockSpec((tm, k), lambda i, j: (i, 0))]
                  + [pl.BlockSpec((k, tn), lambda i, j: (0, j)) for _ in ws]
                  + [pl.BlockSpec((tm, tn), lambda i, j: (i, j)) for _ in extras]),
        out_specs=pl.BlockSpec((tm, tn), lambda i, j: (i, j)),
        out_shape=jax.ShapeDtypeStruct((m, n), out_dtype),
        compiler_params=_params(("parallel", "arbitrary")),
        name=name,
    )(a, *ws, *extras)


def _mm_ksplit_kernel(a_ref, w_ref, r_ref, o_ref):
    part = jnp.dot(a_ref[...], w_ref[...], preferred_element_type=F32)

    @pl.when(pl.program_id(2) == 0)
    def _():
        o_ref[...] = r_ref[...] + part

    @pl.when(pl.program_id(2) != 0)
    def _():
        o_ref[...] += part


def _matmul_ksplit_residual(a, w, res, tm, tn, tk, name):
    m, k = a.shape
    n = w.shape[1]
    return pl.pallas_call(
        _mm_ksplit_kernel,
        grid=(m // tm, n // tn, k // tk),
        in_specs=[pl.BlockSpec((tm, tk), lambda i, j, kk: (i, kk)),
                  pl.BlockSpec((tk, tn), lambda i, j, kk: (kk, j)),
                  pl.BlockSpec((tm, tn), lambda i, j, kk: (i, j))],
        out_specs=pl.BlockSpec((tm, tn), lambda i, j, kk: (i, j)),
        out_shape=jax.ShapeDtypeStruct((m, n), F32),
        compiler_params=_params(("parallel", "arbitrary", "arbitrary")),
        name=name,
    )(a, w, res)


def _headproj_kernel(*refs, heads, norm):
    a_ref, w_ref = refs[0], refs[1]
    o_ref = refs[-1]
    acc = jnp.dot(a_ref[...], w_ref[...], preferred_element_type=F32)
    for hh in range(heads):
        blk = acc[:, hh * LANES:(hh + 1) * LANES]
        if norm:
            ms = jnp.mean(blk * blk, axis=-1, keepdims=True)
            blk = blk * lax.rsqrt(ms + EPS) * refs[2][...]
        o_ref[hh] = blk.astype(o_ref.dtype)


def _headproj(h, w, g, name, tm=1024, tn=1024):
    m, k = h.shape
    n_heads = w.shape[1] // LANES
    heads = tn // LANES
    norm = g is not None
    extra_specs = [pl.BlockSpec((1, LANES), lambda i, j: (0, 0))] if norm else []
    extra_args = [g.reshape(1, LANES)] if norm else []
    return pl.pallas_call(
        functools.partial(_headproj_kernel, heads=heads, norm=norm),
        grid=(m // tm, w.shape[1] // tn),
        in_specs=[pl.BlockSpec((tm, k), lambda i, j: (i, 0)),
                  pl.BlockSpec((k, tn), lambda i, j: (0, j))] + extra_specs,
        out_specs=pl.BlockSpec((heads, tm, LANES), lambda i, j: (j, i, 0)),
        out_shape=jax.ShapeDtypeStruct((n_heads, m, LANES), BF16),
        compiler_params=_params(("parallel", "arbitrary")),
        name=name,
    )(h, w, *extra_args)


def _small_kernel(a_ref, w_ref, g_ref, k_ref, v_ref, ki_ref, wi_ref):
    acc = jnp.dot(a_ref[...], w_ref[...], preferred_element_type=F32)
    kk = acc[:, 0:LANES]
    ms = jnp.mean(kk * kk, axis=-1, keepdims=True)
    k_ref[...] = (kk * lax.rsqrt(ms + EPS) * g_ref[...]).astype(k_ref.dtype)
    v_ref[...] = acc[:, LANES:2 * LANES].astype(v_ref.dtype)
    ki_ref[...] = acc[:, 2 * LANES:3 * LANES].astype(ki_ref.dtype)
    wi_ref[...] = acc[:, 3 * LANES:4 * LANES]


def _small_proj(h, w_small, g, tm=1024):
    m, k = h.shape
    row = lambda i: (i, 0)
    return pl.pallas_call(
        _small_kernel,
        grid=(m // tm,),
        in_specs=[pl.BlockSpec((tm, k), row),
                  pl.BlockSpec((k, 4 * LANES), lambda i: (0, 0)),
                  pl.BlockSpec((1, LANES), lambda i: (0, 0))],
        out_specs=[pl.BlockSpec((tm, LANES), row)] * 4,
        out_shape=[jax.ShapeDtypeStruct((m, LANES), BF16)] * 3
                  + [jax.ShapeDtypeStruct((m, LANES), F32)],
        compiler_params=_params(("parallel",)),
        name="kv_idx_proj",
    )(h, w_small, g.reshape(1, LANES))


def _attn_kernel(qi_ref, kidx_ref, wt_ref, q_ref, k_ref, vt_ref, near_ref, cvec_ref, eye_ref, o_ref,
                 key_ref, m_ref, l_ref, acc_ref, qa_ref,
                 *, k_sel, idx_scale, logit_scale):
    j = pl.program_id(1)
    nt = (((1,), (1,)), ((), ()))
    n_chunks = (j + 2) // 2

    key_row = lax.broadcasted_iota(I32, (SCORE_CHUNK, Q_BLOCK), 0)
    q_col = lax.broadcasted_iota(I32, (SCORE_CHUNK, Q_BLOCK), 1)

    def score_chunk(c, carry):
        start = pl.multiple_of(c * SCORE_CHUNK, SCORE_CHUNK)
        kc = kidx_ref[pl.ds(start, SCORE_CHUNK), :]
        acc = jnp.zeros((SCORE_CHUNK, Q_BLOCK), F32)
        for hp in range(N_IDX_HEADS // 2):
            qpair = qi_ref[2 * hp:2 * hp + 2].reshape(2 * Q_BLOCK, IDX_DIM)
            d = lax.dot_general(kc, qpair, nt, preferred_element_type=F32)
            acc = acc + jnp.maximum(d[:, :Q_BLOCK], 0.0) * wt_ref[2 * hp:2 * hp + 1, :]
            acc = acc + jnp.maximum(d[:, Q_BLOCK:], 0.0) * wt_ref[2 * hp + 1:2 * hp + 2, :]
        s = acc * idx_scale
        causal = (start + key_row) <= (j * Q_BLOCK + q_col)
        s = jnp.where(causal, s, -jnp.inf)
        bits = lax.bitcast_convert_type(s, I32)
        key_ref[pl.ds(start, SCORE_CHUNK), :] = bits ^ ((bits >> 31) & 0x7FFFFFFF)
        return carry

    lax.fori_loop(0, n_chunks, score_chunk, 0)

    @pl.when(n_chunks % 2 == 1)
    def _():
        key_ref[pl.ds(pl.multiple_of(n_chunks * SCORE_CHUNK, SCORE_CHUNK), SCORE_CHUNK), :] = (
            jnp.full((SCORE_CHUNK, Q_BLOCK), KEY_NEG_INF, I32))

    def bit_body(i, tu):
        cu = tu | jnp.left_shift(jnp.int32(1), 31 - i)
        cs = cu ^ INT_MIN

        def cnt_body(c, a):
            start = pl.multiple_of(c * COUNT_CHUNK, COUNT_CHUNK)
            ge = jnp.where(key_ref[pl.ds(start, COUNT_CHUNK), :] >= cs, 1, 0)
            return a + jnp.sum(ge.reshape(COUNT_CHUNK // SUBLANES, SUBLANES, Q_BLOCK), axis=0)

        a = lax.fori_loop(0, (n_chunks + 1) // 2, cnt_body, jnp.zeros((SUBLANES, Q_BLOCK), I32))
        cnt = jnp.sum(a, axis=0, keepdims=True)
        return jnp.where(cnt >= k_sel, cu, tu)

    tu = lax.fori_loop(0, 32, bit_body, jnp.zeros((1, Q_BLOCK), I32))
    thr = tu ^ INT_MIN

    m_ref[...] = jnp.full(m_ref.shape, NEG, F32)
    l_ref[...] = jnp.zeros(l_ref.shape, F32)
    acc_ref[...] = jnp.zeros(acc_ref.shape, F32)
    qa_ref[:, :HEAD_DIM] = q_ref[...].reshape(N_HEADS * Q_BLOCK, HEAD_DIM)
    qa_ref[:, HEAD_DIM:] = eye_ref[...]

    def softmax_step(h, t, scale):
        hs = slice(h * Q_BLOCK, (h + 1) * Q_BLOCK)
        m_old = m_ref[:, hs]
        m_new = jnp.maximum(m_old, jnp.max(t, axis=0, keepdims=True))
        if scale is None:
            p = jnp.exp2(t - m_new)
            alpha = jnp.exp2(m_old - m_new)
        else:
            p = jnp.exp2((t - m_new) * scale)
            alpha = jnp.exp2((m_old - m_new) * scale)
        l_ref[:, hs] = alpha * l_ref[:, hs] + jnp.sum(p, axis=0, keepdims=True)
        m_ref[:, hs] = m_new
        return p.astype(BF16), alpha

    def attend_group(g, keys, vt, logits_of):
        gs = slice(g * HEAD_GROUP * Q_BLOCK, (g + 1) * HEAD_GROUP * Q_BLOCK)
        s = lax.dot_general(keys, qa_ref[gs, 0:keys.shape[1]], nt,
                            preferred_element_type=F32)
        pa = [softmax_step(g * HEAD_GROUP + e,
                           *logits_of(g * HEAD_GROUP + e, s[:, e * Q_BLOCK:(e + 1) * Q_BLOCK]))
              for e in range(HEAD_GROUP)]
        p = jnp.concatenate([x[0] for x in pa], axis=1)
        alpha = jnp.concatenate([x[1] for x in pa], axis=1)
        pv = jnp.dot(vt, p, preferred_element_type=F32)
        acc_ref[:, gs] = acc_ref[:, gs] * alpha + pv

    far_end = (j - 1) * Q_BLOCK
    far_row = lax.broadcasted_iota(I32, (FAR_CHUNK, Q_BLOCK), 0)

    def far_body(c, carry):
        lo = c * FAR_CHUNK
        start = pl.multiple_of(jnp.minimum(lo, jnp.maximum(far_end - FAR_CHUNK, 0)), LANES)
        pos = start + far_row
        sel = (key_ref[pl.ds(start, FAR_CHUNK), :] >= thr) & (pos >= lo) & (pos < far_end)
        maskb = jnp.where(sel, 0.0, NEG).astype(BF16)
        kaug = jnp.concatenate([k_ref[pl.ds(start, FAR_CHUNK), :], maskb], axis=1)
        vt = vt_ref[:, pl.ds(start, FAR_CHUNK)]
        for g in range(N_HEADS // HEAD_GROUP):
            attend_group(g, kaug, vt, lambda h, s: (s, logit_scale))
        return carry

    lax.fori_loop(0, (jnp.maximum(far_end, 0) + FAR_CHUNK - 1) // FAR_CHUNK, far_body, 0)
    m_ref[...] = m_ref[...] * logit_scale + cvec_ref[...]

    near_start = pl.multiple_of(jnp.maximum(j - 1, 0) * Q_BLOCK, LANES)
    first = jnp.where(j == 0, 1, 0)
    sel = key_ref[pl.ds(near_start, NEAR_KEYS), :] >= thr
    maskb = jnp.where(sel, 0.0, NEG)
    k_near = k_ref[pl.ds(near_start, NEAR_KEYS), :]
    vt_near = vt_ref[:, pl.ds(near_start, NEAR_KEYS)]

    def near_logits(h, s):
        return s * logit_scale + near_ref[first, :, h * Q_BLOCK:(h + 1) * Q_BLOCK] + maskb, None

    for g in range(N_HEADS // HEAD_GROUP):
        attend_group(g, k_near, vt_near, near_logits)

    inv_l = 1.0 / l_ref[...]
    for h in range(N_HEADS):
        hs = slice(h * Q_BLOCK, (h + 1) * Q_BLOCK)
        o_ref[:, h * HEAD_DIM:(h + 1) * HEAD_DIM] = (
            acc_ref[:, hs] * inv_l[:, hs]).T.astype(o_ref.dtype)


def _t5_bucket_static(n):
    max_exact = N_BUCKETS // 2
    nf = np.maximum(n, 1).astype(np.float32)
    large = max_exact + (np.log(nf / np.float32(max_exact)) / np.float32(math.log(MAX_DISTANCE / max_exact))
                         * np.float32(N_BUCKETS - max_exact)).astype(np.int32)
    large = np.minimum(large, N_BUCKETS - 1)
    return np.where(n < max_exact, n, large)


def _bias_tables(rel_bias, seq):
    buckets = _t5_bucket_static(np.arange(seq, dtype=np.int32))
    far = buckets[Q_BLOCK + 1:]
    assert (far == far[0]).all()
    bias2 = rel_bias.astype(F32) * LOG2E
    span, origin = 4 * NEAR_KEYS, 2 * NEAR_KEYS
    d = np.arange(span) - origin
    r = jnp.where(jnp.asarray(d >= 0)[None, :], bias2[buckets[np.clip(d, 0, seq - 1)]].T, NEG)
    y = jnp.tile(r, (1, NEAR_KEYS))[:, :NEAR_KEYS * (span - 1)].reshape(N_HEADS, NEAR_KEYS, span - 1)

    def table(off):
        t = y[:, :, origin + off:origin + off + Q_BLOCK]
        return jnp.transpose(t, (1, 0, 2)).reshape(NEAR_KEYS, N_HEADS * Q_BLOCK)

    near = jnp.stack([table(Q_BLOCK), table(0)])
    cvec = jnp.repeat(bias2[int(far[0])], Q_BLOCK).reshape(1, N_HEADS * Q_BLOCK)
    return near, cvec


def _attention(qi_hm, kidx, w_t, q_hm, k, v_t, near, cvec, batch, seq):
    m = batch * seq
    n_blk = seq // Q_BLOCK
    k_sel = min(TOPK_MAX, seq // 4)
    hq = N_HEADS * Q_BLOCK
    kern = functools.partial(
        _attn_kernel, k_sel=k_sel,
        idx_scale=IDX_DIM ** -0.5 * N_IDX_HEADS ** -0.5,
        logit_scale=HEAD_DIM ** -0.5 * LOG2E)
    blk = lambda b, j: (b * n_blk + j)
    assert seq % COUNT_CHUNK == 0 and seq % FAR_CHUNK == 0
    eye = jnp.tile(jnp.eye(Q_BLOCK, dtype=BF16), (N_HEADS, 1))
    return pl.pallas_call(
        kern,
        grid=(batch, n_blk),
        in_specs=[pl.BlockSpec((N_IDX_HEADS, Q_BLOCK, IDX_DIM), lambda b, j: (0, blk(b, j), 0)),
                  pl.BlockSpec((seq, IDX_DIM), lambda b, j: (b, 0)),
                  pl.BlockSpec((N_IDX_HEADS, Q_BLOCK), lambda b, j: (0, blk(b, j))),
                  pl.BlockSpec((N_HEADS, Q_BLOCK, HEAD_DIM), lambda b, j: (0, blk(b, j), 0)),
                  pl.BlockSpec((seq, HEAD_DIM), lambda b, j: (b, 0)),
                  pl.BlockSpec((HEAD_DIM, seq), lambda b, j: (0, b)),
                  pl.BlockSpec((2, NEAR_KEYS, hq), lambda b, j: (0, 0, 0)),
                  pl.BlockSpec((1, hq), lambda b, j: (0, 0)),
                  pl.BlockSpec((hq, Q_BLOCK), lambda b, j: (0, 0))],
        out_specs=pl.BlockSpec((Q_BLOCK, N_HEADS * HEAD_DIM), lambda b, j: (blk(b, j), 0)),
        out_shape=jax.ShapeDtypeStruct((m, N_HEADS * HEAD_DIM), BF16),
        scratch_shapes=[pltpu.VMEM((seq, Q_BLOCK), I32),
                        pltpu.VMEM((1, hq), F32),
                        pltpu.VMEM((1, hq), F32),
                        pltpu.VMEM((HEAD_DIM, hq), F32),
                        pltpu.VMEM((hq, HEAD_DIM + Q_BLOCK), BF16)],
        compiler_params=_params(("parallel", "arbitrary")),
        name="dsa_attention",
    )(qi_hm, kidx, w_t, q_hm, k, v_t, near, cvec, eye)


def _sgu_kernel(uv_ref, g_ref, b_ref, ws_ref, bs_ref, o_ref):
    uv = jax.nn.gelu(uv_ref[...])
    u = uv[:, :GMLP_WIDTH]
    v = uv[:, GMLP_WIDTH:]
    mu = jnp.mean(v, axis=-1, keepdims=True)
    vc = v - mu
    var = jnp.mean(vc * vc, axis=-1, keepdims=True)
    vn = (vc * lax.rsqrt(var + EPS) * g_ref[...] + b_ref[...]).astype(BF16)
    r = lax.broadcasted_iota(I32, (CHUNK, CHUNK), 0)
    c = lax.broadcasted_iota(I32, (CHUNK, CHUNK), 1)
    tril = c <= r
    bs = bs_ref[...]
    for g in range(N_GROUPS):
        ws = jnp.where(tril, ws_ref[g], 0.0).astype(BF16)
        mixed = jnp.dot(ws, vn[:, g * GROUP_DIM:(g + 1) * GROUP_DIM],
                        preferred_element_type=F32) + bs[:, g:g + 1]
        o_ref[:, g * GROUP_DIM:(g + 1) * GROUP_DIM] = (
            u[:, g * GROUP_DIM:(g + 1) * GROUP_DIM] * mixed).astype(o_ref.dtype)


def _sgu(uv, ln_g, ln_b, w_s, b_s):
    m = uv.shape[0]
    bs_t = jnp.zeros((CHUNK, LANES), F32).at[:, :N_GROUPS].set(jnp.transpose(b_s))
    return pl.pallas_call(
        _sgu_kernel,
        grid=(m // CHUNK,),
        in_specs=[pl.BlockSpec((CHUNK, 2 * GMLP_WIDTH), lambda i: (i, 0)),
                  pl.BlockSpec((1, GMLP_WIDTH), lambda i: (0, 0)),
                  pl.BlockSpec((1, GMLP_WIDTH), lambda i: (0, 0)),
                  pl.BlockSpec((N_GROUPS, CHUNK, CHUNK), lambda i: (0, 0, 0)),
                  pl.BlockSpec((CHUNK, LANES), lambda i: (0, 0))],
        out_specs=pl.BlockSpec((CHUNK, GMLP_WIDTH), lambda i: (i, 0)),
        out_shape=jax.ShapeDtypeStruct((m, GMLP_WIDTH), BF16),
        compiler_params=_params(("parallel",)),
        name="chunked_sgu",
    )(uv, ln_g.reshape(1, -1), ln_b.reshape(1, -1), w_s, bs_t)


def _merge_kernel(h_ref, ya_ref, yb_ref, wga_ref, wgb_ref, wa_ref, wb_ref, o_ref):
    h = h_ref[...]
    ga = jax.nn.sigmoid(jnp.dot(h, wga_ref[...], preferred_element_type=F32))
    a = jnp.dot(ya_ref[...], wa_ref[...], preferred_element_type=F32)
    out = ga * a
    gb = jax.nn.sigmoid(jnp.dot(h, wgb_ref[...], preferred_element_type=F32))
    b = jnp.dot(yb_ref[...], wb_ref[...], preferred_element_type=F32)
    o_ref[...] = (out + gb * b).astype(o_ref.dtype)


def _merge(h, ya, yb, wga, wgb, wa, wb, tm=512, tn=512):
    m, d = h.shape
    n = wga.shape[1]
    ka, kb = ya.shape[1], yb.shape[1]
    row = lambda i, j: (i, 0)
    colw = lambda i, j: (0, j)
    return pl.pallas_call(
        _merge_kernel,
        grid=(m // tm, n // tn),
        in_specs=[pl.BlockSpec((tm, d), row), pl.BlockSpec((tm, ka), row), pl.BlockSpec((tm, kb), row),
                  pl.BlockSpec((d, tn), colw), pl.BlockSpec((d, tn), colw),
                  pl.BlockSpec((ka, tn), colw), pl.BlockSpec((kb, tn), colw)],
        out_specs=pl.BlockSpec((tm, tn), lambda i, j: (i, j)),
        out_shape=jax.ShapeDtypeStruct((m, n), BF16),
        compiler_params=_params(("parallel", "arbitrary")),
        name="branch_merge",
    )(h, ya, yb, wga, wgb, wa, wb)


def _ple_kernel(p_ref, w_ref, g_ref, o_ref):
    acc = jnp.dot(p_ref[...].astype(BF16), w_ref[...], preferred_element_type=F32)
    ms = jnp.mean(acc * acc, axis=-1, keepdims=True)
    o_ref[...] = (acc * lax.rsqrt(ms + EPS) * g_ref[...]).astype(o_ref.dtype)


def _ple(p, w, g, tm=256):
    m, dp = p.shape
    d = w.shape[1]
    return pl.pallas_call(
        _ple_kernel,
        grid=(m // tm,),
        in_specs=[pl.BlockSpec((tm, dp), lambda i: (i, 0)),
                  pl.BlockSpec((dp, d), lambda i: (0, 0)),
                  pl.BlockSpec((1, d), lambda i: (0, 0))],
        out_specs=pl.BlockSpec((tm, d), lambda i: (i, 0)),
        out_shape=jax.ShapeDtypeStruct((m, d), F32),
        compiler_params=_params(("parallel",)),
        name="ple_embed",
    )(p, w, g.reshape(1, d))


def kernel(x, p, w_in, q_norm_g, k_norm_g, rel_bias, sgu_ln_g, sgu_ln_b, sgu_w, sgu_b, w_branch_a, w_branch_b, w_out, norm_mix_g, norm_ffn_g, w_gate_ffn, w_up_ffn, w_down_ffn, w_ple, ple_norm_g, w_ple_gate, norm_ple_g):
    batch, seq, d_model = x.shape
    depth = w_in.shape[0]
    m = batch * seq
    a_width = N_HEADS * HEAD_DIM
    sizes = (a_width, HEAD_DIM, HEAD_DIM, N_IDX_HEADS * IDX_DIM, IDX_DIM, N_IDX_HEADS,
             2 * GMLP_WIDTH, d_model, d_model)
    offs = np.concatenate([[0], np.cumsum(sizes)])
    near, cvec = _bias_tables(rel_bias, seq)

    xf = x.reshape(m, d_model)
    for i in range(depth):
        seg = lambda s: w_in[i][:, int(offs[s]):int(offs[s + 1])].astype(BF16)
        w_q, w_k, w_v, w_qi, w_ki, w_wi, w_uv, w_ga, w_gb = [seg(s) for s in range(9)]
        w_small = jnp.concatenate(
            [w_k, w_v, w_ki, jnp.pad(w_wi, ((0, 0), (0, LANES - N_IDX_HEADS)))], axis=1)

        h = _rmsnorm(xf, norm_mix_g[i])
        q_hm = _headproj(h, w_q, q_norm_g[i], "q_proj")
        qi_hm = _headproj(h, w_qi, None, "q_idx_proj")
        k_, v_, kidx, widx = _small_proj(h, w_small, k_norm_g[i])
        uv = _matmul(h, [w_uv], [], lambda a, e: a[0], F32, 1024, 1024, "uv_proj")
        y_a = _attention(qi_hm, kidx, jnp.transpose(widx[:, :N_IDX_HEADS]), q_hm, k_,
                         jnp.transpose(v_), near, cvec, batch, seq)
        y_b = _sgu(uv, sgu_ln_g[i], sgu_ln_b[i], sgu_w[i], sgu_b[i])
        merged = _merge(h, y_a, y_b, w_ga, w_gb,
                        w_branch_a[i].astype(BF16), w_branch_b[i].astype(BF16))
        x1 = _matmul(merged, [w_out[i].astype(BF16)], [xf],
                     lambda a, e: e[0] + a[0], F32, 1024, 512, "out_proj")

        h2 = _rmsnorm(x1, norm_ffn_g[i])
        t = _matmul(h2, [w_gate_ffn[i].astype(BF16), w_up_ffn[i].astype(BF16)], [],
                    lambda a, e: jax.nn.silu(a[0]) * a[1], BF16, 1024, 256, "ffn_gate_up")
        d_ff = t.shape[1]
        x2 = _matmul_ksplit_residual(t, w_down_ffn[i].astype(BF16), x1, 1024, 512, d_ff // 2, "ffn_down")

        pe = _ple(p[i].reshape(m, -1), w_ple[i].astype(BF16), ple_norm_g[i])
        h3 = _rmsnorm(x2, norm_ple_g[i])
        xf = _matmul(h3, [w_ple_gate[i].astype(BF16)], [x2, pe],
                     lambda a, e: e[0] + jax.nn.sigmoid(a[0]) * e[1], F32, 1024, 512, "ple_gate")
    return xf.reshape(batch, seq, d_model)
```

```python
import functools
import math

import numpy as np
import jax
import jax.numpy as jnp
from jax import lax
from jax.experimental import pallas as pl
from jax.experimental.pallas import tpu as pltpu

F32 = jnp.float32
BF16 = jnp.bfloat16
I32 = jnp.int32

N_HEADS = 16
HEAD_DIM = 128
N_IDX_HEADS = 32
IDX_DIM = 128
TOPK_MAX = 256
Q_BLOCK = 128
N_BUCKETS = 32
MAX_DISTANCE = 128
GMLP_WIDTH = 2048
N_GROUPS = 8
GROUP_DIM = GMLP_WIDTH // N_GROUPS
CHUNK = 128
EPS = 1e-6

LANES = 128
SUBLANES = 8
VMEM_LIMIT = 56 * 1024 * 1024
NEG = -(2.0 ** 100)
INT_MIN = -2 ** 31
KEY_NEG_INF = -2139095041
LOG2E = math.log2(math.e)

SCORE_CHUNK = 2 * LANES
COUNT_CHUNK = 2 * SCORE_CHUNK
FAR_CHUNK = 4 * LANES
NEAR_KEYS = 2 * Q_BLOCK
HEAD_GROUP = 16


def _params(sem):
    return pltpu.CompilerParams(dimension_semantics=sem, vmem_limit_bytes=VMEM_LIMIT)


def _rmsnorm_kernel(x_ref, g_ref, o_ref):
    x = x_ref[...]
    ms = jnp.mean(x * x, axis=-1, keepdims=True)
    o_ref[...] = (x * lax.rsqrt(ms + EPS) * g_ref[...]).astype(o_ref.dtype)


def _rmsnorm(x, g, tm=256):
    m, d = x.shape
    return pl.pallas_call(
        _rmsnorm_kernel,
        grid=(m // tm,),
        in_specs=[pl.BlockSpec((tm, d), lambda i: (i, 0)),
                  pl.BlockSpec((1, d), lambda i: (0, 0))],
        out_specs=pl.BlockSpec((tm, d), lambda i: (i, 0)),
        out_shape=jax.ShapeDtypeStruct((m, d), BF16),
        compiler_params=_params(("parallel",)),
        name="rmsnorm",
    )(x, g.reshape(1, d))


def _mm_kernel(*refs, n_w, n_x, epi):
    a_ref = refs[0]
    w_refs = refs[1:1 + n_w]
    x_refs = refs[1 + n_w:1 + n_w + n_x]
    o_ref = refs[1 + n_w + n_x]
    a = a_ref[...]
    accs = [jnp.dot(a, w[...], preferred_element_type=F32) for w in w_refs]
    o_ref[...] = epi(accs, [x[...] for x in x_refs]).astype(o_ref.dtype)


def _matmul(a, ws, extras, epi, out_dtype, tm, tn, name):
    m, k = a.shape
    n = ws[0].shape[1]
    kern = functools.partial(_mm_kernel, n_w=len(ws), n_x=len(extras), epi=epi)
    return pl.pallas_call(
        kern,
        grid=(m // tm, n // tn),
        in_specs=([pl.BlockSpec((tm, k), lambda i, j: (i, 0))]
                  + [pl.BlockSpec((k, tn), lambda i, j: (0, j)) for _ in ws]
                  + [pl.BlockSpec((tm, tn), lambda i, j: (i, j)) for _ in extras]),
        out_specs=pl.BlockSpec((tm, tn), lambda i, j: (i, j)),
        out_shape=jax.ShapeDtypeStruct((m, n), out_dtype),
        compiler_params=_params(("parallel", "arbitrary")),
        name=name,
    )(a, *ws, *extras)


def _mm_ksplit_kernel(a_ref, w_ref, r_ref, o_ref):
    part = jnp.dot(a_ref[...], w_ref[...], preferred_element_type=F32)

    @pl.when(pl.program_id(2) == 0)
    def _():
        o_ref[...] = r_ref[...] + part

    @pl.when(pl.program_id(2) != 0)
    def _():
        o_ref[...] += part


def _matmul_ksplit_residual(a, w, res, tm, tn, tk, name):
    m, k = a.shape
    n = w.shape[1]
    return pl.pallas_call(
        _mm_ksplit_kernel,
        grid=(m // tm, n // tn, k // tk),
        in_specs=[pl.BlockSpec((tm, tk), lambda i, j, kk: (i, kk)),
                  pl.BlockSpec((tk, tn), lambda i, j, kk: (kk, j)),
                  pl.BlockSpec((tm, tn), lambda i, j, kk: (i, j))],
        out_specs=pl.BlockSpec((tm, tn), lambda i, j, kk: (i, j)),
        out_shape=jax.ShapeDtypeStruct((m, n), F32),
        compiler_params=_params(("parallel", "arbitrary", "arbitrary")),
        name=name,
    )(a, w, res)


def _headproj_kernel(*refs, heads, norm):
    a_ref, w_ref = refs[0], refs[1]
    o_ref = refs[-1]
    acc = jnp.dot(a_ref[...], w_ref[...], preferred_element_type=F32)
    for hh in range(heads):
        blk = acc[:, hh * LANES:(hh + 1) * LANES]
        if norm:
            ms = jnp.mean(blk * blk, axis=-1, keepdims=True)
            blk = blk * lax.rsqrt(ms + EPS) * refs[2][...]
        o_ref[hh] = blk.astype(o_ref.dtype)


def _headproj(h, w, g, name, tm=1024, tn=1024):
    m, k = h.shape
    n_heads = w.shape[1] // LANES
    heads = tn // LANES
    norm = g is not None
    extra_specs = [pl.BlockSpec((1, LANES), lambda i, j: (0, 0))] if norm else []
    extra_args = [g.reshape(1, LANES)] if norm else []
    return pl.pallas_call(
        functools.partial(_headproj_kernel, heads=heads, norm=norm),
        grid=(m // tm, w.shape[1] // tn),
        in_specs=[pl.BlockSpec((tm, k), lambda i, j: (i, 0)),
                  pl.BlockSpec((k, tn), lambda i, j: (0, j))] + extra_specs,
        out_specs=pl.BlockSpec((heads, tm, LANES), lambda i, j: (j, i, 0)),
        out_shape=jax.ShapeDtypeStruct((n_heads, m, LANES), BF16),
        compiler_params=_params(("parallel", "arbitrary")),
        name=name,
    )(h, w, *extra_args)


def _small_kernel(a_ref, w_ref, g_ref, k_ref, v_ref, ki_ref, wi_ref):
    acc = jnp.dot(a_ref[...], w_ref[...], preferred_element_type=F32)
    kk = acc[:, 0:LANES]
    ms = jnp.mean(kk * kk, axis=-1, keepdims=True)
    k_ref[...] = (kk * lax.rsqrt(ms + EPS) * g_ref[...]).astype(k_ref.dtype)
    v_ref[...] = acc[:, LANES:2 * LANES].astype(v_ref.dtype)
    ki_ref[...] = acc[:, 2 * LANES:3 * LANES].astype(ki_ref.dtype)
    wi_ref[...] = acc[:, 3 * LANES:4 * LANES]


def _small_proj(h, w_small, g, tm=1024):
    m, k = h.shape
    row = lambda i: (i, 0)
    return pl.pallas_call(
        _small_kernel,
        grid=(m // tm,),
        in_specs=[pl.BlockSpec((tm, k), row),
                  pl.BlockSpec((k, 4 * LANES), lambda i: (0, 0)),
                  pl.BlockSpec((1, LANES), lambda i: (0, 0))],
        out_specs=[pl.BlockSpec((tm, LANES), row)] * 4,
        out_shape=[jax.ShapeDtypeStruct((m, LANES), BF16)] * 3
                  + [jax.ShapeDtypeStruct((m, LANES), F32)],
        compiler_params=_params(("parallel",)),
        name="kv_idx_proj",
    )(h, w_small, g.reshape(1, LANES))


def _attn_kernel(qi_ref, kidx_ref, wt_ref, q_ref, k_ref, vt_ref, near_ref, cvec_ref, eye_ref, o_ref,
                 key_ref, m_ref, l_ref, acc_ref, qa_ref,
                 *, k_sel, idx_scale, logit_scale):
    j = pl.program_id(1)
    nt = (((1,), (1,)), ((), ()))
    n_chunks = (j + 2) // 2

    key_row = lax.broadcasted_iota(I32, (SCORE_CHUNK, Q_BLOCK), 0)
    q_col = lax.broadcasted_iota(I32, (SCORE_CHUNK, Q_BLOCK), 1)

    def score_chunk(c, carry):
        start = pl.multiple_of(c * SCORE_CHUNK, SCORE_CHUNK)
        kc = kidx_ref[pl.ds(start, SCORE_CHUNK), :]
        acc = jnp.zeros((SCORE_CHUNK, Q_BLOCK), F32)
        for hp in range(N_IDX_HEADS // 2):
            qpair = qi_ref[2 * hp:2 * hp + 2].reshape(2 * Q_BLOCK, IDX_DIM)
            d = lax.dot_general(kc, qpair, nt, preferred_element_type=F32)
            acc = acc + jnp.maximum(d[:, :Q_BLOCK], 0.0) * wt_ref[2 * hp:2 * hp + 1, :]
            acc = acc + jnp.maximum(d[:, Q_BLOCK:], 0.0) * wt_ref[2 * hp + 1:2 * hp + 2, :]
        s = acc * idx_scale
        causal = (start + key_row) <= (j * Q_BLOCK + q_col)
        s = jnp.where(causal, s, -jnp.inf)
        bits = lax.bitcast_convert_type(s, I32)
        key_ref[pl.ds(start, SCORE_CHUNK), :] = bits ^ ((bits >> 31) & 0x7FFFFFFF)
        return carry

    lax.fori_loop(0, n_chunks, score_chunk, 0)

    @pl.when(n_chunks % 2 == 1)
    def _():
        key_ref[pl.ds(pl.multiple_of(n_chunks * SCORE_CHUNK, SCORE_CHUNK), SCORE_CHUNK), :] = (
            jnp.full((SCORE_CHUNK, Q_BLOCK), KEY_NEG_INF, I32))

    def bit_body(i, tu):
        cu = tu | jnp.left_shift(jnp.int32(1), 31 - i)
        cs = cu ^ INT_MIN

        def cnt_body(c, a):
            start = pl.multiple_of(c * COUNT_CHUNK, COUNT_CHUNK)
            ge = jnp.where(key_ref[pl.ds(start, COUNT_CHUNK), :] >= cs, 1, 0)
            return a + jnp.sum(ge.reshape(COUNT_CHUNK // SUBLANES, SUBLANES, Q_BLOCK), axis=0)

        a = lax.fori_loop(0, (n_chunks + 1) // 2, cnt_body, jnp.zeros((SUBLANES, Q_BLOCK), I32))
        cnt = jnp.sum(a, axis=0, keepdims=True)
        return jnp.where(cnt >= k_sel, cu, tu)

    tu = lax.fori_loop(0, 32, bit_body, jnp.zeros((1, Q_BLOCK), I32))
    thr = tu ^ INT_MIN

    m_ref[...] = jnp.full(m_ref.shape, NEG, F32)
    l_ref[...] = jnp.zeros(l_ref.shape, F32)
    acc_ref[...] = jnp.zeros(acc_ref.shape, F32)
    qa_ref[:, :HEAD_DIM] = q_ref[...].reshape(N_HEADS * Q_BLOCK, HEAD_DIM)
    qa_ref[:, HEAD_DIM:] = eye_ref[...]

    def softmax_step(h, t, scale):
        hs = slice(h * Q_BLOCK, (h + 1) * Q_BLOCK)
        m_old = m_ref[:, hs]
        m_new = jnp.maximum(m_old, jnp.max(t, axis=0, keepdims=True))
        if scale is None:
            p = jnp.exp2(t - m_new)
            alpha = jnp.exp2(m_old - m_new)
        else:
            p = jnp.exp2((t - m_new) * scale)
            alpha = jnp.exp2((m_old - m_new) * scale)
        l_ref[:, hs] = alpha * l_ref[:, hs] + jnp.sum(p, axis=0, keepdims=True)
        m_ref[:, hs] = m_new
        return p.astype(BF16), alpha

    def attend_group(g, keys, vt, logits_of):
        gs = slice(g * HEAD_GROUP * Q_BLOCK, (g + 1) * HEAD_GROUP * Q_BLOCK)
        s = lax.dot_general(keys, qa_ref[gs, 0:keys.shape[1]], nt,
                            preferred_element_type=F32)
        pa = [softmax_step(g * HEAD_GROUP + e,
                           *logits_of(g * HEAD_GROUP + e, s[:, e * Q_BLOCK:(e + 1) * Q_BLOCK]))
              for e in range(HEAD_GROUP)]
        p = jnp.concatenate([x[0] for x in pa], axis=1)
        alpha = jnp.concatenate([x[1] for x in pa], axis=1)
        pv = jnp.dot(vt, p, preferred_element_type=F32)
        acc_ref[:, gs] = acc_ref[:, gs] * alpha + pv

    far_end = (j - 1) * Q_BLOCK
    far_row = lax.broadcasted_iota(I32, (FAR_CHUNK, Q_BLOCK), 0)

    def far_body(c, carry):
        lo = c * FAR_CHUNK
        start = pl.multiple_of(jnp.minimum(lo, jnp.maximum(far_end - FAR_CHUNK, 0)), LANES)
        pos = start + far_row
        sel = (key_ref[pl.ds(start, FAR_CHUNK), :] >= thr) & (pos >= lo) & (pos < far_end)
        maskb = jnp.where(sel, 0.0, NEG).astype(BF16)
        kaug = jnp.concatenate([k_ref[pl.ds(start, FAR_CHUNK), :], maskb], axis=1)
        vt = vt_ref[:, pl.ds(start, FAR_CHUNK)]
        for g in range(N_HEADS // HEAD_GROUP):
            attend_group(g, kaug, vt, lambda h, s: (s, logit_scale))
        return carry

    lax.fori_loop(0, (jnp.maximum(far_end, 0) + FAR_CHUNK - 1) // FAR_CHUNK, far_body, 0)
    m_ref[...] = m_ref[...] * logit_scale + cvec_ref[...]

    near_start = pl.multiple_of(jnp.maximum(j - 1, 0) * Q_BLOCK, LANES)
    first = jnp.where(j == 0, 1, 0)
    sel = key_ref[pl.ds(near_start, NEAR_KEYS), :] >= thr
    maskb = jnp.where(sel, 0.0, NEG)
    k_near = k_ref[pl.ds(near_start, NEAR_KEYS), :]
    vt_near = vt_ref[:, pl.ds(near_start, NEAR_KEYS)]

    def near_logits(h, s):
        return s * logit_scale + near_ref[first, :, h * Q_BLOCK:(h + 1) * Q_BLOCK] + maskb, None

    for g in range(N_HEADS // HEAD_GROUP):
        attend_group(g, k_near, vt_near, near_logits)

    inv_l = 1.0 / l_ref[...]
    for h in range(N_HEADS):
        hs = slice(h * Q_BLOCK, (h + 1) * Q_BLOCK)
        o_ref[:, h * HEAD_DIM:(h + 1) * HEAD_DIM] = (
            acc_ref[:, hs] * inv_l[:, hs]).T.astype(o_ref.dtype)


def _t5_bucket_static(n):
    max_exact = N_BUCKETS // 2
    nf = np.maximum(n, 1).astype(np.float32)
    large = max_exact + (np.log(nf / np.float32(max_exact)) / np.float32(math.log(MAX_DISTANCE / max_exact))
                         * np.float32(N_BUCKETS - max_exact)).astype(np.int32)
    large = np.minimum(large, N_BUCKETS - 1)
    return np.where(n < max_exact, n, large)


def _bias_tables(rel_bias, seq):
    buckets = _t5_bucket_static(np.arange(seq, dtype=np.int32))
    far = buckets[Q_BLOCK + 1:]
    assert (far == far[0]).all()
    bias2 = rel_bias.astype(F32) * LOG2E
    span, origin = 4 * NEAR_KEYS, 2 * NEAR_KEYS
    d = np.arange(span) - origin
    r = jnp.where(jnp.asarray(d >= 0)[None, :], bias2[buckets[np.clip(d, 0, seq - 1)]].T, NEG)
    y = jnp.tile(r, (1, NEAR_KEYS))[:, :NEAR_KEYS * (span - 1)].reshape(N_HEADS, NEAR_KEYS, span - 1)

    def table(off):
        t = y[:, :, origin + off:origin + off + Q_BLOCK]
        return jnp.transpose(t, (1, 0, 2)).reshape(NEAR_KEYS, N_HEADS * Q_BLOCK)

    near = jnp.stack([table(Q_BLOCK), table(0)])
    cvec = jnp.repeat(bias2[int(far[0])], Q_BLOCK).reshape(1, N_HEADS * Q_BLOCK)
    return near, cvec


def _attention(qi_hm, kidx, w_t, q_hm, k, v_t, near, cvec, batch, seq):
    m = batch * seq
    n_blk = seq // Q_BLOCK
    k_sel = min(TOPK_MAX, seq // 4)
    hq = N_HEADS * Q_BLOCK
    kern = functools.partial(
        _attn_kernel, k_sel=k_sel,
        idx_scale=IDX_DIM ** -0.5 * N_IDX_HEADS ** -0.5,
        logit_scale=HEAD_DIM ** -0.5 * LOG2E)
    blk = lambda b, j: (b * n_blk + j)
    assert seq % COUNT_CHUNK == 0 and seq % FAR_CHUNK == 0
    eye = jnp.tile(jnp.eye(Q_BLOCK, dtype=BF16), (N_HEADS, 1))
    return pl.pallas_call(
        kern,
        grid=(batch, n_blk),
        in_specs=[pl.BlockSpec((N_IDX_HEADS, Q_BLOCK, IDX_DIM), lambda b, j: (0, blk(b, j), 0)),
                  pl.BlockSpec((seq, IDX_DIM), lambda b, j: (b, 0)),
                  pl.BlockSpec((N_IDX_HEADS, Q_BLOCK), lambda b, j: (0, blk(b, j))),
                  pl.BlockSpec((N_HEADS, Q_BLOCK, HEAD_DIM), lambda b, j: (0, blk(b, j), 0)),
                  pl.BlockSpec((seq, HEAD_DIM), lambda b, j: (b, 0)),
                  pl.BlockSpec((HEAD_DIM, seq), lambda b, j: (0, b)),
                  pl.BlockSpec((2, NEAR_KEYS, hq), lambda b, j: (0, 0, 0)),
                  pl.BlockSpec((1, hq), lambda b, j: (0, 0)),
                  pl.BlockSpec((hq, Q_BLOCK), lambda b, j: (0, 0))],
        out_specs=pl.BlockSpec((Q_BLOCK, N_HEADS * HEAD_DIM), lambda b, j: (blk(b, j), 0)),
        out_shape=jax.ShapeDtypeStruct((m, N_HEADS * HEAD_DIM), BF16),
        scratch_shapes=[pltpu.VMEM((seq, Q_BLOCK), I32),
                        pltpu.VMEM((1, hq), F32),
                        pltpu.VMEM((1, hq), F32),
                        pltpu.VMEM((HEAD_DIM, hq), F32),
                        pltpu.VMEM((hq, HEAD_DIM + Q_BLOCK), BF16)],
        compiler_params=_params(("parallel", "arbitrary")),
        name="dsa_attention",
    )(qi_hm, kidx, w_t, q_hm, k, v_t, near, cvec, eye)


def _sgu_kernel(uv_ref, g_ref, b_ref, ws_ref, bs_ref, o_ref):
    uv = jax.nn.gelu(uv_ref[...])
    u = uv[:, :GMLP_WIDTH]
    v = uv[:, GMLP_WIDTH:]
    mu = jnp.mean(v, axis=-1, keepdims=True)
    vc = v - mu
    var = jnp.mean(vc * vc, axis=-1, keepdims=True)
    vn = (vc * lax.rsqrt(var + EPS) * g_ref[...] + b_ref[...]).astype(BF16)
    r = lax.broadcasted_iota(I32, (CHUNK, CHUNK), 0)
    c = lax.broadcasted_iota(I32, (CHUNK, CHUNK), 1)
    tril = c <= r
    bs = bs_ref[...]
    for g in range(N_GROUPS):
        ws = jnp.where(tril, ws_ref[g], 0.0).astype(BF16)
        mixed = jnp.dot(ws, vn[:, g * GROUP_DIM:(g + 1) * GROUP_DIM],
                        preferred_element_type=F32) + bs[:, g:g + 1]
        o_ref[:, g * GROUP_DIM:(g + 1) * GROUP_DIM] = (
            u[:, g * GROUP_DIM:(g + 1) * GROUP_DIM] * mixed).astype(o_ref.dtype)


def _sgu(uv, ln_g, ln_b, w_s, b_s):
    m = uv.shape[0]
    bs_t = jnp.zeros((CHUNK, LANES), F32).at[:, :N_GROUPS].set(jnp.transpose(b_s))
    return pl.pallas_call(
        _sgu_kernel,
        grid=(m // CHUNK,),
        in_specs=[pl.BlockSpec((CHUNK, 2 * GMLP_WIDTH), lambda i: (i, 0)),
                  pl.BlockSpec((1, GMLP_WIDTH), lambda i: (0, 0)),
                  pl.BlockSpec((1, GMLP_WIDTH), lambda i: (0, 0)),
                  pl.BlockSpec((N_GROUPS, CHUNK, CHUNK), lambda i: (0, 0, 0)),
                  pl.BlockSpec((CHUNK, LANES), lambda i: (0, 0))],
        out_specs=pl.BlockSpec((CHUNK, GMLP_WIDTH), lambda i: (i, 0)),
        out_shape=jax.ShapeDtypeStruct((m, GMLP_WIDTH), BF16),
        compiler_params=_params(("parallel",)),
        name="chunked_sgu",
    )(uv, ln_g.reshape(1, -1), ln_b.reshape(1, -1), w_s, bs_t)


def _merge_kernel(h_ref, ya_ref, yb_ref, wga_ref, wgb_ref, wa_ref, wb_ref, o_ref):
    h = h_ref[...]
    ga = jax.nn.sigmoid(jnp.dot(h, wga_ref[...], preferred_element_type=F32))
    a = jnp.dot(ya_ref[...], wa_ref[...], preferred_element_type=F32)
    out = ga * a
    gb = jax.nn.sigmoid(jnp.dot(h, wgb_ref[...], preferred_element_type=F32))
    b = jnp.dot(yb_ref[...], wb_ref[...], preferred_element_type=F32)
    o_ref[...] = (out + gb * b).astype(o_ref.dtype)


def _merge(h, ya, yb, wga, wgb, wa, wb, tm=512, tn=512):
    m, d = h.shape
    n = wga.shape[1]
    ka, kb = ya.shape[1], yb.shape[1]
    row = lambda i, j: (i, 0)
    colw = lambda i, j: (0, j)
    return pl.pallas_call(
        _merge_kernel,
        grid=(m // tm, n // tn),
        in_specs=[pl.BlockSpec((tm, d), row), pl.BlockSpec((tm, ka), row), pl.BlockSpec((tm, kb), row),
                  pl.BlockSpec((d, tn), colw), pl.BlockSpec((d, tn), colw),
                  pl.BlockSpec((ka, tn), colw), pl.BlockSpec((kb, tn), colw)],
        out_specs=pl.BlockSpec((tm, tn), lambda i, j: (i, j)),
        out_shape=jax.ShapeDtypeStruct((m, n), BF16),
        compiler_params=_params(("parallel", "arbitrary")),
        name="branch_merge",
    )(h, ya, yb, wga, wgb, wa, wb)


NORM_ROWS = 128


def _rms_rows(x, g):
    ms = jnp.mean(x * x, axis=-1, keepdims=True)
    return x * lax.rsqrt(ms + EPS) * g


def _rms_block(src, g_ref, dst_ref):
    def body(r, carry):
        rows = pl.ds(pl.multiple_of(r * NORM_ROWS, NORM_ROWS), NORM_ROWS)
        dst_ref[rows, :] = _rms_rows(src(rows), g_ref[...]).astype(dst_ref.dtype)
        return carry

    lax.fori_loop(0, dst_ref.shape[0] // NORM_ROWS, body, 0)


def _ffn_up_kernel(x_ref, g_ref, wg_ref, wu_ref, o_ref, h_ref):
    @pl.when(pl.program_id(1) == 0)
    def _():
        _rms_block(lambda rows: x_ref[rows, :], g_ref, h_ref)

    h = h_ref[...]
    gate = jnp.dot(h, wg_ref[...], preferred_element_type=F32)
    up = jnp.dot(h, wu_ref[...], preferred_element_type=F32)
    o_ref[...] = (jax.nn.silu(gate) * up).astype(o_ref.dtype)


def _ffn_up(x, g, wg, wu, tm=1024, tn=256):
    m, d = x.shape
    n = wg.shape[1]
    return pl.pallas_call(
        _ffn_up_kernel,
        grid=(m // tm, n // tn),
        in_specs=[pl.BlockSpec((tm, d), lambda i, j: (i, 0)),
                  pl.BlockSpec((1, d), lambda i, j: (0, 0)),
                  pl.BlockSpec((d, tn), lambda i, j: (0, j)),
                  pl.BlockSpec((d, tn), lambda i, j: (0, j))],
        out_specs=pl.BlockSpec((tm, tn), lambda i, j: (i, j)),
        out_shape=jax.ShapeDtypeStruct((m, n), BF16),
        scratch_shapes=[pltpu.VMEM((tm, d), BF16)],
        compiler_params=_params(("parallel", "arbitrary")),
        name="ffn_gate_up",
    )(x, g.reshape(1, d), wg, wu)


def _ple_gate_kernel(x_ref, p_ref, wp_ref, gp_ref, gn_ref, w_ref, o_ref, h_ref, pe_ref, *, tn):
    @pl.when(pl.program_id(1) == 0)
    def _():
        _rms_block(lambda rows: x_ref[rows, :], gn_ref, h_ref)
        _rms_block(lambda rows: jnp.dot(p_ref[rows, :].astype(BF16), wp_ref[...],
                                        preferred_element_type=F32), gp_ref, pe_ref)

    cols = pl.ds(pl.multiple_of(pl.program_id(1) * tn, tn), tn)
    gate = jax.nn.sigmoid(jnp.dot(h_ref[...], w_ref[...], preferred_element_type=F32))
    o_ref[...] = x_ref[:, cols] + gate * pe_ref[:, cols]


def _ple_gate(x, p, wp, gp, gn, w, tm=512, tn=512):
    m, d = x.shape
    dp = p.shape[1]
    return pl.pallas_call(
        functools.partial(_ple_gate_kernel, tn=tn),
        grid=(m // tm, d // tn),
        in_specs=[pl.BlockSpec((tm, d), lambda i, j: (i, 0)),
                  pl.BlockSpec((tm, dp), lambda i, j: (i, 0)),
                  pl.BlockSpec((dp, d), lambda i, j: (0, 0)),
                  pl.BlockSpec((1, d), lambda i, j: (0, 0)),
                  pl.BlockSpec((1, d), lambda i, j: (0, 0)),
                  pl.BlockSpec((d, tn), lambda i, j: (0, j))],
        out_specs=pl.BlockSpec((tm, tn), lambda i, j: (i, j)),
        out_shape=jax.ShapeDtypeStruct((m, d), F32),
        scratch_shapes=[pltpu.VMEM((tm, d), BF16), pltpu.VMEM((tm, d), F32)],
        compiler_params=_params(("parallel", "arbitrary")),
        name="ple_gate",
    )(x, p, wp, gp.reshape(1, d), gn.reshape(1, d), w)


def kernel(x, p, w_in, q_norm_g, k_norm_g, rel_bias, sgu_ln_g, sgu_ln_b, sgu_w, sgu_b, w_branch_a, w_branch_b, w_out, norm_mix_g, norm_ffn_g, w_gate_ffn, w_up_ffn, w_down_ffn, w_ple, ple_norm_g, w_ple_gate, norm_ple_g):
    batch, seq, d_model = x.shape
    depth = w_in.shape[0]
    m = batch * seq
    a_width = N_HEADS * HEAD_DIM
    sizes = (a_width, HEAD_DIM, HEAD_DIM, N_IDX_HEADS * IDX_DIM, IDX_DIM, N_IDX_HEADS,
             2 * GMLP_WIDTH, d_model, d_model)
    offs = np.concatenate([[0], np.cumsum(sizes)])
    near, cvec = _bias_tables(rel_bias, seq)

    xf = x.reshape(m, d_model)
    for i in range(depth):
        seg = lambda s: w_in[i][:, int(offs[s]):int(offs[s + 1])].astype(BF16)
        w_q, w_k, w_v, w_qi, w_ki, w_wi, w_uv, w_ga, w_gb = [seg(s) for s in range(9)]
        w_small = jnp.concatenate(
            [w_k, w_v, w_ki, jnp.pad(w_wi, ((0, 0), (0, LANES - N_IDX_HEADS)))], axis=1)

        h = _rmsnorm(xf, norm_mix_g[i])
        q_hm = _headproj(h, w_q, q_norm_g[i], "q_proj")
        qi_hm = _headproj(h, w_qi, None, "q_idx_proj")
        k_, v_, kidx, widx = _small_proj(h, w_small, k_norm_g[i])
        uv = _matmul(h, [w_uv], [], lambda a, e: a[0], F32, 1024, 1024, "uv_proj")
        y_a = _attention(qi_hm, kidx, jnp.transpose(widx[:, :N_IDX_HEADS]), q_hm, k_,
                         jnp.transpose(v_), near, cvec, batch, seq)
        y_b = _sgu(uv, sgu_ln_g[i], sgu_ln_b[i], sgu_w[i], sgu_b[i])
        merged = _merge(h, y_a, y_b, w_ga, w_gb,
                        w_branch_a[i].astype(BF16), w_branch_b[i].astype(BF16))
        x1 = _matmul(merged, [w_out[i].astype(BF16)], [xf],
                     lambda a, e: e[0] + a[0], F32, 1024, 512, "out_proj")

        t = _ffn_up(x1, norm_ffn_g[i], w_gate_ffn[i].astype(BF16), w_up_ffn[i].astype(BF16))
        d_ff = t.shape[1]
        x2 = _matmul_ksplit_residual(t, w_down_ffn[i].astype(BF16), x1, 1024, 512, d_ff // 2, "ffn_down")

        xf = _ple_gate(x2, p[i].reshape(m, -1), w_ple[i].astype(BF16), ple_norm_g[i],
                       norm_ple_g[i], w_ple_gate[i].astype(BF16))
    return xf.reshape(batch, seq, d_model)
```

```python
import functools
import math

import numpy as np
import jax
import jax.numpy as jnp
from jax import lax
from jax.experimental import pallas as pl
from jax.experimental.pallas import tpu as pltpu

F32 = jnp.float32
BF16 = jnp.bfloat16
I32 = jnp.int32

N_HEADS = 16
HEAD_DIM = 128
N_IDX_HEADS = 32
IDX_DIM = 128
TOPK_MAX = 256
Q_BLOCK = 128
N_BUCKETS = 32
MAX_DISTANCE = 128
GMLP_WIDTH = 2048
N_GROUPS = 8
GROUP_DIM = GMLP_WIDTH // N_GROUPS
CHUNK = 128
EPS = 1e-6

LANES = 128
SUBLANES = 8
VMEM_LIMIT = 56 * 1024 * 1024
NEG = -(2.0 ** 100)
INT_MIN = -2 ** 31
KEY_NEG_INF = -2139095041
LOG2E = math.log2(math.e)

SCORE_CHUNK = 2 * LANES
COUNT_CHUNK = 2 * SCORE_CHUNK
FAR_CHUNK = 4 * LANES
NEAR_KEYS = 2 * Q_BLOCK
HEAD_GROUP = 16
MM_SUB_ROWS = 256


def _params(sem):
    return pltpu.CompilerParams(dimension_semantics=sem, vmem_limit_bytes=VMEM_LIMIT)


def _rmsnorm_kernel(x_ref, g_ref, o_ref):
    x = x_ref[...]
    ms = jnp.mean(x * x, axis=-1, keepdims=True)
    o_ref[...] = (x * lax.rsqrt(ms + EPS) * g_ref[...]).astype(o_ref.dtype)


def _rmsnorm(x, g, tm=256):
    m, d = x.shape
    return pl.pallas_call(
        _rmsnorm_kernel,
        grid=(m // tm,),
        in_specs=[pl.BlockSpec((tm, d), lambda i: (i, 0)),
                  pl.BlockSpec((1, d), lambda i: (0, 0))],
        out_specs=pl.BlockSpec((tm, d), lambda i: (i, 0)),
        out_shape=jax.ShapeDtypeStruct((m, d), BF16),
        compiler_params=_params(("parallel",)),
        name="rmsnorm",
    )(x, g.reshape(1, d))


def _lane_tiles(x):
    return [x[:, c * LANES:(c + 1) * LANES] for c in range(x.shape[1] // LANES)]


def _mm_kernel(*refs, n_w, n_x, epi, norm_dim):
    a_ref = refs[0]
    w_refs = refs[1:1 + n_w]
    x_refs = refs[1 + n_w:1 + n_w + n_x]
    if norm_dim:
        ssq_ref, o_ref, r_ref = refs[1 + n_w + n_x:]

        @pl.when(pl.program_id(1) == 0)
        def _():
            ms = jnp.sum(ssq_ref[...], axis=-1, keepdims=True) * (1.0 / norm_dim)
            r_ref[...] = jnp.broadcast_to(lax.rsqrt(ms + EPS), r_ref.shape)
    else:
        o_ref = refs[1 + n_w + n_x]
    for s in range(a_ref.shape[0] // MM_SUB_ROWS):
        rows = slice(s * MM_SUB_ROWS, (s + 1) * MM_SUB_ROWS)
        a = a_ref[rows, :]
        accs = [jnp.dot(a, w[...], preferred_element_type=F32) for w in w_refs]
        if norm_dim:
            accs = [acc * r_ref[rows, :] for acc in accs]
        o_ref[rows, :] = epi(accs, [x[rows, :] for x in x_refs]).astype(o_ref.dtype)


def _matmul(a, ws, extras, epi, out_dtype, tm, tn, name, row_ssq=None):
    m, k = a.shape
    n = ws[0].shape[1]
    normed = row_ssq is not None
    kern = functools.partial(_mm_kernel, n_w=len(ws), n_x=len(extras), epi=epi,
                             norm_dim=k if normed else 0)
    return pl.pallas_call(
        kern,
        grid=(m // tm, n // tn),
        in_specs=([pl.BlockSpec((tm, k), lambda i, j: (i, 0))]
                  + [pl.BlockSpec((k, tn), lambda i, j: (0, j)) for _ in ws]
                  + [pl.BlockSpec((tm, tn), lambda i, j: (i, j)) for _ in extras]
                  + ([pl.BlockSpec((tm, row_ssq.shape[1]), lambda i, j: (i, 0))] if normed else [])),
        out_specs=pl.BlockSpec((tm, tn), lambda i, j: (i, j)),
        out_shape=jax.ShapeDtypeStruct((m, n), out_dtype),
        scratch_shapes=[pltpu.VMEM((tm, tn), F32)] if normed else [],
        compiler_params=_params(("parallel", "arbitrary")),
        name=name,
    )(a, *ws, *extras, *([row_ssq] if normed else []))


def _res_mm_kernel(a_ref, w_ref, r_ref, g_ref, o_ref, og_ref, ssq_ref, *, nk):
    def sub_blocks(base_ref, last):
        for s in range(a_ref.shape[0] // MM_SUB_ROWS):
            rows = slice(s * MM_SUB_ROWS, (s + 1) * MM_SUB_ROWS)
            x = base_ref[rows, :] + jnp.dot(a_ref[rows, :], w_ref[...],
                                            preferred_element_type=F32)
            o_ref[rows, :] = x
            if last:
                og_ref[rows, :] = (x * g_ref[...]).astype(og_ref.dtype)
                ssq_ref[rows, :] = functools.reduce(lambda u, v: u + v, _lane_tiles(x * x))

    if nk == 1:
        sub_blocks(r_ref, True)
    else:
        @pl.when(pl.program_id(2) == 0)
        def _():
            sub_blocks(r_ref, False)

        @pl.when(pl.program_id(2) == 1)
        def _():
            sub_blocks(o_ref, True)


def _res_matmul(a, w, res, g_next, tm, tn, tk, name):
    m, k = a.shape
    n = w.shape[1]
    nk = k // tk
    assert nk in (1, 2) and nk * tk == k
    return pl.pallas_call(
        functools.partial(_res_mm_kernel, nk=nk),
        grid=(m // tm, n // tn, nk),
        in_specs=[pl.BlockSpec((tm, tk), lambda i, j, kk: (i, kk)),
                  pl.BlockSpec((tk, tn), lambda i, j, kk: (kk, j)),
                  pl.BlockSpec((tm, tn), lambda i, j, kk: (i, j)),
                  pl.BlockSpec((1, tn), lambda i, j, kk: (0, j))],
        out_specs=[pl.BlockSpec((tm, tn), lambda i, j, kk: (i, j)),
                   pl.BlockSpec((tm, tn), lambda i, j, kk: (i, j)),
                   pl.BlockSpec((tm, LANES), lambda i, j, kk: (i, j))],
        out_shape=[jax.ShapeDtypeStruct((m, n), F32),
                   jax.ShapeDtypeStruct((m, n), BF16),
                   jax.ShapeDtypeStruct((m, (n // tn) * LANES), F32)],
        compiler_params=_params(("parallel", "arbitrary", "arbitrary")),
        name=name,
    )(a, w, res, g_next.reshape(1, n))


def _headproj_kernel(*refs, heads, norm):
    a_ref, w_ref = refs[0], refs[1]
    o_ref = refs[-1]
    acc = jnp.dot(a_ref[...], w_ref[...], preferred_element_type=F32)
    for hh in range(heads):
        blk = acc[:, hh * LANES:(hh + 1) * LANES]
        if norm:
            ms = jnp.mean(blk * blk, axis=-1, keepdims=True)
            blk = blk * lax.rsqrt(ms + EPS) * refs[2][...]
        o_ref[hh] = blk.astype(o_ref.dtype)


def _headproj(h, w, g, name, tm=1024, tn=1024):
    m, k = h.shape
    n_heads = w.shape[1] // LANES
    heads = tn // LANES
    norm = g is not None
    extra_specs = [pl.BlockSpec((1, LANES), lambda i, j: (0, 0))] if norm else []
    extra_args = [g.reshape(1, LANES)] if norm else []
    return pl.pallas_call(
        functools.partial(_headproj_kernel, heads=heads, norm=norm),
        grid=(m // tm, w.shape[1] // tn),
        in_specs=[pl.BlockSpec((tm, k), lambda i, j: (i, 0)),
                  pl.BlockSpec((k, tn), lambda i, j: (0, j))] + extra_specs,
        out_specs=pl.BlockSpec((heads, tm, LANES), lambda i, j: (j, i, 0)),
        out_shape=jax.ShapeDtypeStruct((n_heads, m, LANES), BF16),
        compiler_params=_params(("parallel", "arbitrary")),
        name=name,
    )(h, w, *extra_args)


def _small_kernel(a_ref, w_ref, g_ref, k_ref, v_ref, ki_ref, wi_ref):
    acc = jnp.dot(a_ref[...], w_ref[...], preferred_element_type=F32)
    kk = acc[:, 0:LANES]
    ms = jnp.mean(kk * kk, axis=-1, keepdims=True)
    k_ref[...] = (kk * lax.rsqrt(ms + EPS) * g_ref[...]).astype(k_ref.dtype)
    v_ref[...] = acc[:, LANES:2 * LANES].astype(v_ref.dtype)
    ki_ref[...] = acc[:, 2 * LANES:3 * LANES].astype(ki_ref.dtype)
    wi_ref[...] = acc[:, 3 * LANES:4 * LANES]


def _small_proj(h, w_small, g, tm=1024):
    m, k = h.shape
    row = lambda i: (i, 0)
    return pl.pallas_call(
        _small_kernel,
        grid=(m // tm,),
        in_specs=[pl.BlockSpec((tm, k), row),
                  pl.BlockSpec((k, 4 * LANES), lambda i: (0, 0)),
                  pl.BlockSpec((1, LANES), lambda i: (0, 0))],
        out_specs=[pl.BlockSpec((tm, LANES), row)] * 4,
        out_shape=[jax.ShapeDtypeStruct((m, LANES), BF16)] * 3
                  + [jax.ShapeDtypeStruct((m, LANES), F32)],
        compiler_params=_params(("parallel",)),
        name="kv_idx_proj",
    )(h, w_small, g.reshape(1, LANES))


def _attn_kernel(qi_ref, kidx_ref, wt_ref, q_ref, k_ref, vt_ref, near_ref, cvec_ref, eye_ref, o_ref,
                 key_ref, m_ref, l_ref, acc_ref, qa_ref,
                 *, k_sel, idx_scale, logit_scale):
    j = pl.program_id(1)
    nt = (((1,), (1,)), ((), ()))
    n_chunks = (j + 2) // 2

    key_row = lax.broadcasted_iota(I32, (SCORE_CHUNK, Q_BLOCK), 0)
    q_col = lax.broadcasted_iota(I32, (SCORE_CHUNK, Q_BLOCK), 1)

    def score_chunk(c, carry):
        start = pl.multiple_of(c * SCORE_CHUNK, SCORE_CHUNK)
        kc = kidx_ref[pl.ds(start, SCORE_CHUNK), :]
        acc = jnp.zeros((SCORE_CHUNK, Q_BLOCK), F32)
        for hp in range(N_IDX_HEADS // 2):
            qpair = qi_ref[2 * hp:2 * hp + 2].reshape(2 * Q_BLOCK, IDX_DIM)
            d = lax.dot_general(kc, qpair, nt, preferred_element_type=F32)
            acc = acc + jnp.maximum(d[:, :Q_BLOCK], 0.0) * wt_ref[2 * hp:2 * hp + 1, :]
            acc = acc + jnp.maximum(d[:, Q_BLOCK:], 0.0) * wt_ref[2 * hp + 1:2 * hp + 2, :]
        s = acc * idx_scale
        causal = (start + key_row) <= (j * Q_BLOCK + q_col)
        s = jnp.where(causal, s, -jnp.inf)
        bits = lax.bitcast_convert_type(s, I32)
        key_ref[pl.ds(start, SCORE_CHUNK), :] = bits ^ ((bits >> 31) & 0x7FFFFFFF)
        return carry

    lax.fori_loop(0, n_chunks, score_chunk, 0)

    @pl.when(n_chunks % 2 == 1)
    def _():
        key_ref[pl.ds(pl.multiple_of(n_chunks * SCORE_CHUNK, SCORE_CHUNK), SCORE_CHUNK), :] = (
            jnp.full((SCORE_CHUNK, Q_BLOCK), KEY_NEG_INF, I32))

    def bit_body(i, tu):
        cu = tu | jnp.left_shift(jnp.int32(1), 31 - i)
        cs = cu ^ INT_MIN

        def cnt_body(c, a):
            start = pl.multiple_of(c * COUNT_CHUNK, COUNT_CHUNK)
            ge = jnp.where(key_ref[pl.ds(start, COUNT_CHUNK), :] >= cs, 1, 0)
            return a + jnp.sum(ge.reshape(COUNT_CHUNK // SUBLANES, SUBLANES, Q_BLOCK), axis=0)

        a = lax.fori_loop(0, (n_chunks + 1) // 2, cnt_body, jnp.zeros((SUBLANES, Q_BLOCK), I32))
        cnt = jnp.sum(a, axis=0, keepdims=True)
        return jnp.where(cnt >= k_sel, cu, tu)

    tu = lax.fori_loop(0, 32, bit_body, jnp.zeros((1, Q_BLOCK), I32))
    thr = tu ^ INT_MIN

    m_ref[...] = jnp.full(m_ref.shape, NEG, F32)
    l_ref[...] = jnp.zeros(l_ref.shape, F32)
    acc_ref[...] = jnp.zeros(acc_ref.shape, F32)
    qa_ref[:, :HEAD_DIM] = q_ref[...].reshape(N_HEADS * Q_BLOCK, HEAD_DIM)
    qa_ref[:, HEAD_DIM:] = eye_ref[...]

    def softmax_step(h, t, scale):
        hs = slice(h * Q_BLOCK, (h + 1) * Q_BLOCK)
        m_old = m_ref[:, hs]
        m_new = jnp.maximum(m_old, jnp.max(t, axis=0, keepdims=True))
        if scale is None:
            p = jnp.exp2(t - m_new)
            alpha = jnp.exp2(m_old - m_new)
        else:
            p = jnp.exp2((t - m_new) * scale)
            alpha = jnp.exp2((m_old - m_new) * scale)
        l_ref[:, hs] = alpha * l_ref[:, hs] + jnp.sum(p, axis=0, keepdims=True)
        m_ref[:, hs] = m_new
        return p.astype(BF16), alpha

    def attend_group(g, keys, vt, logits_of):
        gs = slice(g * HEAD_GROUP * Q_BLOCK, (g + 1) * HEAD_GROUP * Q_BLOCK)
        s = lax.dot_general(keys, qa_ref[gs, 0:keys.shape[1]], nt,
                            preferred_element_type=F32)
        pa = [softmax_step(g * HEAD_GROUP + e,
                           *logits_of(g * HEAD_GROUP + e, s[:, e * Q_BLOCK:(e + 1) * Q_BLOCK]))
              for e in range(HEAD_GROUP)]
        p = jnp.concatenate([x[0] for x in pa], axis=1)
        alpha = jnp.concatenate([x[1] for x in pa], axis=1)
        pv = jnp.dot(vt, p, preferred_element_type=F32)
        acc_ref[:, gs] = acc_ref[:, gs] * alpha + pv

    far_end = (j - 1) * Q_BLOCK
    far_row = lax.broadcasted_iota(I32, (FAR_CHUNK, Q_BLOCK), 0)

    def far_body(c, carry):
        lo = c * FAR_CHUNK
        start = pl.multiple_of(jnp.minimum(lo, jnp.maximum(far_end - FAR_CHUNK, 0)), LANES)
        pos = start + far_row
        sel = (key_ref[pl.ds(start, FAR_CHUNK), :] >= thr) & (pos >= lo) & (pos < far_end)
        maskb = jnp.where(sel, 0.0, NEG).astype(BF16)
        kaug = jnp.concatenate([k_ref[pl.ds(start, FAR_CHUNK), :], maskb], axis=1)
        vt = vt_ref[:, pl.ds(start, FAR_CHUNK)]
        for g in range(N_HEADS // HEAD_GROUP):
            attend_group(g, kaug, vt, lambda h, s: (s, logit_scale))
        return carry

    lax.fori_loop(0, (jnp.maximum(far_end, 0) + FAR_CHUNK - 1) // FAR_CHUNK, far_body, 0)
    m_ref[...] = m_ref[...] * logit_scale + cvec_ref[...]

    near_start = pl.multiple_of(jnp.maximum(j - 1, 0) * Q_BLOCK, LANES)
    first = jnp.where(j == 0, 1, 0)
    sel = key_ref[pl.ds(near_start, NEAR_KEYS), :] >= thr
    maskb = jnp.where(sel, 0.0, NEG)
    k_near = k_ref[pl.ds(near_start, NEAR_KEYS), :]
    vt_near = vt_ref[:, pl.ds(near_start, NEAR_KEYS)]

    def near_logits(h, s):
        return s * logit_scale + near_ref[first, :, h * Q_BLOCK:(h + 1) * Q_BLOCK] + maskb, None

    for g in range(N_HEADS // HEAD_GROUP):
        attend_group(g, k_near, vt_near, near_logits)

    inv_l = 1.0 / l_ref[...]
    for h in range(N_HEADS):
        hs = slice(h * Q_BLOCK, (h + 1) * Q_BLOCK)
        o_ref[:, h * HEAD_DIM:(h + 1) * HEAD_DIM] = (
            acc_ref[:, hs] * inv_l[:, hs]).T.astype(o_ref.dtype)


def _t5_bucket_static(n):
    max_exact = N_BUCKETS // 2
    nf = np.maximum(n, 1).astype(np.float32)
    large = max_exact + (np.log(nf / np.float32(max_exact)) / np.float32(math.log(MAX_DISTANCE / max_exact))
                         * np.float32(N_BUCKETS - max_exact)).astype(np.int32)
    large = np.minimum(large, N_BUCKETS - 1)
    return np.where(n < max_exact, n, large)


def _bias_tables(rel_bias, seq):
    buckets = _t5_bucket_static(np.arange(seq, dtype=np.int32))
    far = buckets[Q_BLOCK + 1:]
    assert (far == far[0]).all()
    bias2 = rel_bias.astype(F32) * LOG2E
    span, origin = 4 * NEAR_KEYS, 2 * NEAR_KEYS
    d = np.arange(span) - origin
    r = jnp.where(jnp.asarray(d >= 0)[None, :], bias2[buckets[np.clip(d, 0, seq - 1)]].T, NEG)
    y = jnp.tile(r, (1, NEAR_KEYS))[:, :NEAR_KEYS * (span - 1)].reshape(N_HEADS, NEAR_KEYS, span - 1)

    def table(off):
        t = y[:, :, origin + off:origin + off + Q_BLOCK]
        return jnp.transpose(t, (1, 0, 2)).reshape(NEAR_KEYS, N_HEADS * Q_BLOCK)

    near = jnp.stack([table(Q_BLOCK), table(0)])
    cvec = jnp.repeat(bias2[int(far[0])], Q_BLOCK).reshape(1, N_HEADS * Q_BLOCK)
    return near, cvec


def _attention(qi_hm, kidx, w_t, q_hm, k, v_t, near, cvec, batch, seq):
    m = batch * seq
    n_blk = seq // Q_BLOCK
    k_sel = min(TOPK_MAX, seq // 4)
    hq = N_HEADS * Q_BLOCK
    kern = functools.partial(
        _attn_kernel, k_sel=k_sel,
        idx_scale=IDX_DIM ** -0.5 * N_IDX_HEADS ** -0.5,
        logit_scale=HEAD_DIM ** -0.5 * LOG2E)
    blk = lambda b, j: (b * n_blk + j)
    assert seq % COUNT_CHUNK == 0 and seq % FAR_CHUNK == 0
    eye = jnp.tile(jnp.eye(Q_BLOCK, dtype=BF16), (N_HEADS, 1))
    return pl.pallas_call(
        kern,
        grid=(batch, n_blk),
        in_specs=[pl.BlockSpec((N_IDX_HEADS, Q_BLOCK, IDX_DIM), lambda b, j: (0, blk(b, j), 0)),
                  pl.BlockSpec((seq, IDX_DIM), lambda b, j: (b, 0)),
                  pl.BlockSpec((N_IDX_HEADS, Q_BLOCK), lambda b, j: (0, blk(b, j))),
                  pl.BlockSpec((N_HEADS, Q_BLOCK, HEAD_DIM), lambda b, j: (0, blk(b, j), 0)),
                  pl.BlockSpec((seq, HEAD_DIM), lambda b, j: (b, 0)),
                  pl.BlockSpec((HEAD_DIM, seq), lambda b, j: (0, b)),
                  pl.BlockSpec((2, NEAR_KEYS, hq), lambda b, j: (0, 0, 0)),
                  pl.BlockSpec((1, hq), lambda b, j: (0, 0)),
                  pl.BlockSpec((hq, Q_BLOCK), lambda b, j: (0, 0))],
        out_specs=pl.BlockSpec((Q_BLOCK, N_HEADS * HEAD_DIM), lambda b, j: (blk(b, j), 0)),
        out_shape=jax.ShapeDtypeStruct((m, N_HEADS * HEAD_DIM), BF16),
        scratch_shapes=[pltpu.VMEM((seq, Q_BLOCK), I32),
                        pltpu.VMEM((1, hq), F32),
                        pltpu.VMEM((1, hq), F32),
                        pltpu.VMEM((HEAD_DIM, hq), F32),
                        pltpu.VMEM((hq, HEAD_DIM + Q_BLOCK), BF16)],
        compiler_params=_params(("parallel", "arbitrary")),
        name="dsa_attention",
    )(qi_hm, kidx, w_t, q_hm, k, v_t, near, cvec, eye)


def _sgu_kernel(uv_ref, g_ref, b_ref, ws_ref, bs_ref, o_ref):
    uv = jax.nn.gelu(uv_ref[...])
    u = uv[:, :GMLP_WIDTH]
    v = uv[:, GMLP_WIDTH:]
    mu = jnp.mean(v, axis=-1, keepdims=True)
    vc = v - mu
    var = jnp.mean(vc * vc, axis=-1, keepdims=True)
    vn = (vc * lax.rsqrt(var + EPS) * g_ref[...] + b_ref[...]).astype(BF16)
    r = lax.broadcasted_iota(I32, (CHUNK, CHUNK), 0)
    c = lax.broadcasted_iota(I32, (CHUNK, CHUNK), 1)
    tril = c <= r
    bs = bs_ref[...]
    for g in range(N_GROUPS):
        ws = jnp.where(tril, ws_ref[g], 0.0).astype(BF16)
        mixed = jnp.dot(ws, vn[:, g * GROUP_DIM:(g + 1) * GROUP_DIM],
                        preferred_element_type=F32) + bs[:, g:g + 1]
        o_ref[:, g * GROUP_DIM:(g + 1) * GROUP_DIM] = (
            u[:, g * GROUP_DIM:(g + 1) * GROUP_DIM] * mixed).astype(o_ref.dtype)


def _sgu(uv, ln_g, ln_b, w_s, b_s):
    m = uv.shape[0]
    bs_t = jnp.zeros((CHUNK, LANES), F32).at[:, :N_GROUPS].set(jnp.transpose(b_s))
    return pl.pallas_call(
        _sgu_kernel,
        grid=(m // CHUNK,),
        in_specs=[pl.BlockSpec((CHUNK, 2 * GMLP_WIDTH), lambda i: (i, 0)),
                  pl.BlockSpec((1, GMLP_WIDTH), lambda i: (0, 0)),
                  pl.BlockSpec((1, GMLP_WIDTH), lambda i: (0, 0)),
                  pl.BlockSpec((N_GROUPS, CHUNK, CHUNK), lambda i: (0, 0, 0)),
                  pl.BlockSpec((CHUNK, LANES), lambda i: (0, 0))],
        out_specs=pl.BlockSpec((CHUNK, GMLP_WIDTH), lambda i: (i, 0)),
        out_shape=jax.ShapeDtypeStruct((m, GMLP_WIDTH), BF16),
        compiler_params=_params(("parallel",)),
        name="chunked_sgu",
    )(uv, ln_g.reshape(1, -1), ln_b.reshape(1, -1), w_s, bs_t)


def _merge_kernel(h_ref, ya_ref, yb_ref, wga_ref, wgb_ref, wa_ref, wb_ref, o_ref):
    h = h_ref[...]
    ga = jax.nn.sigmoid(jnp.dot(h, wga_ref[...], preferred_element_type=F32))
    a = jnp.dot(ya_ref[...], wa_ref[...], preferred_element_type=F32)
    out = ga * a
    gb = jax.nn.sigmoid(jnp.dot(h, wgb_ref[...], preferred_element_type=F32))
    b = jnp.dot(yb_ref[...], wb_ref[...], preferred_element_type=F32)
    o_ref[...] = (out + gb * b).astype(o_ref.dtype)


def _merge(h, ya, yb, wga, wgb, wa, wb, tm=512, tn=512):
    m, d = h.shape
    n = wga.shape[1]
    ka, kb = ya.shape[1], yb.shape[1]
    row = lambda i, j: (i, 0)
    colw = lambda i, j: (0, j)
    return pl.pallas_call(
        _merge_kernel,
        grid=(m // tm, n // tn),
        in_specs=[pl.BlockSpec((tm, d), row), pl.BlockSpec((tm, ka), row), pl.BlockSpec((tm, kb), row),
                  pl.BlockSpec((d, tn), colw), pl.BlockSpec((d, tn), colw),
                  pl.BlockSpec((ka, tn), colw), pl.BlockSpec((kb, tn), colw)],
        out_specs=pl.BlockSpec((tm, tn), lambda i, j: (i, j)),
        out_shape=jax.ShapeDtypeStruct((m, n), BF16),
        compiler_params=_params(("parallel", "arbitrary")),
        name="branch_merge",
    )(h, ya, yb, wga, wgb, wa, wb)


def _ple_kernel(p_ref, w_ref, g_ref, o_ref):
    acc = jnp.dot(p_ref[...].astype(BF16), w_ref[...], preferred_element_type=F32)
    ms = jnp.mean(acc * acc, axis=-1, keepdims=True)
    o_ref[...] = (acc * lax.rsqrt(ms + EPS) * g_ref[...]).astype(o_ref.dtype)


def _ple(p, w, g, tm=256):
    m, dp = p.shape
    d = w.shape[1]
    return pl.pallas_call(
        _ple_kernel,
        grid=(m // tm,),
        in_specs=[pl.BlockSpec((tm, dp), lambda i: (i, 0)),
                  pl.BlockSpec((dp, d), lambda i: (0, 0)),
                  pl.BlockSpec((1, d), lambda i: (0, 0))],
        out_specs=pl.BlockSpec((tm, d), lambda i: (i, 0)),
        out_shape=jax.ShapeDtypeStruct((m, d), F32),
        compiler_params=_params(("parallel",)),
        name="ple_embed",
    )(p, w, g.reshape(1, d))


def kernel(x, p, w_in, q_norm_g, k_norm_g, rel_bias, sgu_ln_g, sgu_ln_b, sgu_w, sgu_b, w_branch_a, w_branch_b, w_out, norm_mix_g, norm_ffn_g, w_gate_ffn, w_up_ffn, w_down_ffn, w_ple, ple_norm_g, w_ple_gate, norm_ple_g):
    batch, seq, d_model = x.shape
    depth = w_in.shape[0]
    m = batch * seq
    a_width = N_HEADS * HEAD_DIM
    sizes = (a_width, HEAD_DIM, HEAD_DIM, N_IDX_HEADS * IDX_DIM, IDX_DIM, N_IDX_HEADS,
             2 * GMLP_WIDTH, d_model, d_model)
    offs = np.concatenate([[0], np.cumsum(sizes)])
    near, cvec = _bias_tables(rel_bias, seq)

    xf = x.reshape(m, d_model)
    for i in range(depth):
        seg = lambda s: w_in[i][:, int(offs[s]):int(offs[s + 1])].astype(BF16)
        w_q, w_k, w_v, w_qi, w_ki, w_wi, w_uv, w_ga, w_gb = [seg(s) for s in range(9)]
        w_small = jnp.concatenate(
            [w_k, w_v, w_ki, jnp.pad(w_wi, ((0, 0), (0, LANES - N_IDX_HEADS)))], axis=1)

        h = _rmsnorm(xf, norm_mix_g[i])
        q_hm = _headproj(h, w_q, q_norm_g[i], "q_proj")
        qi_hm = _headproj(h, w_qi, None, "q_idx_proj")
        k_, v_, kidx, widx = _small_proj(h, w_small, k_norm_g[i])
        uv = _matmul(h, [w_uv], [], lambda a, e: a[0], F32, 1024, 1024, "uv_proj")
        y_a = _attention(qi_hm, kidx, jnp.transpose(widx[:, :N_IDX_HEADS]), q_hm, k_,
                         jnp.transpose(v_), near, cvec, batch, seq)
        y_b = _sgu(uv, sgu_ln_g[i], sgu_ln_b[i], sgu_w[i], sgu_b[i])
        merged = _merge(h, y_a, y_b, w_ga, w_gb,
                        w_branch_a[i].astype(BF16), w_branch_b[i].astype(BF16))
        x1 = _matmul(merged, [w_out[i].astype(BF16)], [xf],
                     lambda a, e: e[0] + a[0], F32, 1024, 512, "out_proj")

        h2 = _rmsnorm(x1, norm_ffn_g[i])
        t = _matmul(h2, [w_gate_ffn[i].astype(BF16), w_up_ffn[i].astype(BF16)], [],
                    lambda a, e: jax.nn.silu(a[0]) * a[1], BF16, 1024, 256, "ffn_gate_up")
        d_ff = t.shape[1]
        x2, x2g, ssq2 = _res_matmul(t, w_down_ffn[i].astype(BF16), x1, norm_ple_g[i],
                                    1024, 512, d_ff // 2, "ffn_down")

        pe = _ple(p[i].reshape(m, -1), w_ple[i].astype(BF16), ple_norm_g[i])
        xf = _matmul(x2g, [w_ple_gate[i].astype(BF16)], [x2, pe],
                     lambda a, e: e[0] + jax.nn.sigmoid(a[0]) * e[1], F32, 1024, 512, "ple_gate",
                     row_ssq=ssq2)
    return xf.reshape(batch, seq, d_model)
```

```python
import functools
import math

import numpy as np
import jax
import jax.numpy as jnp
from jax import lax
from jax.experimental import pallas as pl
from jax.experimental.pallas import tpu as pltpu

F32 = jnp.float32
BF16 = jnp.bfloat16
I32 = jnp.int32

N_HEADS = 16
HEAD_DIM = 128
N_IDX_HEADS = 32
IDX_DIM = 128
TOPK_MAX = 256
Q_BLOCK = 128
N_BUCKETS = 32
MAX_DISTANCE = 128
GMLP_WIDTH = 2048
N_GROUPS = 8
GROUP_DIM = GMLP_WIDTH // N_GROUPS
CHUNK = 128
EPS = 1e-6

LANES = 128
SUBLANES = 8
VMEM_LIMIT = 56 * 1024 * 1024
NEG = -(2.0 ** 100)
INT_MIN = -2 ** 31
KEY_NEG_INF = -2139095041
LOG2E = math.log2(math.e)

SCORE_CHUNK = 2 * LANES
COUNT_CHUNK = 2 * SCORE_CHUNK
FAR_CHUNK = 4 * LANES
NEAR_KEYS = 2 * Q_BLOCK
HEAD_GROUP = 4
VT_ROWS = HEAD_DIM + 16
MM_SUB_ROWS = 256


def _params(sem):
    return pltpu.CompilerParams(dimension_semantics=sem, vmem_limit_bytes=VMEM_LIMIT)


def _rmsnorm_kernel(x_ref, g_ref, o_ref):
    x = x_ref[...]
    ms = jnp.mean(x * x, axis=-1, keepdims=True)
    o_ref[...] = (x * lax.rsqrt(ms + EPS) * g_ref[...]).astype(o_ref.dtype)


def _rmsnorm(x, g, tm=256):
    m, d = x.shape
    return pl.pallas_call(
        _rmsnorm_kernel,
        grid=(m // tm,),
        in_specs=[pl.BlockSpec((tm, d), lambda i: (i, 0)),
                  pl.BlockSpec((1, d), lambda i: (0, 0))],
        out_specs=pl.BlockSpec((tm, d), lambda i: (i, 0)),
        out_shape=jax.ShapeDtypeStruct((m, d), BF16),
        compiler_params=_params(("parallel",)),
        name="rmsnorm",
    )(x, g.reshape(1, d))


def _lane_tiles(x):
    return [x[:, c * LANES:(c + 1) * LANES] for c in range(x.shape[1] // LANES)]


def _mm_kernel(*refs, n_w, n_x, epi, norm_dim):
    a_ref = refs[0]
    w_refs = refs[1:1 + n_w]
    x_refs = refs[1 + n_w:1 + n_w + n_x]
    if norm_dim:
        ssq_ref, o_ref, r_ref = refs[1 + n_w + n_x:]

        @pl.when(pl.program_id(1) == 0)
        def _():
            ms = jnp.sum(ssq_ref[...], axis=-1, keepdims=True) * (1.0 / norm_dim)
            r_ref[...] = jnp.broadcast_to(lax.rsqrt(ms + EPS), r_ref.shape)
    else:
        o_ref = refs[1 + n_w + n_x]
    for s in range(a_ref.shape[0] // MM_SUB_ROWS):
        rows = slice(s * MM_SUB_ROWS, (s + 1) * MM_SUB_ROWS)
        a = a_ref[rows, :]
        accs = [jnp.dot(a, w[...], preferred_element_type=F32) for w in w_refs]
        if norm_dim:
            accs = [acc * r_ref[rows, :] for acc in accs]
        o_ref[rows, :] = epi(accs, [x[rows, :] for x in x_refs]).astype(o_ref.dtype)


def _matmul(a, ws, extras, epi, out_dtype, tm, tn, name, row_ssq=None):
    m, k = a.shape
    n = ws[0].shape[1]
    normed = row_ssq is not None
    kern = functools.partial(_mm_kernel, n_w=len(ws), n_x=len(extras), epi=epi,
                             norm_dim=k if normed else 0)
    return pl.pallas_call(
        kern,
        grid=(m // tm, n // tn),
        in_specs=([pl.BlockSpec((tm, k), lambda i, j: (i, 0))]
                  + [pl.BlockSpec((k, tn), lambda i, j: (0, j)) for _ in ws]
                  + [pl.BlockSpec((tm, tn), lambda i, j: (i, j)) for _ in extras]
                  + ([pl.BlockSpec((tm, row_ssq.shape[1]), lambda i, j: (i, 0))] if normed else [])),
        out_specs=pl.BlockSpec((tm, tn), lambda i, j: (i, j)),
        out_shape=jax.ShapeDtypeStruct((m, n), out_dtype),
        scratch_shapes=[pltpu.VMEM((tm, tn), F32)] if normed else [],
        compiler_params=_params(("parallel", "arbitrary")),
        name=name,
    )(a, *ws, *extras, *([row_ssq] if normed else []))


def _res_mm_kernel(a_ref, w_ref, r_ref, g_ref, o_ref, og_ref, ssq_ref, *, nk):
    def sub_blocks(base_ref, last):
        for s in range(a_ref.shape[0] // MM_SUB_ROWS):
            rows = slice(s * MM_SUB_ROWS, (s + 1) * MM_SUB_ROWS)
            x = base_ref[rows, :] + jnp.dot(a_ref[rows, :], w_ref[...],
                                            preferred_element_type=F32)
            o_ref[rows, :] = x
            if last:
                og_ref[rows, :] = (x * g_ref[...]).astype(og_ref.dtype)
                ssq_ref[rows, :] = functools.reduce(lambda u, v: u + v, _lane_tiles(x * x))

    if nk == 1:
        sub_blocks(r_ref, True)
    else:
        @pl.when(pl.program_id(2) == 0)
        def _():
            sub_blocks(r_ref, False)

        @pl.when(pl.program_id(2) == 1)
        def _():
            sub_blocks(o_ref, True)


def _res_matmul(a, w, res, g_next, tm, tn, tk, name):
    m, k = a.shape
    n = w.shape[1]
    nk = k // tk
    assert nk in (1, 2) and nk * tk == k
    return pl.pallas_call(
        functools.partial(_res_mm_kernel, nk=nk),
        grid=(m // tm, n // tn, nk),
        in_specs=[pl.BlockSpec((tm, tk), lambda i, j, kk: (i, kk)),
                  pl.BlockSpec((tk, tn), lambda i, j, kk: (kk, j)),
                  pl.BlockSpec((tm, tn), lambda i, j, kk: (i, j)),
                  pl.BlockSpec((1, tn), lambda i, j, kk: (0, j))],
        out_specs=[pl.BlockSpec((tm, tn), lambda i, j, kk: (i, j)),
                   pl.BlockSpec((tm, tn), lambda i, j, kk: (i, j)),
                   pl.BlockSpec((tm, LANES), lambda i, j, kk: (i, j))],
        out_shape=[jax.ShapeDtypeStruct((m, n), F32),
                   jax.ShapeDtypeStruct((m, n), BF16),
                   jax.ShapeDtypeStruct((m, (n // tn) * LANES), F32)],
        compiler_params=_params(("parallel", "arbitrary", "arbitrary")),
        name=name,
    )(a, w, res, g_next.reshape(1, n))


def _headproj_kernel(*refs, heads, norm, post_scale):
    a_ref, w_ref = refs[0], refs[1]
    o_ref = refs[-1]
    acc = jnp.dot(a_ref[...], w_ref[...], preferred_element_type=F32)
    for hh in range(heads):
        blk = acc[:, hh * LANES:(hh + 1) * LANES]
        if norm:
            ms = jnp.mean(blk * blk, axis=-1, keepdims=True)
            blk = blk * lax.rsqrt(ms + EPS) * refs[2][...]
        if post_scale is not None:
            blk = blk * post_scale
        o_ref[hh] = blk.astype(o_ref.dtype)


def _headproj(h, w, g, name, post_scale=None, tm=1024, tn=1024):
    m, k = h.shape
    n_heads = w.shape[1] // LANES
    heads = tn // LANES
    norm = g is not None
    extra_specs = [pl.BlockSpec((1, LANES), lambda i, j: (0, 0))] if norm else []
    extra_args = [g.reshape(1, LANES)] if norm else []
    return pl.pallas_call(
        functools.partial(_headproj_kernel, heads=heads, norm=norm, post_scale=post_scale),
        grid=(m // tm, w.shape[1] // tn),
        in_specs=[pl.BlockSpec((tm, k), lambda i, j: (i, 0)),
                  pl.BlockSpec((k, tn), lambda i, j: (0, j))] + extra_specs,
        out_specs=pl.BlockSpec((heads, tm, LANES), lambda i, j: (j, i, 0)),
        out_shape=jax.ShapeDtypeStruct((n_heads, m, LANES), BF16),
        compiler_params=_params(("parallel", "arbitrary")),
        name=name,
    )(h, w, *extra_args)


def _small_kernel(a_ref, w_ref, g_ref, k_ref, v_ref, ki_ref, wi_ref):
    acc = jnp.dot(a_ref[...], w_ref[...], preferred_element_type=F32)
    kk = acc[:, 0:LANES]
    ms = jnp.mean(kk * kk, axis=-1, keepdims=True)
    k_ref[...] = (kk * lax.rsqrt(ms + EPS) * g_ref[...]).astype(k_ref.dtype)
    v_ref[...] = acc[:, LANES:2 * LANES].astype(v_ref.dtype)
    ki_ref[...] = acc[:, 2 * LANES:3 * LANES].astype(ki_ref.dtype)
    wi_ref[...] = acc[:, 3 * LANES:4 * LANES]


def _small_proj(h, w_small, g, tm=1024):
    m, k = h.shape
    row = lambda i: (i, 0)
    return pl.pallas_call(
        _small_kernel,
        grid=(m // tm,),
        in_specs=[pl.BlockSpec((tm, k), row),
                  pl.BlockSpec((k, 4 * LANES), lambda i: (0, 0)),
                  pl.BlockSpec((1, LANES), lambda i: (0, 0))],
        out_specs=[pl.BlockSpec((tm, LANES), row)] * 4,
        out_shape=[jax.ShapeDtypeStruct((m, LANES), BF16)] * 3
                  + [jax.ShapeDtypeStruct((m, LANES), F32)],
        compiler_params=_params(("parallel",)),
        name="kv_idx_proj",
    )(h, w_small, g.reshape(1, LANES))


def _attn_kernel(qi_ref, kidx_ref, wt_ref, q_ref, k_ref, vt_ref, near_ref, cvec_ref, eye_ref, o_ref,
                 key_ref, m_ref, acc_ref, qa_ref, s_ref,
                 *, k_sel, idx_scale):
    j = pl.program_id(1)
    nt = (((1,), (1,)), ((), ()))
    n_chunks = (j + 2) // 2

    key_row = lax.broadcasted_iota(I32, (SCORE_CHUNK, Q_BLOCK), 0)
    q_col = lax.broadcasted_iota(I32, (SCORE_CHUNK, Q_BLOCK), 1)

    def score_chunk(c, carry):
        start = pl.multiple_of(c * SCORE_CHUNK, SCORE_CHUNK)
        kc = kidx_ref[pl.ds(start, SCORE_CHUNK), :]
        acc = jnp.zeros((SCORE_CHUNK, Q_BLOCK), F32)
        for hp in range(N_IDX_HEADS // 2):
            qpair = qi_ref[2 * hp:2 * hp + 2].reshape(2 * Q_BLOCK, IDX_DIM)
            d = lax.dot_general(kc, qpair, nt, preferred_element_type=F32)
            acc = acc + jnp.maximum(d[:, :Q_BLOCK], 0.0) * wt_ref[2 * hp:2 * hp + 1, :]
            acc = acc + jnp.maximum(d[:, Q_BLOCK:], 0.0) * wt_ref[2 * hp + 1:2 * hp + 2, :]
        s = acc * idx_scale
        causal = (start + key_row) <= (j * Q_BLOCK + q_col)
        s = jnp.where(causal, s, -jnp.inf)
        bits = lax.bitcast_convert_type(s, I32)
        key_ref[pl.ds(start, SCORE_CHUNK), :] = bits ^ ((bits >> 31) & 0x7FFFFFFF)
        return carry

    lax.fori_loop(0, n_chunks, score_chunk, 0)

    @pl.when(n_chunks % 2 == 1)
    def _():
        key_ref[pl.ds(pl.multiple_of(n_chunks * SCORE_CHUNK, SCORE_CHUNK), SCORE_CHUNK), :] = (
            jnp.full((SCORE_CHUNK, Q_BLOCK), KEY_NEG_INF, I32))

    def bit_body(i, tu):
        cu = tu | jnp.left_shift(jnp.int32(1), 31 - i)
        cs = cu ^ INT_MIN

        def cnt_body(c, a):
            start = pl.multiple_of(c * COUNT_CHUNK, COUNT_CHUNK)
            ge = jnp.where(key_ref[pl.ds(start, COUNT_CHUNK), :] >= cs, 1, 0)
            return a + jnp.sum(ge.reshape(COUNT_CHUNK // SUBLANES, SUBLANES, Q_BLOCK), axis=0)

        a = lax.fori_loop(0, (n_chunks + 1) // 2, cnt_body, jnp.zeros((SUBLANES, Q_BLOCK), I32))
        cnt = jnp.sum(a, axis=0, keepdims=True)
        return jnp.where(cnt >= k_sel, cu, tu)

    tu = lax.fori_loop(0, 32, bit_body, jnp.zeros((1, Q_BLOCK), I32))
    thr = tu ^ INT_MIN

    m_ref[...] = jnp.full(m_ref.shape, NEG, F32)
    acc_ref[...] = jnp.zeros(acc_ref.shape, F32)
    qa_ref[:, :HEAD_DIM] = q_ref[...].reshape(N_HEADS * Q_BLOCK, HEAD_DIM)
    qa_ref[:, HEAD_DIM:] = eye_ref[...]

    def softmax_step(h, t):
        hs = slice(h * Q_BLOCK, (h + 1) * Q_BLOCK)
        m_old = m_ref[:, hs]
        m_new = jnp.maximum(m_old, jnp.max(t, axis=0, keepdims=True))
        m_ref[:, hs] = m_new
        return jnp.exp2(t - m_new).astype(BF16), jnp.exp2(m_old - m_new)

    def attend(keys, vt, logits_of):
        groups = [slice(g * HEAD_GROUP * Q_BLOCK, (g + 1) * HEAD_GROUP * Q_BLOCK)
                  for g in range(N_HEADS // HEAD_GROUP)]
        n_keys = keys.shape[0]
        for g, gs in enumerate(groups):
            s_ref[g, 0:n_keys, :] = lax.dot_general(keys, qa_ref[gs, 0:keys.shape[1]], nt,
                                                    preferred_element_type=F32)
        for g, gs in enumerate(groups):
            pa = [softmax_step(g * HEAD_GROUP + e,
                               logits_of(g * HEAD_GROUP + e,
                                         s_ref[g, 0:n_keys, e * Q_BLOCK:(e + 1) * Q_BLOCK]))
                  for e in range(HEAD_GROUP)]
            p = jnp.concatenate([x[0] for x in pa], axis=1)
            alpha = jnp.concatenate([x[1] for x in pa], axis=1)
            pv = jnp.dot(vt, p, preferred_element_type=F32)
            acc_ref[:, gs] = acc_ref[:, gs] * alpha + pv

    far_end = (j - 1) * Q_BLOCK
    far_row = lax.broadcasted_iota(I32, (FAR_CHUNK, Q_BLOCK), 0)

    def far_body(c, carry):
        lo = c * FAR_CHUNK
        start = pl.multiple_of(jnp.minimum(lo, jnp.maximum(far_end - FAR_CHUNK, 0)), LANES)
        pos = start + far_row
        sel = (key_ref[pl.ds(start, FAR_CHUNK), :] >= thr) & (pos >= lo) & (pos < far_end)
        maskb = jnp.where(sel, 0.0, NEG).astype(BF16)
        kaug = jnp.concatenate([k_ref[pl.ds(start, FAR_CHUNK), :], maskb], axis=1)
        vt = vt_ref[:, pl.ds(start, FAR_CHUNK)]
        attend(kaug, vt, lambda h, s: s)
        return carry

    lax.fori_loop(0, (jnp.maximum(far_end, 0) + FAR_CHUNK - 1) // FAR_CHUNK, far_body, 0)
    m_ref[...] = m_ref[...] + cvec_ref[...]

    near_start = pl.multiple_of(jnp.maximum(j - 1, 0) * Q_BLOCK, LANES)
    first = jnp.where(j == 0, 1, 0)
    sel = key_ref[pl.ds(near_start, NEAR_KEYS), :] >= thr
    maskb = jnp.where(sel, 0.0, NEG)
    k_near = k_ref[pl.ds(near_start, NEAR_KEYS), :]
    vt_near = vt_ref[:, pl.ds(near_start, NEAR_KEYS)]

    def near_logits(h, s):
        return s + near_ref[first, :, h * Q_BLOCK:(h + 1) * Q_BLOCK] + maskb

    attend(k_near, vt_near, near_logits)

    inv_l = 1.0 / acc_ref[HEAD_DIM:HEAD_DIM + 1, :]
    for h in range(N_HEADS):
        hs = slice(h * Q_BLOCK, (h + 1) * Q_BLOCK)
        o_ref[:, h * HEAD_DIM:(h + 1) * HEAD_DIM] = (
            acc_ref[0:HEAD_DIM, hs] * inv_l[:, hs]).T.astype(o_ref.dtype)


def _t5_bucket_static(n):
    max_exact = N_BUCKETS // 2
    nf = np.maximum(n, 1).astype(np.float32)
    large = max_exact + (np.log(nf / np.float32(max_exact)) / np.float32(math.log(MAX_DISTANCE / max_exact))
                         * np.float32(N_BUCKETS - max_exact)).astype(np.int32)
    large = np.minimum(large, N_BUCKETS - 1)
    return np.where(n < max_exact, n, large)


def _bias_tables(rel_bias, seq):
    buckets = _t5_bucket_static(np.arange(seq, dtype=np.int32))
    far = buckets[Q_BLOCK + 1:]
    assert (far == far[0]).all()
    bias2 = rel_bias.astype(F32) * LOG2E
    span, origin = 4 * NEAR_KEYS, 2 * NEAR_KEYS
    d = np.arange(span) - origin
    r = jnp.where(jnp.asarray(d >= 0)[None, :], bias2[buckets[np.clip(d, 0, seq - 1)]].T, NEG)
    y = jnp.tile(r, (1, NEAR_KEYS))[:, :NEAR_KEYS * (span - 1)].reshape(N_HEADS, NEAR_KEYS, span - 1)

    def table(off):
        t = y[:, :, origin + off:origin + off + Q_BLOCK]
        return jnp.transpose(t, (1, 0, 2)).reshape(NEAR_KEYS, N_HEADS * Q_BLOCK)

    near = jnp.stack([table(Q_BLOCK), table(0)])
    cvec = jnp.repeat(bias2[int(far[0])], Q_BLOCK).reshape(1, N_HEADS * Q_BLOCK)
    return near, cvec


def _attention(qi_hm, kidx, w_t, q_hm, k, v_t, near, cvec, batch, seq):
    m = batch * seq
    n_blk = seq // Q_BLOCK
    k_sel = min(TOPK_MAX, seq // 4)
    hq = N_HEADS * Q_BLOCK
    kern = functools.partial(
        _attn_kernel, k_sel=k_sel, idx_scale=IDX_DIM ** -0.5 * N_IDX_HEADS ** -0.5)
    blk = lambda b, j: (b * n_blk + j)
    v_t = jnp.concatenate([v_t, jnp.ones((VT_ROWS - HEAD_DIM, m), v_t.dtype)], axis=0)
    assert seq % COUNT_CHUNK == 0 and seq % FAR_CHUNK == 0
    eye = jnp.tile(jnp.eye(Q_BLOCK, dtype=BF16), (N_HEADS, 1))
    return pl.pallas_call(
        kern,
        grid=(batch, n_blk),
        in_specs=[pl.BlockSpec((N_IDX_HEADS, Q_BLOCK, IDX_DIM), lambda b, j: (0, blk(b, j), 0)),
                  pl.BlockSpec((seq, IDX_DIM), lambda b, j: (b, 0)),
                  pl.BlockSpec((N_IDX_HEADS, Q_BLOCK), lambda b, j: (0, blk(b, j))),
                  pl.BlockSpec((N_HEADS, Q_BLOCK, HEAD_DIM), lambda b, j: (0, blk(b, j), 0)),
                  pl.BlockSpec((seq, HEAD_DIM), lambda b, j: (b, 0)),
                  pl.BlockSpec((VT_ROWS, seq), lambda b, j: (0, b)),
                  pl.BlockSpec((2, NEAR_KEYS, hq), lambda b, j: (0, 0, 0)),
                  pl.BlockSpec((1, hq), lambda b, j: (0, 0)),
                  pl.BlockSpec((hq, Q_BLOCK), lambda b, j: (0, 0))],
        out_specs=pl.BlockSpec((Q_BLOCK, N_HEADS * HEAD_DIM), lambda b, j: (blk(b, j), 0)),
        out_shape=jax.ShapeDtypeStruct((m, N_HEADS * HEAD_DIM), BF16),
        scratch_shapes=[pltpu.VMEM((seq, Q_BLOCK), I32),
                        pltpu.VMEM((1, hq), F32),
                        pltpu.VMEM((VT_ROWS, hq), F32),
                        pltpu.VMEM((hq, HEAD_DIM + Q_BLOCK), BF16),
                        pltpu.VMEM((N_HEADS // HEAD_GROUP, FAR_CHUNK, HEAD_GROUP * Q_BLOCK), F32)],
        compiler_params=_params(("parallel", "arbitrary")),
        name="dsa_attention",
    )(qi_hm, kidx, w_t, q_hm, k, v_t, near, cvec, eye)


def _sgu_kernel(uv_ref, g_ref, b_ref, ws_ref, bs_ref, o_ref):
    uv = jax.nn.gelu(uv_ref[...])
    u = uv[:, :GMLP_WIDTH]
    v = uv[:, GMLP_WIDTH:]
    mu = jnp.mean(v, axis=-1, keepdims=True)
    vc = v - mu
    var = jnp.mean(vc * vc, axis=-1, keepdims=True)
    vn = (vc * lax.rsqrt(var + EPS) * g_ref[...] + b_ref[...]).astype(BF16)
    r = lax.broadcasted_iota(I32, (CHUNK, CHUNK), 0)
    c = lax.broadcasted_iota(I32, (CHUNK, CHUNK), 1)
    tril = c <= r
    bs = bs_ref[...]
    for g in range(N_GROUPS):
        ws = jnp.where(tril, ws_ref[g], 0.0).astype(BF16)
        mixed = jnp.dot(ws, vn[:, g * GROUP_DIM:(g + 1) * GROUP_DIM],
                        preferred_element_type=F32) + bs[:, g:g + 1]
        o_ref[:, g * GROUP_DIM:(g + 1) * GROUP_DIM] = (
            u[:, g * GROUP_DIM:(g + 1) * GROUP_DIM] * mixed).astype(o_ref.dtype)


def _sgu(uv, ln_g, ln_b, w_s, b_s):
    m = uv.shape[0]
    bs_t = jnp.zeros((CHUNK, LANES), F32).at[:, :N_GROUPS].set(jnp.transpose(b_s))
    return pl.pallas_call(
        _sgu_kernel,
        grid=(m // CHUNK,),
        in_specs=[pl.BlockSpec((CHUNK, 2 * GMLP_WIDTH), lambda i: (i, 0)),
                  pl.BlockSpec((1, GMLP_WIDTH), lambda i: (0, 0)),
                  pl.BlockSpec((1, GMLP_WIDTH), lambda i: (0, 0)),
                  pl.BlockSpec((N_GROUPS, CHUNK, CHUNK), lambda i: (0, 0, 0)),
                  pl.BlockSpec((CHUNK, LANES), lambda i: (0, 0))],
        out_specs=pl.BlockSpec((CHUNK, GMLP_WIDTH), lambda i: (i, 0)),
        out_shape=jax.ShapeDtypeStruct((m, GMLP_WIDTH), BF16),
        compiler_params=_params(("parallel",)),
        name="chunked_sgu",
    )(uv, ln_g.reshape(1, -1), ln_b.reshape(1, -1), w_s, bs_t)


def _merge_kernel(h_ref, ya_ref, yb_ref, wga_ref, wgb_ref, wa_ref, wb_ref, o_ref):
    h = h_ref[...]
    ga = jax.nn.sigmoid(jnp.dot(h, wga_ref[...], preferred_element_type=F32))
    a = jnp.dot(ya_ref[...], wa_ref[...], preferred_element_type=F32)
    out = ga * a
    gb = jax.nn.sigmoid(jnp.dot(h, wgb_ref[...], preferred_element_type=F32))
    b = jnp.dot(yb_ref[...], wb_ref[...], preferred_element_type=F32)
    o_ref[...] = (out + gb * b).astype(o_ref.dtype)


def _merge(h, ya, yb, wga, wgb, wa, wb, tm=512, tn=512):
    m, d = h.shape
    n = wga.shape[1]
    ka, kb = ya.shape[1], yb.shape[1]
    row = lambda i, j: (i, 0)
    colw = lambda i, j: (0, j)
    return pl.pallas_call(
        _merge_kernel,
        grid=(m // tm, n // tn),
        in_specs=[pl.BlockSpec((tm, d), row), pl.BlockSpec((tm, ka), row), pl.BlockSpec((tm, kb), row),
                  pl.BlockSpec((d, tn), colw), pl.BlockSpec((d, tn), colw),
                  pl.BlockSpec((ka, tn), colw), pl.BlockSpec((kb, tn), colw)],
        out_specs=pl.BlockSpec((tm, tn), lambda i, j: (i, j)),
        out_shape=jax.ShapeDtypeStruct((m, n), BF16),
        compiler_params=_params(("parallel", "arbitrary")),
        name="branch_merge",
    )(h, ya, yb, wga, wgb, wa, wb)


def _ple_kernel(p_ref, w_ref, g_ref, o_ref):
    acc = jnp.dot(p_ref[...].astype(BF16), w_ref[...], preferred_element_type=F32)
    ms = jnp.mean(acc * acc, axis=-1, keepdims=True)
    o_ref[...] = (acc * lax.rsqrt(ms + EPS) * g_ref[...]).astype(o_ref.dtype)


def _ple(p, w, g, tm=256):
    m, dp = p.shape
    d = w.shape[1]
    return pl.pallas_call(
        _ple_kernel,
        grid=(m // tm,),
        in_specs=[pl.BlockSpec((tm, dp), lambda i: (i, 0)),
                  pl.BlockSpec((dp, d), lambda i: (0, 0)),
                  pl.BlockSpec((1, d), lambda i: (0, 0))],
        out_specs=pl.BlockSpec((tm, d), lambda i: (i, 0)),
        out_shape=jax.ShapeDtypeStruct((m, d), F32),
        compiler_params=_params(("parallel",)),
        name="ple_embed",
    )(p, w, g.reshape(1, d))


def kernel(x, p, w_in, q_norm_g, k_norm_g, rel_bias, sgu_ln_g, sgu_ln_b, sgu_w, sgu_b, w_branch_a, w_branch_b, w_out, norm_mix_g, norm_ffn_g, w_gate_ffn, w_up_ffn, w_down_ffn, w_ple, ple_norm_g, w_ple_gate, norm_ple_g):
    batch, seq, d_model = x.shape
    depth = w_in.shape[0]
    m = batch * seq
    a_width = N_HEADS * HEAD_DIM
    sizes = (a_width, HEAD_DIM, HEAD_DIM, N_IDX_HEADS * IDX_DIM, IDX_DIM, N_IDX_HEADS,
             2 * GMLP_WIDTH, d_model, d_model)
    offs = np.concatenate([[0], np.cumsum(sizes)])
    near, cvec = _bias_tables(rel_bias, seq)

    xf = x.reshape(m, d_model)
    for i in range(depth):
        seg = lambda s: w_in[i][:, int(offs[s]):int(offs[s + 1])].astype(BF16)
        w_q, w_k, w_v, w_qi, w_ki, w_wi, w_uv, w_ga, w_gb = [seg(s) for s in range(9)]
        w_small = jnp.concatenate(
            [w_k, w_v, w_ki, jnp.pad(w_wi, ((0, 0), (0, LANES - N_IDX_HEADS)))], axis=1)

        h = _rmsnorm(xf, norm_mix_g[i])
        q_hm = _headproj(h, w_q, q_norm_g[i], "q_proj", post_scale=HEAD_DIM ** -0.5 * LOG2E)
        qi_hm = _headproj(h, w_qi, None, "q_idx_proj")
        k_, v_, kidx, widx = _small_proj(h, w_small, k_norm_g[i])
        uv = _matmul(h, [w_uv], [], lambda a, e: a[0], F32, 1024, 1024, "uv_proj")
        y_a = _attention(qi_hm, kidx, jnp.transpose(widx[:, :N_IDX_HEADS]), q_hm, k_,
                         jnp.transpose(v_), near, cvec, batch, seq)
        y_b = _sgu(uv, sgu_ln_g[i], sgu_ln_b[i], sgu_w[i], sgu_b[i])
        merged = _merge(h, y_a, y_b, w_ga, w_gb,
                        w_branch_a[i].astype(BF16), w_branch_b[i].astype(BF16))
        x1 = _matmul(merged, [w_out[i].astype(BF16)], [xf],
                     lambda a, e: e[0] + a[0], F32, 1024, 512, "out_proj")

        h2 = _rmsnorm(x1, norm_ffn_g[i])
        t = _matmul(h2, [w_gate_ffn[i].astype(BF16), w_up_ffn[i].astype(BF16)], [],
                    lambda a, e: jax.nn.silu(a[0]) * a[1], BF16, 1024, 256, "ffn_gate_up")
        d_ff = t.shape[1]
        x2, x2g, ssq2 = _res_matmul(t, w_down_ffn[i].astype(BF16), x1, norm_ple_g[i],
                                    1024, 512, d_ff // 2, "ffn_down")

        pe = _ple(p[i].reshape(m, -1), w_ple[i].astype(BF16), ple_norm_g[i])
        xf = _matmul(x2g, [w_ple_gate[i].astype(BF16)], [x2, pe],
                     lambda a, e: e[0] + jax.nn.sigmoid(a[0]) * e[1], F32, 1024, 512, "ple_gate",
                     row_ssq=ssq2)
    return xf.reshape(batch, seq, d_model)
```

```python
import functools
import math

import numpy as np
import jax
import jax.numpy as jnp
from jax import lax
from jax.experimental import pallas as pl
from jax.experimental.pallas import tpu as pltpu

F32 = jnp.float32
BF16 = jnp.bfloat16
I32 = jnp.int32

N_HEADS = 16
HEAD_DIM = 128
N_IDX_HEADS = 32
IDX_DIM = 128
TOPK_MAX = 256
Q_BLOCK = 128
N_BUCKETS = 32
MAX_DISTANCE = 128
GMLP_WIDTH = 2048
N_GROUPS = 8
GROUP_DIM = GMLP_WIDTH // N_GROUPS
CHUNK = 128
EPS = 1e-6

LANES = 128
SUBLANES = 8
VMEM_LIMIT = 56 * 1024 * 1024
NEG = -(2.0 ** 100)
INT_MIN = -2 ** 31
INT_MAX = 2 ** 31 - 1
KEY_NEG_INF = -2139095041
LOG2E = math.log2(math.e)

SCORE_CHUNK = 2 * LANES
COUNT_CHUNK = 2 * SCORE_CHUNK
FAR_CHUNK = 4 * LANES
NEAR_KEYS = 2 * Q_BLOCK
HEAD_GROUP = 4
VT_ROWS = HEAD_DIM + 16
MM_SUB_ROWS = 256


def _params(sem):
    return pltpu.CompilerParams(dimension_semantics=sem, vmem_limit_bytes=VMEM_LIMIT)


def _rmsnorm_kernel(x_ref, g_ref, o_ref):
    x = x_ref[...]
    ms = jnp.mean(x * x, axis=-1, keepdims=True)
    o_ref[...] = (x * lax.rsqrt(ms + EPS) * g_ref[...]).astype(o_ref.dtype)


def _rmsnorm(x, g, tm=256):
    m, d = x.shape
    return pl.pallas_call(
        _rmsnorm_kernel,
        grid=(m // tm,),
        in_specs=[pl.BlockSpec((tm, d), lambda i: (i, 0)),
                  pl.BlockSpec((1, d), lambda i: (0, 0))],
        out_specs=pl.BlockSpec((tm, d), lambda i: (i, 0)),
        out_shape=jax.ShapeDtypeStruct((m, d), BF16),
        compiler_params=_params(("parallel",)),
        name="rmsnorm",
    )(x, g.reshape(1, d))


def _lane_tiles(x):
    return [x[:, c * LANES:(c + 1) * LANES] for c in range(x.shape[1] // LANES)]


def _mm_kernel(*refs, n_w, n_x, epi, norm_dim):
    a_ref = refs[0]
    w_refs = refs[1:1 + n_w]
    x_refs = refs[1 + n_w:1 + n_w + n_x]
    if norm_dim:
        ssq_ref, o_ref, r_ref = refs[1 + n_w + n_x:]

        @pl.when(pl.program_id(1) == 0)
        def _():
            ms = jnp.sum(ssq_ref[...], axis=-1, keepdims=True) * (1.0 / norm_dim)
            r_ref[...] = jnp.broadcast_to(lax.rsqrt(ms + EPS), r_ref.shape)
    else:
        o_ref = refs[1 + n_w + n_x]
    for s in range(a_ref.shape[0] // MM_SUB_ROWS):
        rows = slice(s * MM_SUB_ROWS, (s + 1) * MM_SUB_ROWS)
        a = a_ref[rows, :]
        accs = [jnp.dot(a, w[...], preferred_element_type=F32) for w in w_refs]
        if norm_dim:
            accs = [acc * r_ref[rows, :] for acc in accs]
        o_ref[rows, :] = epi(accs, [x[rows, :] for x in x_refs]).astype(o_ref.dtype)


def _matmul(a, ws, extras, epi, out_dtype, tm, tn, name, row_ssq=None):
    m, k = a.shape
    n = ws[0].shape[1]
    normed = row_ssq is not None
    kern = functools.partial(_mm_kernel, n_w=len(ws), n_x=len(extras), epi=epi,
                             norm_dim=k if normed else 0)
    return pl.pallas_call(
        kern,
        grid=(m // tm, n // tn),
        in_specs=([pl.BlockSpec((tm, k), lambda i, j: (i, 0))]
                  + [pl.BlockSpec((k, tn), lambda i, j: (0, j)) for _ in ws]
                  + [pl.BlockSpec((tm, tn), lambda i, j: (i, j)) for _ in extras]
                  + ([pl.BlockSpec((tm, row_ssq.shape[1]), lambda i, j: (i, 0))] if normed else [])),
        out_specs=pl.BlockSpec((tm, tn), lambda i, j: (i, j)),
        out_shape=jax.ShapeDtypeStruct((m, n), out_dtype),
        scratch_shapes=[pltpu.VMEM((tm, tn), F32)] if normed else [],
        compiler_params=_params(("parallel", "arbitrary")),
        name=name,
    )(a, *ws, *extras, *([row_ssq] if normed else []))


def _res_mm_kernel(a_ref, w_ref, r_ref, g_ref, o_ref, og_ref, ssq_ref, *, nk):
    def sub_blocks(base_ref, last):
        for s in range(a_ref.shape[0] // MM_SUB_ROWS):
            rows = slice(s * MM_SUB_ROWS, (s + 1) * MM_SUB_ROWS)
            x = base_ref[rows, :] + jnp.dot(a_ref[rows, :], w_ref[...],
                                            preferred_element_type=F32)
            o_ref[rows, :] = x
            if last:
                og_ref[rows, :] = (x * g_ref[...]).astype(og_ref.dtype)
                ssq_ref[rows, :] = functools.reduce(lambda u, v: u + v, _lane_tiles(x * x))

    if nk == 1:
        sub_blocks(r_ref, True)
    else:
        @pl.when(pl.program_id(2) == 0)
        def _():
            sub_blocks(r_ref, False)

        @pl.when(pl.program_id(2) == 1)
        def _():
            sub_blocks(o_ref, True)


def _res_matmul(a, w, res, g_next, tm, tn, tk, name):
    m, k = a.shape
    n = w.shape[1]
    nk = k // tk
    assert nk in (1, 2) and nk * tk == k
    return pl.pallas_call(
        functools.partial(_res_mm_kernel, nk=nk),
        grid=(m // tm, n // tn, nk),
        in_specs=[pl.BlockSpec((tm, tk), lambda i, j, kk: (i, kk)),
                  pl.BlockSpec((tk, tn), lambda i, j, kk: (kk, j)),
                  pl.BlockSpec((tm, tn), lambda i, j, kk: (i, j)),
                  pl.BlockSpec((1, tn), lambda i, j, kk: (0, j))],
        out_specs=[pl.BlockSpec((tm, tn), lambda i, j, kk: (i, j)),
                   pl.BlockSpec((tm, tn), lambda i, j, kk: (i, j)),
                   pl.BlockSpec((tm, LANES), lambda i, j, kk: (i, j))],
        out_shape=[jax.ShapeDtypeStruct((m, n), F32),
                   jax.ShapeDtypeStruct((m, n), BF16),
                   jax.ShapeDtypeStruct((m, (n // tn) * LANES), F32)],
        compiler_params=_params(("parallel", "arbitrary", "arbitrary")),
        name=name,
    )(a, w, res, g_next.reshape(1, n))


def _headproj_kernel(*refs, heads, norm, post_scale):
    a_ref, w_ref = refs[0], refs[1]
    o_ref = refs[-1]
    acc = jnp.dot(a_ref[...], w_ref[...], preferred_element_type=F32)
    for hh in range(heads):
        blk = acc[:, hh * LANES:(hh + 1) * LANES]
        if norm:
            ms = jnp.mean(blk * blk, axis=-1, keepdims=True)
            blk = blk * lax.rsqrt(ms + EPS) * refs[2][...]
        if post_scale is not None:
            blk = blk * post_scale
        o_ref[hh] = blk.astype(o_ref.dtype)


def _headproj(h, w, g, name, post_scale=None, tm=1024, tn=1024):
    m, k = h.shape
    n_heads = w.shape[1] // LANES
    heads = tn // LANES
    norm = g is not None
    extra_specs = [pl.BlockSpec((1, LANES), lambda i, j: (0, 0))] if norm else []
    extra_args = [g.reshape(1, LANES)] if norm else []
    return pl.pallas_call(
        functools.partial(_headproj_kernel, heads=heads, norm=norm, post_scale=post_scale),
        grid=(m // tm, w.shape[1] // tn),
        in_specs=[pl.BlockSpec((tm, k), lambda i, j: (i, 0)),
                  pl.BlockSpec((k, tn), lambda i, j: (0, j))] + extra_specs,
        out_specs=pl.BlockSpec((heads, tm, LANES), lambda i, j: (j, i, 0)),
        out_shape=jax.ShapeDtypeStruct((n_heads, m, LANES), BF16),
        compiler_params=_params(("parallel", "arbitrary")),
        name=name,
    )(h, w, *extra_args)


def _small_kernel(a_ref, w_ref, g_ref, k_ref, v_ref, ki_ref, wi_ref):
    acc = jnp.dot(a_ref[...], w_ref[...], preferred_element_type=F32)
    kk = acc[:, 0:LANES]
    ms = jnp.mean(kk * kk, axis=-1, keepdims=True)
    k_ref[...] = (kk * lax.rsqrt(ms + EPS) * g_ref[...]).astype(k_ref.dtype)
    v_ref[...] = acc[:, LANES:2 * LANES].astype(v_ref.dtype)
    ki_ref[...] = acc[:, 2 * LANES:3 * LANES].astype(ki_ref.dtype)
    wi_ref[...] = acc[:, 3 * LANES:4 * LANES]


def _small_proj(h, w_small, g, tm=1024):
    m, k = h.shape
    row = lambda i: (i, 0)
    return pl.pallas_call(
        _small_kernel,
        grid=(m // tm,),
        in_specs=[pl.BlockSpec((tm, k), row),
                  pl.BlockSpec((k, 4 * LANES), lambda i: (0, 0)),
                  pl.BlockSpec((1, LANES), lambda i: (0, 0))],
        out_specs=[pl.BlockSpec((tm, LANES), row)] * 4,
        out_shape=[jax.ShapeDtypeStruct((m, LANES), BF16)] * 3
                  + [jax.ShapeDtypeStruct((m, LANES), F32)],
        compiler_params=_params(("parallel",)),
        name="kv_idx_proj",
    )(h, w_small, g.reshape(1, LANES))


def _attn_kernel(qi_ref, kidx_ref, wt_ref, q_ref, k_ref, vt_ref, near_ref, cvec_ref, eye_ref, o_ref,
                 key_ref, m_ref, acc_ref, qa_ref, s_ref, lim_ref,
                 *, k_sel, idx_scale, pos_bits):
    j = pl.program_id(1)
    nt = (((1,), (1,)), ((), ()))
    n_chunks = (j + 2) // 2

    key_row = lax.broadcasted_iota(I32, (SCORE_CHUNK, Q_BLOCK), 0)
    q_col = lax.broadcasted_iota(I32, (SCORE_CHUNK, Q_BLOCK), 1)

    def score_chunk(c, carry):
        start = pl.multiple_of(c * SCORE_CHUNK, SCORE_CHUNK)
        kc = kidx_ref[pl.ds(start, SCORE_CHUNK), :]
        acc = jnp.zeros((SCORE_CHUNK, Q_BLOCK), F32)
        for hp in range(N_IDX_HEADS // 2):
            qpair = qi_ref[2 * hp:2 * hp + 2].reshape(2 * Q_BLOCK, IDX_DIM)
            d = lax.dot_general(kc, qpair, nt, preferred_element_type=F32)
            acc = acc + jnp.maximum(d[:, :Q_BLOCK], 0.0) * wt_ref[2 * hp:2 * hp + 1, :]
            acc = acc + jnp.maximum(d[:, Q_BLOCK:], 0.0) * wt_ref[2 * hp + 1:2 * hp + 2, :]
        s = acc * idx_scale
        causal = (start + key_row) <= (j * Q_BLOCK + q_col)
        s = jnp.where(causal, s, -jnp.inf)
        bits = lax.bitcast_convert_type(s, I32)
        key_ref[pl.ds(start, SCORE_CHUNK), :] = bits ^ ((bits >> 31) & 0x7FFFFFFF)
        return carry

    lax.fori_loop(0, n_chunks, score_chunk, 0)

    @pl.when(n_chunks % 2 == 1)
    def _():
        key_ref[pl.ds(pl.multiple_of(n_chunks * SCORE_CHUNK, SCORE_CHUNK), SCORE_CHUNK), :] = (
            jnp.full((SCORE_CHUNK, Q_BLOCK), KEY_NEG_INF, I32))

    n_count = (n_chunks + 1) // 2
    count_row = lax.broadcasted_iota(I32, (COUNT_CHUNK, Q_BLOCK), 0)

    def count_keys(pred):
        def body(c, a):
            start = pl.multiple_of(c * COUNT_CHUNK, COUNT_CHUNK)
            hit = jnp.where(pred(key_ref[pl.ds(start, COUNT_CHUNK), :], start + count_row), 1, 0)
            return a + jnp.sum(hit.reshape(COUNT_CHUNK // SUBLANES, SUBLANES, Q_BLOCK), axis=0)

        a = lax.fori_loop(0, n_count, body, jnp.zeros((SUBLANES, Q_BLOCK), I32))
        return jnp.sum(a, axis=0, keepdims=True)

    def bit_body(i, carry):
        tu, cnt_tu = carry
        cu = tu | jnp.left_shift(jnp.int32(1), 31 - i)
        cs = cu ^ INT_MIN
        cnt = count_keys(lambda keys, pos: keys >= cs)
        take = cnt >= k_sel
        return jnp.where(take, cu, tu), jnp.where(take, cnt, cnt_tu)

    tu, cnt_ge = lax.fori_loop(
        0, 32, bit_body,
        (jnp.zeros((1, Q_BLOCK), I32), jnp.full((1, Q_BLOCK), n_count * COUNT_CHUNK, I32)))
    thr = tu ^ INT_MIN

    tied = (cnt_ge > k_sel) & (thr > KEY_NEG_INF)
    lim_ref[...] = jnp.full(lim_ref.shape, INT_MAX, I32)

    @pl.when(jnp.max(jnp.where(tied, 1, 0)) > 0)
    def _():
        n_tied_kept = k_sel - count_keys(lambda keys, pos: keys > thr)

        def pos_body(i, x):
            cand = x | jnp.left_shift(jnp.int32(1), pos_bits - 1 - i)
            cnt = count_keys(lambda keys, pos: (keys == thr) & (pos < cand))
            return jnp.where(cnt <= n_tied_kept, cand, x)

        x = lax.fori_loop(0, pos_bits, pos_body, jnp.zeros((1, Q_BLOCK), I32))
        lim_ref[...] = jnp.where(tied, x, INT_MAX)

    pos_lim = lim_ref[...]

    def selected(keys, pos):
        return (keys > thr) | ((keys == thr) & (pos < pos_lim))

    m_ref[...] = jnp.full(m_ref.shape, NEG, F32)
    acc_ref[...] = jnp.zeros(acc_ref.shape, F32)
    qa_ref[:, :HEAD_DIM] = q_ref[...].reshape(N_HEADS * Q_BLOCK, HEAD_DIM)
    qa_ref[:, HEAD_DIM:] = eye_ref[...]

    def softmax_step(h, t):
        hs = slice(h * Q_BLOCK, (h + 1) * Q_BLOCK)
        m_old = m_ref[:, hs]
        m_new = jnp.maximum(m_old, jnp.max(t, axis=0, keepdims=True))
        m_ref[:, hs] = m_new
        return jnp.exp2(t - m_new).astype(BF16), jnp.exp2(m_old - m_new)

    def attend(keys, vt, logits_of):
        groups = [slice(g * HEAD_GROUP * Q_BLOCK, (g + 1) * HEAD_GROUP * Q_BLOCK)
                  for g in range(N_HEADS // HEAD_GROUP)]
        n_keys = keys.shape[0]
        for g, gs in enumerate(groups):
            s_ref[g, 0:n_keys, :] = lax.dot_general(keys, qa_ref[gs, 0:keys.shape[1]], nt,
                                                    preferred_element_type=F32)
        for g, gs in enumerate(groups):
            pa = [softmax_step(g * HEAD_GROUP + e,
                               logits_of(g * HEAD_GROUP + e,
                                         s_ref[g, 0:n_keys, e * Q_BLOCK:(e + 1) * Q_BLOCK]))
                  for e in range(HEAD_GROUP)]
            p = jnp.concatenate([x[0] for x in pa], axis=1)
            alpha = jnp.concatenate([x[1] for x in pa], axis=1)
            pv = jnp.dot(vt, p, preferred_element_type=F32)
            acc_ref[:, gs] = acc_ref[:, gs] * alpha + pv

    far_end = (j - 1) * Q_BLOCK
    far_row = lax.broadcasted_iota(I32, (FAR_CHUNK, Q_BLOCK), 0)

    def far_body(c, carry):
        lo = c * FAR_CHUNK
        start = pl.multiple_of(jnp.minimum(lo, jnp.maximum(far_end - FAR_CHUNK, 0)), LANES)
        pos = start + far_row
        sel = selected(key_ref[pl.ds(start, FAR_CHUNK), :], pos) & (pos >= lo) & (pos < far_end)
        maskb = jnp.where(sel, 0.0, NEG).astype(BF16)
        kaug = jnp.concatenate([k_ref[pl.ds(start, FAR_CHUNK), :], maskb], axis=1)
        vt = vt_ref[:, pl.ds(start, FAR_CHUNK)]
        attend(kaug, vt, lambda h, s: s)
        return carry

    lax.fori_loop(0, (jnp.maximum(far_end, 0) + FAR_CHUNK - 1) // FAR_CHUNK, far_body, 0)
    m_ref[...] = m_ref[...] + cvec_ref[...]

    near_start = pl.multiple_of(jnp.maximum(j - 1, 0) * Q_BLOCK, LANES)
    first = jnp.where(j == 0, 1, 0)
    near_pos = near_start + lax.broadcasted_iota(I32, (NEAR_KEYS, Q_BLOCK), 0)
    sel = selected(key_ref[pl.ds(near_start, NEAR_KEYS), :], near_pos)
    maskb = jnp.where(sel, 0.0, NEG)
    k_near = k_ref[pl.ds(near_start, NEAR_KEYS), :]
    vt_near = vt_ref[:, pl.ds(near_start, NEAR_KEYS)]

    def near_logits(h, s):
        return s + near_ref[first, :, h * Q_BLOCK:(h + 1) * Q_BLOCK] + maskb

    attend(k_near, vt_near, near_logits)

    inv_l = 1.0 / acc_ref[HEAD_DIM:HEAD_DIM + 1, :]
    for h in range(N_HEADS):
        hs = slice(h * Q_BLOCK, (h + 1) * Q_BLOCK)
        o_ref[:, h * HEAD_DIM:(h + 1) * HEAD_DIM] = (
            acc_ref[0:HEAD_DIM, hs] * inv_l[:, hs]).T.astype(o_ref.dtype)


def _t5_bucket_static(n):
    max_exact = N_BUCKETS // 2
    nf = np.maximum(n, 1).astype(np.float32)
    large = max_exact + (np.log(nf / np.float32(max_exact)) / np.float32(math.log(MAX_DISTANCE / max_exact))
                         * np.float32(N_BUCKETS - max_exact)).astype(np.int32)
    large = np.minimum(large, N_BUCKETS - 1)
    return np.where(n < max_exact, n, large)


def _bias_tables(rel_bias, seq):
    buckets = _t5_bucket_static(np.arange(seq, dtype=np.int32))
    far = buckets[Q_BLOCK + 1:]
    assert (far == far[0]).all()
    bias2 = rel_bias.astype(F32) * LOG2E
    span, origin = 4 * NEAR_KEYS, 2 * NEAR_KEYS
    d = np.arange(span) - origin
    r = jnp.where(jnp.asarray(d >= 0)[None, :], bias2[buckets[np.clip(d, 0, seq - 1)]].T, NEG)
    y = jnp.tile(r, (1, NEAR_KEYS))[:, :NEAR_KEYS * (span - 1)].reshape(N_HEADS, NEAR_KEYS, span - 1)

    def table(off):
        t = y[:, :, origin + off:origin + off + Q_BLOCK]
        return jnp.transpose(t, (1, 0, 2)).reshape(NEAR_KEYS, N_HEADS * Q_BLOCK)

    near = jnp.stack([table(Q_BLOCK), table(0)])
    cvec = jnp.repeat(bias2[int(far[0])], Q_BLOCK).reshape(1, N_HEADS * Q_BLOCK)
    return near, cvec


def _attention(qi_hm, kidx, w_t, q_hm, k, v_t, near, cvec, batch, seq):
    m = batch * seq
    n_blk = seq // Q_BLOCK
    k_sel = min(TOPK_MAX, seq // 4)
    hq = N_HEADS * Q_BLOCK
    kern = functools.partial(
        _attn_kernel, k_sel=k_sel, idx_scale=IDX_DIM ** -0.5 * N_IDX_HEADS ** -0.5,
        pos_bits=seq.bit_length())
    blk = lambda b, j: (b * n_blk + j)
    v_t = jnp.concatenate([v_t, jnp.ones((VT_ROWS - HEAD_DIM, m), v_t.dtype)], axis=0)
    assert seq % COUNT_CHUNK == 0 and seq % FAR_CHUNK == 0
    eye = jnp.tile(jnp.eye(Q_BLOCK, dtype=BF16), (N_HEADS, 1))
    return pl.pallas_call(
        kern,
        grid=(batch, n_blk),
        in_specs=[pl.BlockSpec((N_IDX_HEADS, Q_BLOCK, IDX_DIM), lambda b, j: (0, blk(b, j), 0)),
                  pl.BlockSpec((seq, IDX_DIM), lambda b, j: (b, 0)),
                  pl.BlockSpec((N_IDX_HEADS, Q_BLOCK), lambda b, j: (0, blk(b, j))),
                  pl.BlockSpec((N_HEADS, Q_BLOCK, HEAD_DIM), lambda b, j: (0, blk(b, j), 0)),
                  pl.BlockSpec((seq, HEAD_DIM), lambda b, j: (b, 0)),
                  pl.BlockSpec((VT_ROWS, seq), lambda b, j: (0, b)),
                  pl.BlockSpec((2, NEAR_KEYS, hq), lambda b, j: (0, 0, 0)),
                  pl.BlockSpec((1, hq), lambda b, j: (0, 0)),
                  pl.BlockSpec((hq, Q_BLOCK), lambda b, j: (0, 0))],
        out_specs=pl.BlockSpec((Q_BLOCK, N_HEADS * HEAD_DIM), lambda b, j: (blk(b, j), 0)),
        out_shape=jax.ShapeDtypeStruct((m, N_HEADS * HEAD_DIM), BF16),
        scratch_shapes=[pltpu.VMEM((seq, Q_BLOCK), I32),
                        pltpu.VMEM((1, hq), F32),
                        pltpu.VMEM((VT_ROWS, hq), F32),
                        pltpu.VMEM((hq, HEAD_DIM + Q_BLOCK), BF16),
                        pltpu.VMEM((N_HEADS // HEAD_GROUP, FAR_CHUNK, HEAD_GROUP * Q_BLOCK), F32),
                        pltpu.VMEM((1, Q_BLOCK), I32)],
        compiler_params=_params(("parallel", "arbitrary")),
        name="dsa_attention",
    )(qi_hm, kidx, w_t, q_hm, k, v_t, near, cvec, eye)


def _sgu_kernel(uv_ref, g_ref, b_ref, ws_ref, bs_ref, o_ref):
    uv = jax.nn.gelu(uv_ref[...])
    u = uv[:, :GMLP_WIDTH]
    v = uv[:, GMLP_WIDTH:]
    mu = jnp.mean(v, axis=-1, keepdims=True)
    vc = v - mu
    var = jnp.mean(vc * vc, axis=-1, keepdims=True)
    vn = (vc * lax.rsqrt(var + EPS) * g_ref[...] + b_ref[...]).astype(BF16)
    r = lax.broadcasted_iota(I32, (CHUNK, CHUNK), 0)
    c = lax.broadcasted_iota(I32, (CHUNK, CHUNK), 1)
    tril = c <= r
    bs = bs_ref[...]
    for g in range(N_GROUPS):
        ws = jnp.where(tril, ws_ref[g], 0.0).astype(BF16)
        mixed = jnp.dot(ws, vn[:, g * GROUP_DIM:(g + 1) * GROUP_DIM],
                        preferred_element_type=F32) + bs[:, g:g + 1]
        o_ref[:, g * GROUP_DIM:(g + 1) * GROUP_DIM] = (
            u[:, g * GROUP_DIM:(g + 1) * GROUP_DIM] * mixed).astype(o_ref.dtype)


def _sgu(uv, ln_g, ln_b, w_s, b_s):
    m = uv.shape[0]
    bs_t = jnp.zeros((CHUNK, LANES), F32).at[:, :N_GROUPS].set(jnp.transpose(b_s))
    return pl.pallas_call(
        _sgu_kernel,
        grid=(m // CHUNK,),
        in_specs=[pl.BlockSpec((CHUNK, 2 * GMLP_WIDTH), lambda i: (i, 0)),
                  pl.BlockSpec((1, GMLP_WIDTH), lambda i: (0, 0)),
                  pl.BlockSpec((1, GMLP_WIDTH), lambda i: (0, 0)),
                  pl.BlockSpec((N_GROUPS, CHUNK, CHUNK), lambda i: (0, 0, 0)),
                  pl.BlockSpec((CHUNK, LANES), lambda i: (0, 0))],
        out_specs=pl.BlockSpec((CHUNK, GMLP_WIDTH), lambda i: (i, 0)),
        out_shape=jax.ShapeDtypeStruct((m, GMLP_WIDTH), BF16),
        compiler_params=_params(("parallel",)),
        name="chunked_sgu",
    )(uv, ln_g.reshape(1, -1), ln_b.reshape(1, -1), w_s, bs_t)


def _merge_kernel(h_ref, ya_ref, yb_ref, wga_ref, wgb_ref, wa_ref, wb_ref, o_ref):
    h = h_ref[...]
    ga = jax.nn.sigmoid(jnp.dot(h, wga_ref[...], preferred_element_type=F32))
    a = jnp.dot(ya_ref[...], wa_ref[...], preferred_element_type=F32)
    out = ga * a
    gb = jax.nn.sigmoid(jnp.dot(h, wgb_ref[...], preferred_element_type=F32))
    b = jnp.dot(yb_ref[...], wb_ref[...], preferred_element_type=F32)
    o_ref[...] = (out + gb * b).astype(o_ref.dtype)


def _merge(h, ya, yb, wga, wgb, wa, wb, tm=512, tn=512):
    m, d = h.shape
    n = wga.shape[1]
    ka, kb = ya.shape[1], yb.shape[1]
    row = lambda i, j: (i, 0)
    colw = lambda i, j: (0, j)
    return pl.pallas_call(
        _merge_kernel,
        grid=(m // tm, n // tn),
        in_specs=[pl.BlockSpec((tm, d), row), pl.BlockSpec((tm, ka), row), pl.BlockSpec((tm, kb), row),
                  pl.BlockSpec((d, tn), colw), pl.BlockSpec((d, tn), colw),
                  pl.BlockSpec((ka, tn), colw), pl.BlockSpec((kb, tn), colw)],
        out_specs=pl.BlockSpec((tm, tn), lambda i, j: (i, j)),
        out_shape=jax.ShapeDtypeStruct((m, n), BF16),
        compiler_params=_params(("parallel", "arbitrary")),
        name="branch_merge",
    )(h, ya, yb, wga, wgb, wa, wb)


def _ple_kernel(p_ref, w_ref, g_ref, o_ref):
    acc = jnp.dot(p_ref[...].astype(BF16), w_ref[...], preferred_element_type=F32)
    ms = jnp.mean(acc * acc, axis=-1, keepdims=True)
    o_ref[...] = (acc * lax.rsqrt(ms + EPS) * g_ref[...]).astype(o_ref.dtype)


def _ple(p, w, g, tm=256):
    m, dp = p.shape
    d = w.shape[1]
    return pl.pallas_call(
        _ple_kernel,
        grid=(m // tm,),
        in_specs=[pl.BlockSpec((tm, dp), lambda i: (i, 0)),
                  pl.BlockSpec((dp, d), lambda i: (0, 0)),
                  pl.BlockSpec((1, d), lambda i: (0, 0))],
        out_specs=pl.BlockSpec((tm, d), lambda i: (i, 0)),
        out_shape=jax.ShapeDtypeStruct((m, d), F32),
        compiler_params=_params(("parallel",)),
        name="ple_embed",
    )(p, w, g.reshape(1, d))


def kernel(x, p, w_in, q_norm_g, k_norm_g, rel_bias, sgu_ln_g, sgu_ln_b, sgu_w, sgu_b, w_branch_a, w_branch_b, w_out, norm_mix_g, norm_ffn_g, w_gate_ffn, w_up_ffn, w_down_ffn, w_ple, ple_norm_g, w_ple_gate, norm_ple_g):
    batch, seq, d_model = x.shape
    depth = w_in.shape[0]
    m = batch * seq
    a_width = N_HEADS * HEAD_DIM
    sizes = (a_width, HEAD_DIM, HEAD_DIM, N_IDX_HEADS * IDX_DIM, IDX_DIM, N_IDX_HEADS,
             2 * GMLP_WIDTH, d_model, d_model)
    offs = np.concatenate([[0], np.cumsum(sizes)])
    near, cvec = _bias_tables(rel_bias, seq)

    xf = x.reshape(m, d_model)
    for i in range(depth):
        seg = lambda s: w_in[i][:, int(offs[s]):int(offs[s + 1])].astype(BF16)
        w_q, w_k, w_v, w_qi, w_ki, w_wi, w_uv, w_ga, w_gb = [seg(s) for s in range(9)]
        w_small = jnp.concatenate(
            [w_k, w_v, w_ki, jnp.pad(w_wi, ((0, 0), (0, LANES - N_IDX_HEADS)))], axis=1)

        h = _rmsnorm(xf, norm_mix_g[i])
        q_hm = _headproj(h, w_q, q_norm_g[i], "q_proj", post_scale=HEAD_DIM ** -0.5 * LOG2E)
        qi_hm = _headproj(h, w_qi, None, "q_idx_proj")
        k_, v_, kidx, widx = _small_proj(h, w_small, k_norm_g[i])
        uv = _matmul(h, [w_uv], [], lambda a, e: a[0], F32, 1024, 1024, "uv_proj")
        y_a = _attention(qi_hm, kidx, jnp.transpose(widx[:, :N_IDX_HEADS]), q_hm, k_,
                         jnp.transpose(v_), near, cvec, batch, seq)
        y_b = _sgu(uv, sgu_ln_g[i], sgu_ln_b[i], sgu_w[i], sgu_b[i])
        merged = _merge(h, y_a, y_b, w_ga, w_gb,
                        w_branch_a[i].astype(BF16), w_branch_b[i].astype(BF16))
        x1 = _matmul(merged, [w_out[i].astype(BF16)], [xf],
                     lambda a, e: e[0] + a[0], F32, 1024, 512, "out_proj")

        h2 = _rmsnorm(x1, norm_ffn_g[i])
        t = _matmul(h2, [w_gate_ffn[i].astype(BF16), w_up_ffn[i].astype(BF16)], [],
                    lambda a, e: jax.nn.silu(a[0]) * a[1], BF16, 1024, 256, "ffn_gate_up")
        d_ff = t.shape[1]
        x2, x2g, ssq2 = _res_matmul(t, w_down_ffn[i].astype(BF16), x1, norm_ple_g[i],
                                    1024, 512, d_ff // 2, "ffn_down")

        pe = _ple(p[i].reshape(m, -1), w_ple[i].astype(BF16), ple_norm_g[i])
        xf = _matmul(x2g, [w_ple_gate[i].astype(BF16)], [x2, pe],
                     lambda a, e: e[0] + jax.nn.sigmoid(a[0]) * e[1], F32, 1024, 512, "ple_gate",
                     row_ssq=ssq2)
    return xf.reshape(batch, seq, d_model)
```

```python
import functools
import math

import numpy as np
import jax
import jax.numpy as jnp
from jax import lax
from jax.experimental import pallas as pl
from jax.experimental.pallas import tpu as pltpu

F32 = jnp.float32
BF16 = jnp.bfloat16
I32 = jnp.int32

N_HEADS = 16
HEAD_DIM = 128
N_IDX_HEADS = 32
IDX_DIM = 128
TOPK_MAX = 256
Q_BLOCK = 128
N_BUCKETS = 32
MAX_DISTANCE = 128
GMLP_WIDTH = 2048
N_GROUPS = 8
GROUP_DIM = GMLP_WIDTH // N_GROUPS
CHUNK = 128
EPS = 1e-6

LANES = 128
SUBLANES = 8
VMEM_LIMIT = 56 * 1024 * 1024
NEG = -(2.0 ** 100)
INT_MIN = -2 ** 31
INT_MAX = 2 ** 31 - 1
KEY_NEG_INF = -2139095041
F32_NEG_INF_BITS = -8388608
HI16_MASK = -65536
PACKED_SUBLANES = 16
LOG2E = math.log2(math.e)

SCORE_CHUNK = 2 * LANES
COUNT_CHUNK = 2 * SCORE_CHUNK
FAR_CHUNK = 4 * LANES
NEAR_KEYS = 2 * Q_BLOCK
HEAD_GROUP = 4
VT_ROWS = HEAD_DIM + 16
MM_SUB_ROWS = 256


def _params(sem):
    return pltpu.CompilerParams(dimension_semantics=sem, vmem_limit_bytes=VMEM_LIMIT)


def _rmsnorm_kernel(x_ref, g_ref, o_ref):
    x = x_ref[...]
    ms = jnp.mean(x * x, axis=-1, keepdims=True)
    o_ref[...] = (x * lax.rsqrt(ms + EPS) * g_ref[...]).astype(o_ref.dtype)


def _rmsnorm(x, g, tm=256):
    m, d = x.shape
    return pl.pallas_call(
        _rmsnorm_kernel,
        grid=(m // tm,),
        in_specs=[pl.BlockSpec((tm, d), lambda i: (i, 0)),
                  pl.BlockSpec((1, d), lambda i: (0, 0))],
        out_specs=pl.BlockSpec((tm, d), lambda i: (i, 0)),
        out_shape=jax.ShapeDtypeStruct((m, d), BF16),
        compiler_params=_params(("parallel",)),
        name="rmsnorm",
    )(x, g.reshape(1, d))


def _lane_tiles(x):
    return [x[:, c * LANES:(c + 1) * LANES] for c in range(x.shape[1] // LANES)]


def _mm_kernel(*refs, n_w, n_x, epi, norm_dim):
    a_ref = refs[0]
    w_refs = refs[1:1 + n_w]
    x_refs = refs[1 + n_w:1 + n_w + n_x]
    if norm_dim:
        ssq_ref, o_ref, r_ref = refs[1 + n_w + n_x:]

        @pl.when(pl.program_id(1) == 0)
        def _():
            ms = jnp.sum(ssq_ref[...], axis=-1, keepdims=True) * (1.0 / norm_dim)
            r_ref[...] = jnp.broadcast_to(lax.rsqrt(ms + EPS), r_ref.shape)
    else:
        o_ref = refs[1 + n_w + n_x]
    for s in range(a_ref.shape[0] // MM_SUB_ROWS):
        rows = slice(s * MM_SUB_ROWS, (s + 1) * MM_SUB_ROWS)
        a = a_ref[rows, :]
        accs = [jnp.dot(a, w[...], preferred_element_type=F32) for w in w_refs]
        if norm_dim:
            accs = [acc * r_ref[rows, :] for acc in accs]
        o_ref[rows, :] = epi(accs, [x[rows, :] for x in x_refs]).astype(o_ref.dtype)


def _matmul(a, ws, extras, epi, out_dtype, tm, tn, name, row_ssq=None):
    m, k = a.shape
    n = ws[0].shape[1]
    normed = row_ssq is not None
    kern = functools.partial(_mm_kernel, n_w=len(ws), n_x=len(extras), epi=epi,
                             norm_dim=k if normed else 0)
    return pl.pallas_call(
        kern,
        grid=(m // tm, n // tn),
        in_specs=([pl.BlockSpec((tm, k), lambda i, j: (i, 0))]
                  + [pl.BlockSpec((k, tn), lambda i, j: (0, j)) for _ in ws]
                  + [pl.BlockSpec((tm, tn), lambda i, j: (i, j)) for _ in extras]
                  + ([pl.BlockSpec((tm, row_ssq.shape[1]), lambda i, j: (i, 0))] if normed else [])),
        out_specs=pl.BlockSpec((tm, tn), lambda i, j: (i, j)),
        out_shape=jax.ShapeDtypeStruct((m, n), out_dtype),
        scratch_shapes=[pltpu.VMEM((tm, tn), F32)] if normed else [],
        compiler_params=_params(("parallel", "arbitrary")),
        name=name,
    )(a, *ws, *extras, *([row_ssq] if normed else []))


def _res_mm_kernel(a_ref, w_ref, r_ref, g_ref, o_ref, og_ref, ssq_ref, *, nk):
    def sub_blocks(base_ref, last):
        for s in range(a_ref.shape[0] // MM_SUB_ROWS):
            rows = slice(s * MM_SUB_ROWS, (s + 1) * MM_SUB_ROWS)
            x = base_ref[rows, :] + jnp.dot(a_ref[rows, :], w_ref[...],
                                            preferred_element_type=F32)
            o_ref[rows, :] = x
            if last:
                og_ref[rows, :] = (x * g_ref[...]).astype(og_ref.dtype)
                ssq_ref[rows, :] = functools.reduce(lambda u, v: u + v, _lane_tiles(x * x))

    if nk == 1:
        sub_blocks(r_ref, True)
    else:
        @pl.when(pl.program_id(2) == 0)
        def _():
            sub_blocks(r_ref, False)

        @pl.when(pl.program_id(2) == 1)
        def _():
            sub_blocks(o_ref, True)


def _res_matmul(a, w, res, g_next, tm, tn, tk, name):
    m, k = a.shape
    n = w.shape[1]
    nk = k // tk
    assert nk in (1, 2) and nk * tk == k
    return pl.pallas_call(
        functools.partial(_res_mm_kernel, nk=nk),
        grid=(m // tm, n // tn, nk),
        in_specs=[pl.BlockSpec((tm, tk), lambda i, j, kk: (i, kk)),
                  pl.BlockSpec((tk, tn), lambda i, j, kk: (kk, j)),
                  pl.BlockSpec((tm, tn), lambda i, j, kk: (i, j)),
                  pl.BlockSpec((1, tn), lambda i, j, kk: (0, j))],
        out_specs=[pl.BlockSpec((tm, tn), lambda i, j, kk: (i, j)),
                   pl.BlockSpec((tm, tn), lambda i, j, kk: (i, j)),
                   pl.BlockSpec((tm, LANES), lambda i, j, kk: (i, j))],
        out_shape=[jax.ShapeDtypeStruct((m, n), F32),
                   jax.ShapeDtypeStruct((m, n), BF16),
                   jax.ShapeDtypeStruct((m, (n // tn) * LANES), F32)],
        compiler_params=_params(("parallel", "arbitrary", "arbitrary")),
        name=name,
    )(a, w, res, g_next.reshape(1, n))


def _headproj_kernel(*refs, heads, norm, post_scale):
    a_ref, w_ref = refs[0], refs[1]
    o_ref = refs[-1]
    acc = jnp.dot(a_ref[...], w_ref[...], preferred_element_type=F32)
    for hh in range(heads):
        blk = acc[:, hh * LANES:(hh + 1) * LANES]
        if norm:
            ms = jnp.mean(blk * blk, axis=-1, keepdims=True)
            blk = blk * lax.rsqrt(ms + EPS) * refs[2][...]
        if post_scale is not None:
            blk = blk * post_scale
        o_ref[hh] = blk.astype(o_ref.dtype)


def _headproj(h, w, g, name, post_scale=None, tm=1024, tn=1024):
    m, k = h.shape
    n_heads = w.shape[1] // LANES
    heads = tn // LANES
    norm = g is not None
    extra_specs = [pl.BlockSpec((1, LANES), lambda i, j: (0, 0))] if norm else []
    extra_args = [g.reshape(1, LANES)] if norm else []
    return pl.pallas_call(
        functools.partial(_headproj_kernel, heads=heads, norm=norm, post_scale=post_scale),
        grid=(m // tm, w.shape[1] // tn),
        in_specs=[pl.BlockSpec((tm, k), lambda i, j: (i, 0)),
                  pl.BlockSpec((k, tn), lambda i, j: (0, j))] + extra_specs,
        out_specs=pl.BlockSpec((heads, tm, LANES), lambda i, j: (j, i, 0)),
        out_shape=jax.ShapeDtypeStruct((n_heads, m, LANES), BF16),
        compiler_params=_params(("parallel", "arbitrary")),
        name=name,
    )(h, w, *extra_args)


def _small_kernel(a_ref, w_ref, g_ref, k_ref, v_ref, ki_ref, wi_ref):
    acc = jnp.dot(a_ref[...], w_ref[...], preferred_element_type=F32)
    kk = acc[:, 0:LANES]
    ms = jnp.mean(kk * kk, axis=-1, keepdims=True)
    k_ref[...] = (kk * lax.rsqrt(ms + EPS) * g_ref[...]).astype(k_ref.dtype)
    v_ref[...] = acc[:, LANES:2 * LANES].astype(v_ref.dtype)
    ki_ref[...] = acc[:, 2 * LANES:3 * LANES].astype(ki_ref.dtype)
    wi_ref[...] = acc[:, 3 * LANES:4 * LANES]


def _small_proj(h, w_small, g, tm=1024):
    m, k = h.shape
    row = lambda i: (i, 0)
    return pl.pallas_call(
        _small_kernel,
        grid=(m // tm,),
        in_specs=[pl.BlockSpec((tm, k), row),
                  pl.BlockSpec((k, 4 * LANES), lambda i: (0, 0)),
                  pl.BlockSpec((1, LANES), lambda i: (0, 0))],
        out_specs=[pl.BlockSpec((tm, LANES), row)] * 4,
        out_shape=[jax.ShapeDtypeStruct((m, LANES), BF16)] * 3
                  + [jax.ShapeDtypeStruct((m, LANES), F32)],
        compiler_params=_params(("parallel",)),
        name="kv_idx_proj",
    )(h, w_small, g.reshape(1, LANES))


def _attn_kernel(qi_ref, kidx_ref, wt_ref, q_ref, k_ref, vt_ref, near_ref, cvec_ref, eye_ref, o_ref,
                 key_ref, m_ref, acc_ref, qa_ref, s_ref, lim_ref, hi_ref, b1_ref, b0_ref, vb_ref,
                 *, k_sel, idx_scale, pos_bits):
    j = pl.program_id(1)
    nt = (((1,), (1,)), ((), ()))
    n_chunks = (j + 2) // 2

    key_row = lax.broadcasted_iota(I32, (SCORE_CHUNK, Q_BLOCK), 0)
    q_col = lax.broadcasted_iota(I32, (SCORE_CHUNK, Q_BLOCK), 1)

    def score_chunk(c, carry):
        start = pl.multiple_of(c * SCORE_CHUNK, SCORE_CHUNK)
        kc = kidx_ref[pl.ds(start, SCORE_CHUNK), :]
        acc = jnp.zeros((SCORE_CHUNK, Q_BLOCK), F32)
        for hp in range(N_IDX_HEADS // 2):
            qpair = qi_ref[2 * hp:2 * hp + 2].reshape(2 * Q_BLOCK, IDX_DIM)
            d = lax.dot_general(kc, qpair, nt, preferred_element_type=F32)
            acc = acc + jnp.maximum(d[:, :Q_BLOCK], 0.0) * wt_ref[2 * hp:2 * hp + 1, :]
            acc = acc + jnp.maximum(d[:, Q_BLOCK:], 0.0) * wt_ref[2 * hp + 1:2 * hp + 2, :]
        s = acc * idx_scale
        causal = (start + key_row) <= (j * Q_BLOCK + q_col)
        s = jnp.where(causal, s, -jnp.inf)
        store_keys(pl.ds(start, SCORE_CHUNK), lax.bitcast_convert_type(s, I32))
        return carry

    def store_keys(rows, bits):
        key = bits ^ ((bits >> 31) & 0x7FFFFFFF)
        key_ref[rows, :] = key
        hi_ref[rows, :] = lax.bitcast_convert_type(bits & HI16_MASK, F32).astype(BF16)
        b1_ref[rows, :] = ((key >> 8) & 0xFF).astype(F32).astype(BF16)
        b0_ref[rows, :] = (key & 0xFF).astype(F32).astype(BF16)

    lax.fori_loop(0, n_chunks, score_chunk, 0)

    @pl.when(n_chunks % 2 == 1)
    def _():
        store_keys(pl.ds(pl.multiple_of(n_chunks * SCORE_CHUNK, SCORE_CHUNK), SCORE_CHUNK),
                   jnp.full((SCORE_CHUNK, Q_BLOCK), F32_NEG_INF_BITS, I32))

    n_count = (n_chunks + 1) // 2
    count_row = lax.broadcasted_iota(I32, (COUNT_CHUNK, Q_BLOCK), 0)

    def count_keys(pred):
        def body(c, a):
            start = pl.multiple_of(c * COUNT_CHUNK, COUNT_CHUNK)
            hit = jnp.where(pred(key_ref[pl.ds(start, COUNT_CHUNK), :], start + count_row), 1, 0)
            return a + jnp.sum(hit.reshape(COUNT_CHUNK // SUBLANES, SUBLANES, Q_BLOCK), axis=0)

        a = lax.fori_loop(0, n_count, body, jnp.zeros((SUBLANES, Q_BLOCK), I32))
        return jnp.sum(a, axis=0, keepdims=True)

    def count_packed(ref, preds, prep=None):
        def body(c, accs):
            rows = pl.ds(pl.multiple_of(c * COUNT_CHUNK, COUNT_CHUNK), COUNT_CHUNK)
            vals = ref[rows, :]
            if prep is not None:
                prep(rows, vals)
            out = []
            for a, pred in zip(accs, preds):
                hit = jnp.where(pred(vals), jnp.bfloat16(1), jnp.bfloat16(0))
                parts = [hit[r * PACKED_SUBLANES:(r + 1) * PACKED_SUBLANES, :]
                         for r in range(COUNT_CHUNK // PACKED_SUBLANES)]
                while len(parts) > 1:
                    parts = [u + v for u, v in zip(parts[0::2], parts[1::2])]
                out.append(a + parts[0])
            return tuple(out)

        accs = lax.fori_loop(0, n_count, body,
                             tuple(jnp.zeros((PACKED_SUBLANES, Q_BLOCK), BF16) for _ in preds))
        return [jnp.sum(a.astype(F32), axis=0, keepdims=True).astype(I32) for a in accs]

    def digit_search(ref, n_bits, value_of, need, count_all):
        def body(i, carry):
            t, cnt_t = carry
            cand = t | jnp.left_shift(jnp.int32(1), n_bits - 1 - i)
            cand_v = value_of(cand)
            cnt, = count_packed(ref, [lambda v: v >= cand_v])
            take = cnt >= need
            return jnp.where(take, cand, t), jnp.where(take, cnt, cnt_t)

        return lax.fori_loop(0, n_bits, body, (jnp.zeros((1, Q_BLOCK), I32), count_all))

    def hi_value(pattern):
        cs = pattern ^ 0x8000
        fbits = jnp.where((cs & 0x8000) != 0, cs ^ 0x7FFF, cs)
        return lax.bitcast_convert_type(jnp.left_shift(fbits, 16), F32).astype(BF16)

    def byte_value(pattern):
        return pattern.astype(F32).astype(BF16)

    k_need = jnp.full((1, Q_BLOCK), k_sel, I32)
    n_all = jnp.full((1, Q_BLOCK), n_count * COUNT_CHUNK, I32)
    t_hi, cnt_hi = digit_search(hi_ref, 16, hi_value, k_need, n_all)
    all_keys = t_hi < 0x80
    v_hi = hi_value(t_hi)

    def keep_digit(src_ref, v_sel):
        def prep(rows, vals):
            vb_ref[rows, :] = jnp.where(vals == v_sel, src_ref[rows, :], -1.0).astype(BF16)
        return prep

    cnt_gt_hi, = count_packed(hi_ref, [lambda v: v > v_hi], prep=keep_digit(b1_ref, v_hi))
    t_b1, cnt_b1 = digit_search(vb_ref, 8, byte_value, k_need - cnt_gt_hi, cnt_hi - cnt_gt_hi)
    v_b1 = byte_value(t_b1)
    cnt_gt_b1, = count_packed(vb_ref, [lambda v: v > v_b1], prep=keep_digit(b0_ref, v_b1))
    t_b0, cnt_b0 = digit_search(vb_ref, 8, byte_value, k_need - cnt_gt_hi - cnt_gt_b1,
                                cnt_b1 - cnt_gt_b1)
    key_hi = (t_hi ^ 0x8000) - jnp.left_shift((t_hi ^ 0x8000) & 0x8000, 1)
    thr = jnp.where(all_keys, INT_MIN,
                    jnp.left_shift(key_hi, 16) | jnp.left_shift(t_b1, 8) | t_b0)
    cnt_ge = cnt_gt_hi + cnt_gt_b1 + cnt_b0

    tied = (cnt_ge > k_sel) & (thr > KEY_NEG_INF)
    lim_ref[...] = jnp.full(lim_ref.shape, INT_MAX, I32)

    @pl.when(jnp.max(jnp.where(tied, 1, 0)) > 0)
    def _():
        n_tied_kept = k_sel - count_keys(lambda keys, pos: keys > thr)

        def pos_body(i, x):
            cand = x | jnp.left_shift(jnp.int32(1), pos_bits - 1 - i)
            cnt = count_keys(lambda keys, pos: (keys == thr) & (pos < cand))
            return jnp.where(cnt <= n_tied_kept, cand, x)

        x = lax.fori_loop(0, pos_bits, pos_body, jnp.zeros((1, Q_BLOCK), I32))
        lim_ref[...] = jnp.where(tied, x, INT_MAX)

    pos_lim = lim_ref[...]

    def selected(keys, pos):
        return (keys > thr) | ((keys == thr) & (pos < pos_lim))

    m_ref[...] = jnp.full(m_ref.shape, NEG, F32)
    acc_ref[...] = jnp.zeros(acc_ref.shape, F32)
    qa_ref[:, :HEAD_DIM] = q_ref[...].reshape(N_HEADS * Q_BLOCK, HEAD_DIM)
    qa_ref[:, HEAD_DIM:] = eye_ref[...]

    def softmax_step(h, t):
        hs = slice(h * Q_BLOCK, (h + 1) * Q_BLOCK)
        m_old = m_ref[:, hs]
        m_new = jnp.maximum(m_old, jnp.max(t, axis=0, keepdims=True))
        m_ref[:, hs] = m_new
        return jnp.exp2(t - m_new).astype(BF16), jnp.exp2(m_old - m_new)

    def attend(keys, vt, logits_of):
        groups = [slice(g * HEAD_GROUP * Q_BLOCK, (g + 1) * HEAD_GROUP * Q_BLOCK)
                  for g in range(N_HEADS // HEAD_GROUP)]
        n_keys = keys.shape[0]
        for g, gs in enumerate(groups):
            s_ref[g, 0:n_keys, :] = lax.dot_general(keys, qa_ref[gs, 0:keys.shape[1]], nt,
                                                    preferred_element_type=F32)
        for g, gs in enumerate(groups):
            pa = [softmax_step(g * HEAD_GROUP + e,
                               logits_of(g * HEAD_GROUP + e,
                                         s_ref[g, 0:n_keys, e * Q_BLOCK:(e + 1) * Q_BLOCK]))
                  for e in range(HEAD_GROUP)]
            p = jnp.concatenate([x[0] for x in pa], axis=1)
            alpha = jnp.concatenate([x[1] for x in pa], axis=1)
            pv = jnp.dot(vt, p, preferred_element_type=F32)
            acc_ref[:, gs] = acc_ref[:, gs] * alpha + pv

    far_end = (j - 1) * Q_BLOCK
    far_row = lax.broadcasted_iota(I32, (FAR_CHUNK, Q_BLOCK), 0)

    def far_body(c, carry):
        lo = c * FAR_CHUNK
        start = pl.multiple_of(jnp.minimum(lo, jnp.maximum(far_end - FAR_CHUNK, 0)), LANES)
        pos = start + far_row
        sel = selected(key_ref[pl.ds(start, FAR_CHUNK), :], pos) & (pos >= lo) & (pos < far_end)
        maskb = jnp.where(sel, 0.0, NEG).astype(BF16)
        kaug = jnp.concatenate([k_ref[pl.ds(start, FAR_CHUNK), :], maskb], axis=1)
        vt = vt_ref[:, pl.ds(start, FAR_CHUNK)]
        attend(kaug, vt, lambda h, s: s)
        return carry

    lax.fori_loop(0, (jnp.maximum(far_end, 0) + FAR_CHUNK - 1) // FAR_CHUNK, far_body, 0)
    m_ref[...] = m_ref[...] + cvec_ref[...]

    near_start = pl.multiple_of(jnp.maximum(j - 1, 0) * Q_BLOCK, LANES)
    first = jnp.where(j == 0, 1, 0)
    near_pos = near_start + lax.broadcasted_iota(I32, (NEAR_KEYS, Q_BLOCK), 0)
    sel = selected(key_ref[pl.ds(near_start, NEAR_KEYS), :], near_pos)
    maskb = jnp.where(sel, 0.0, NEG)
    k_near = k_ref[pl.ds(near_start, NEAR_KEYS), :]
    vt_near = vt_ref[:, pl.ds(near_start, NEAR_KEYS)]

    def near_logits(h, s):
        return s + near_ref[first, :, h * Q_BLOCK:(h + 1) * Q_BLOCK] + maskb

    attend(k_near, vt_near, near_logits)

    inv_l = 1.0 / acc_ref[HEAD_DIM:HEAD_DIM + 1, :]
    for h in range(N_HEADS):
        hs = slice(h * Q_BLOCK, (h + 1) * Q_BLOCK)
        o_ref[:, h * HEAD_DIM:(h + 1) * HEAD_DIM] = (
            acc_ref[0:HEAD_DIM, hs] * inv_l[:, hs]).T.astype(o_ref.dtype)


def _t5_bucket_static(n):
    max_exact = N_BUCKETS // 2
    nf = np.maximum(n, 1).astype(np.float32)
    large = max_exact + (np.log(nf / np.float32(max_exact)) / np.float32(math.log(MAX_DISTANCE / max_exact))
                         * np.float32(N_BUCKETS - max_exact)).astype(np.int32)
    large = np.minimum(large, N_BUCKETS - 1)
    return np.where(n < max_exact, n, large)


def _bias_tables(rel_bias, seq):
    buckets = _t5_bucket_static(np.arange(seq, dtype=np.int32))
    far = buckets[Q_BLOCK + 1:]
    assert (far == far[0]).all()
    bias2 = rel_bias.astype(F32) * LOG2E
    span, origin = 4 * NEAR_KEYS, 2 * NEAR_KEYS
    d = np.arange(span) - origin
    r = jnp.where(jnp.asarray(d >= 0)[None, :], bias2[buckets[np.clip(d, 0, seq - 1)]].T, NEG)
    y = jnp.tile(r, (1, NEAR_KEYS))[:, :NEAR_KEYS * (span - 1)].reshape(N_HEADS, NEAR_KEYS, span - 1)

    def table(off):
        t = y[:, :, origin + off:origin + off + Q_BLOCK]
        return jnp.transpose(t, (1, 0, 2)).reshape(NEAR_KEYS, N_HEADS * Q_BLOCK)

    near = jnp.stack([table(Q_BLOCK), table(0)])
    cvec = jnp.repeat(bias2[int(far[0])], Q_BLOCK).reshape(1, N_HEADS * Q_BLOCK)
    return near, cvec


def _attention(qi_hm, kidx, w_t, q_hm, k, v_t, near, cvec, batch, seq):
    m = batch * seq
    n_blk = seq // Q_BLOCK
    k_sel = min(TOPK_MAX, seq // 4)
    hq = N_HEADS * Q_BLOCK
    kern = functools.partial(
        _attn_kernel, k_sel=k_sel, idx_scale=IDX_DIM ** -0.5 * N_IDX_HEADS ** -0.5,
        pos_bits=seq.bit_length())
    blk = lambda b, j: (b * n_blk + j)
    v_t = jnp.concatenate([v_t, jnp.ones((VT_ROWS - HEAD_DIM, m), v_t.dtype)], axis=0)
    assert seq % COUNT_CHUNK == 0 and seq % FAR_CHUNK == 0
    eye = jnp.tile(jnp.eye(Q_BLOCK, dtype=BF16), (N_HEADS, 1))
    return pl.pallas_call(
        kern,
        grid=(batch, n_blk),
        in_specs=[pl.BlockSpec((N_IDX_HEADS, Q_BLOCK, IDX_DIM), lambda b, j: (0, blk(b, j), 0)),
                  pl.BlockSpec((seq, IDX_DIM), lambda b, j: (b, 0)),
                  pl.BlockSpec((N_IDX_HEADS, Q_BLOCK), lambda b, j: (0, blk(b, j))),
                  pl.BlockSpec((N_HEADS, Q_BLOCK, HEAD_DIM), lambda b, j: (0, blk(b, j), 0)),
                  pl.BlockSpec((seq, HEAD_DIM), lambda b, j: (b, 0)),
                  pl.BlockSpec((VT_ROWS, seq), lambda b, j: (0, b)),
                  pl.BlockSpec((2, NEAR_KEYS, hq), lambda b, j: (0, 0, 0)),
                  pl.BlockSpec((1, hq), lambda b, j: (0, 0)),
                  pl.BlockSpec((hq, Q_BLOCK), lambda b, j: (0, 0))],
        out_specs=pl.BlockSpec((Q_BLOCK, N_HEADS * HEAD_DIM), lambda b, j: (blk(b, j), 0)),
        out_shape=jax.ShapeDtypeStruct((m, N_HEADS * HEAD_DIM), BF16),
        scratch_shapes=[pltpu.VMEM((seq, Q_BLOCK), I32),
                        pltpu.VMEM((1, hq), F32),
                        pltpu.VMEM((VT_ROWS, hq), F32),
                        pltpu.VMEM((hq, HEAD_DIM + Q_BLOCK), BF16),
                        pltpu.VMEM((N_HEADS // HEAD_GROUP, FAR_CHUNK, HEAD_GROUP * Q_BLOCK), F32),
                        pltpu.VMEM((1, Q_BLOCK), I32),
                        pltpu.VMEM((seq, Q_BLOCK), BF16),
                        pltpu.VMEM((seq, Q_BLOCK), BF16),
                        pltpu.VMEM((seq, Q_BLOCK), BF16),
                        pltpu.VMEM((seq, Q_BLOCK), BF16)],
        compiler_params=_params(("parallel", "arbitrary")),
        name="dsa_attention",
    )(qi_hm, kidx, w_t, q_hm, k, v_t, near, cvec, eye)


def _sgu_kernel(uv_ref, g_ref, b_ref, ws_ref, bs_ref, o_ref):
    uv = jax.nn.gelu(uv_ref[...])
    u = uv[:, :GMLP_WIDTH]
    v = uv[:, GMLP_WIDTH:]
    mu = jnp.mean(v, axis=-1, keepdims=True)
    vc = v - mu
    var = jnp.mean(vc * vc, axis=-1, keepdims=True)
    vn = (vc * lax.rsqrt(var + EPS) * g_ref[...] + b_ref[...]).astype(BF16)
    r = lax.broadcasted_iota(I32, (CHUNK, CHUNK), 0)
    c = lax.broadcasted_iota(I32, (CHUNK, CHUNK), 1)
    tril = c <= r
    bs = bs_ref[...]
    for g in range(N_GROUPS):
        ws = jnp.where(tril, ws_ref[g], 0.0).astype(BF16)
        mixed = jnp.dot(ws, vn[:, g * GROUP_DIM:(g + 1) * GROUP_DIM],
                        preferred_element_type=F32) + bs[:, g:g + 1]
        o_ref[:, g * GROUP_DIM:(g + 1) * GROUP_DIM] = (
            u[:, g * GROUP_DIM:(g + 1) * GROUP_DIM] * mixed).astype(o_ref.dtype)


def _sgu(uv, ln_g, ln_b, w_s, b_s):
    m = uv.shape[0]
    bs_t = jnp.zeros((CHUNK, LANES), F32).at[:, :N_GROUPS].set(jnp.transpose(b_s))
    return pl.pallas_call(
        _sgu_kernel,
        grid=(m // CHUNK,),
        in_specs=[pl.BlockSpec((CHUNK, 2 * GMLP_WIDTH), lambda i: (i, 0)),
                  pl.BlockSpec((1, GMLP_WIDTH), lambda i: (0, 0)),
                  pl.BlockSpec((1, GMLP_WIDTH), lambda i: (0, 0)),
                  pl.BlockSpec((N_GROUPS, CHUNK, CHUNK), lambda i: (0, 0, 0)),
                  pl.BlockSpec((CHUNK, LANES), lambda i: (0, 0))],
        out_specs=pl.BlockSpec((CHUNK, GMLP_WIDTH), lambda i: (i, 0)),
        out_shape=jax.ShapeDtypeStruct((m, GMLP_WIDTH), BF16),
        compiler_params=_params(("parallel",)),
        name="chunked_sgu",
    )(uv, ln_g.reshape(1, -1), ln_b.reshape(1, -1), w_s, bs_t)


def _merge_kernel(h_ref, ya_ref, yb_ref, wga_ref, wgb_ref, wa_ref, wb_ref, o_ref):
    h = h_ref[...]
    ga = jax.nn.sigmoid(jnp.dot(h, wga_ref[...], preferred_element_type=F32))
    a = jnp.dot(ya_ref[...], wa_ref[...], preferred_element_type=F32)
    out = ga * a
    gb = jax.nn.sigmoid(jnp.dot(h, wgb_ref[...], preferred_element_type=F32))
    b = jnp.dot(yb_ref[...], wb_ref[...], preferred_element_type=F32)
    o_ref[...] = (out + gb * b).astype(o_ref.dtype)


def _merge(h, ya, yb, wga, wgb, wa, wb, tm=512, tn=512):
    m, d = h.shape
    n = wga.shape[1]
    ka, kb = ya.shape[1], yb.shape[1]
    row = lambda i, j: (i, 0)
    colw = lambda i, j: (0, j)
    return pl.pallas_call(
        _merge_kernel,
        grid=(m // tm, n // tn),
        in_specs=[pl.BlockSpec((tm, d), row), pl.BlockSpec((tm, ka), row), pl.BlockSpec((tm, kb), row),
                  pl.BlockSpec((d, tn), colw), pl.BlockSpec((d, tn), colw),
                  pl.BlockSpec((ka, tn), colw), pl.BlockSpec((kb, tn), colw)],
        out_specs=pl.BlockSpec((tm, tn), lambda i, j: (i, j)),
        out_shape=jax.ShapeDtypeStruct((m, n), BF16),
        compiler_params=_params(("parallel", "arbitrary")),
        name="branch_merge",
    )(h, ya, yb, wga, wgb, wa, wb)


def _ple_kernel(p_ref, w_ref, g_ref, o_ref):
    acc = jnp.dot(p_ref[...].astype(BF16), w_ref[...], preferred_element_type=F32)
    ms = jnp.mean(acc * acc, axis=-1, keepdims=True)
    o_ref[...] = (acc * lax.rsqrt(ms + EPS) * g_ref[...]).astype(o_ref.dtype)


def _ple(p, w, g, tm=256):
    m, dp = p.shape
    d = w.shape[1]
    return pl.pallas_call(
        _ple_kernel,
        grid=(m // tm,),
        in_specs=[pl.BlockSpec((tm, dp), lambda i: (i, 0)),
                  pl.BlockSpec((dp, d), lambda i: (0, 0)),
                  pl.BlockSpec((1, d), lambda i: (0, 0))],
        out_specs=pl.BlockSpec((tm, d), lambda i: (i, 0)),
        out_shape=jax.ShapeDtypeStruct((m, d), F32),
        compiler_params=_params(("parallel",)),
        name="ple_embed",
    )(p, w, g.reshape(1, d))


def kernel(x, p, w_in, q_norm_g, k_norm_g, rel_bias, sgu_ln_g, sgu_ln_b, sgu_w, sgu_b, w_branch_a, w_branch_b, w_out, norm_mix_g, norm_ffn_g, w_gate_ffn, w_up_ffn, w_down_ffn, w_ple, ple_norm_g, w_ple_gate, norm_ple_g):
    batch, seq, d_model = x.shape
    depth = w_in.shape[0]
    m = batch * seq
    a_width = N_HEADS * HEAD_DIM
    sizes = (a_width, HEAD_DIM, HEAD_DIM, N_IDX_HEADS * IDX_DIM, IDX_DIM, N_IDX_HEADS,
             2 * GMLP_WIDTH, d_model, d_model)
    offs = np.concatenate([[0], np.cumsum(sizes)])
    near, cvec = _bias_tables(rel_bias, seq)

    xf = x.reshape(m, d_model)
    for i in range(depth):
        seg = lambda s: w_in[i][:, int(offs[s]):int(offs[s + 1])].astype(BF16)
        w_q, w_k, w_v, w_qi, w_ki, w_wi, w_uv, w_ga, w_gb = [seg(s) for s in range(9)]
        w_small = jnp.concatenate(
            [w_k, w_v, w_ki, jnp.pad(w_wi, ((0, 0), (0, LANES - N_IDX_HEADS)))], axis=1)

        h = _rmsnorm(xf, norm_mix_g[i])
        q_hm = _headproj(h, w_q, q_norm_g[i], "q_proj", post_scale=HEAD_DIM ** -0.5 * LOG2E)
        qi_hm = _headproj(h, w_qi, None, "q_idx_proj")
        k_, v_, kidx, widx = _small_proj(h, w_small, k_norm_g[i])
        uv = _matmul(h, [w_uv], [], lambda a, e: a[0], F32, 1024, 1024, "uv_proj")
        y_a = _attention(qi_hm, kidx, jnp.transpose(widx[:, :N_IDX_HEADS]), q_hm, k_,
                         jnp.transpose(v_), near, cvec, batch, seq)
        y_b = _sgu(uv, sgu_ln_g[i], sgu_ln_b[i], sgu_w[i], sgu_b[i])
        merged = _merge(h, y_a, y_b, w_ga, w_gb,
                        w_branch_a[i].astype(BF16), w_branch_b[i].astype(BF16))
        x1 = _matmul(merged, [w_out[i].astype(BF16)], [xf],
                     lambda a, e: e[0] + a[0], F32, 1024, 512, "out_proj")

        h2 = _rmsnorm(x1, norm_ffn_g[i])
        t = _matmul(h2, [w_gate_ffn[i].astype(BF16), w_up_ffn[i].astype(BF16)], [],
                    lambda a, e: jax.nn.silu(a[0]) * a[1], BF16, 1024, 256, "ffn_gate_up")
        d_ff = t.shape[1]
        x2, x2g, ssq2 = _res_matmul(t, w_down_ffn[i].astype(BF16), x1, norm_ple_g[i],
                                    512, 512, d_ff, "ffn_down")

        pe = _ple(p[i].reshape(m, -1), w_ple[i].astype(BF16), ple_norm_g[i])
        xf = _matmul(x2g, [w_ple_gate[i].astype(BF16)], [x2, pe],
                     lambda a, e: e[0] + jax.nn.sigmoid(a[0]) * e[1], F32, 1024, 512, "ple_gate",
                     row_ssq=ssq2)
    return xf.reshape(batch, seq, d_model)
```

```python
import functools
import math

import numpy as np
import jax
import jax.numpy as jnp
from jax import lax
from jax.experimental import pallas as pl
from jax.experimental.pallas import tpu as pltpu

F32 = jnp.float32
BF16 = jnp.bfloat16
I32 = jnp.int32

N_HEADS = 16
HEAD_DIM = 128
N_IDX_HEADS = 32
IDX_DIM = 128
TOPK_MAX = 256
Q_BLOCK = 128
N_BUCKETS = 32
MAX_DISTANCE = 128
GMLP_WIDTH = 2048
N_GROUPS = 8
GROUP_DIM = GMLP_WIDTH // N_GROUPS
CHUNK = 128
EPS = 1e-6

LANES = 128
SUBLANES = 8
VMEM_LIMIT = 56 * 1024 * 1024
NEG = -(2.0 ** 100)
INT_MIN = -2 ** 31
INT_MAX = 2 ** 31 - 1
KEY_NEG_INF = -2139095041
LOG2E = math.log2(math.e)

SCORE_CHUNK = 2 * LANES
COUNT_CHUNK = 2 * SCORE_CHUNK
FAR_CHUNK = 4 * LANES
NEAR_KEYS = 2 * Q_BLOCK
HEAD_GROUP = 4
VT_ROWS = HEAD_DIM + 16
MM_SUB_ROWS = 256


def _params(sem):
    return pltpu.CompilerParams(dimension_semantics=sem, vmem_limit_bytes=VMEM_LIMIT)


def _rmsnorm_kernel(x_ref, g_ref, o_ref):
    x = x_ref[...]
    ms = jnp.mean(x * x, axis=-1, keepdims=True)
    o_ref[...] = (x * lax.rsqrt(ms + EPS) * g_ref[...]).astype(o_ref.dtype)


def _rmsnorm(x, g, tm=256):
    m, d = x.shape
    return pl.pallas_call(
        _rmsnorm_kernel,
        grid=(m // tm,),
        in_specs=[pl.BlockSpec((tm, d), lambda i: (i, 0)),
                  pl.BlockSpec((1, d), lambda i: (0, 0))],
        out_specs=pl.BlockSpec((tm, d), lambda i: (i, 0)),
        out_shape=jax.ShapeDtypeStruct((m, d), BF16),
        compiler_params=_params(("parallel",)),
        name="rmsnorm",
    )(x, g.reshape(1, d))


def _lane_tiles(x):
    return [x[:, c * LANES:(c + 1) * LANES] for c in range(x.shape[1] // LANES)]


def _mm_kernel(*refs, n_w, n_x, epi, norm_dim):
    a_ref = refs[0]
    w_refs = refs[1:1 + n_w]
    x_refs = refs[1 + n_w:1 + n_w + n_x]
    if norm_dim:
        ssq_ref, o_ref, r_ref = refs[1 + n_w + n_x:]

        @pl.when(pl.program_id(1) == 0)
        def _():
            ms = jnp.sum(ssq_ref[...], axis=-1, keepdims=True) * (1.0 / norm_dim)
            r_ref[...] = jnp.broadcast_to(lax.rsqrt(ms + EPS), r_ref.shape)
    else:
        o_ref = refs[1 + n_w + n_x]
    for s in range(a_ref.shape[0] // MM_SUB_ROWS):
        rows = slice(s * MM_SUB_ROWS, (s + 1) * MM_SUB_ROWS)
        a = a_ref[rows, :]
        accs = [jnp.dot(a, w[...], preferred_element_type=F32) for w in w_refs]
        if norm_dim:
            accs = [acc * r_ref[rows, :] for acc in accs]
        o_ref[rows, :] = epi(accs, [x[rows, :] for x in x_refs]).astype(o_ref.dtype)


def _matmul(a, ws, extras, epi, out_dtype, tm, tn, name, row_ssq=None):
    m, k = a.shape
    n = ws[0].shape[1]
    normed = row_ssq is not None
    kern = functools.partial(_mm_kernel, n_w=len(ws), n_x=len(extras), epi=epi,
                             norm_dim=k if normed else 0)
    return pl.pallas_call(
        kern,
        grid=(m // tm, n // tn),
        in_specs=([pl.BlockSpec((tm, k), lambda i, j: (i, 0))]
                  + [pl.BlockSpec((k, tn), lambda i, j: (0, j)) for _ in ws]
                  + [pl.BlockSpec((tm, tn), lambda i, j: (i, j)) for _ in extras]
                  + ([pl.BlockSpec((tm, row_ssq.shape[1]), lambda i, j: (i, 0))] if normed else [])),
        out_specs=pl.BlockSpec((tm, tn), lambda i, j: (i, j)),
        out_shape=jax.ShapeDtypeStruct((m, n), out_dtype),
        scratch_shapes=[pltpu.VMEM((tm, tn), F32)] if normed else [],
        compiler_params=_params(("parallel", "arbitrary")),
        name=name,
    )(a, *ws, *extras, *([row_ssq] if normed else []))


def _res_mm_kernel(a_ref, w_ref, r_ref, g_ref, o_ref, og_ref, ssq_ref, *, nk):
    def sub_blocks(base_ref, last):
        for s in range(a_ref.shape[0] // MM_SUB_ROWS):
            rows = slice(s * MM_SUB_ROWS, (s + 1) * MM_SUB_ROWS)
            x = base_ref[rows, :] + jnp.dot(a_ref[rows, :], w_ref[...],
                                            preferred_element_type=F32)
            o_ref[rows, :] = x
            if last:
                og_ref[rows, :] = (x * g_ref[...]).astype(og_ref.dtype)
                ssq_ref[rows, :] = functools.reduce(lambda u, v: u + v, _lane_tiles(x * x))

    if nk == 1:
        sub_blocks(r_ref, True)
    else:
        @pl.when(pl.program_id(2) == 0)
        def _():
            sub_blocks(r_ref, False)

        @pl.when(pl.program_id(2) == 1)
        def _():
            sub_blocks(o_ref, True)


def _res_matmul(a, w, res, g_next, tm, tn, tk, name):
    m, k = a.shape
    n = w.shape[1]
    nk = k // tk
    assert nk in (1, 2) and nk * tk == k
    return pl.pallas_call(
        functools.partial(_res_mm_kernel, nk=nk),
        grid=(m // tm, n // tn, nk),
        in_specs=[pl.BlockSpec((tm, tk), lambda i, j, kk: (i, kk)),
                  pl.BlockSpec((tk, tn), lambda i, j, kk: (kk, j)),
                  pl.BlockSpec((tm, tn), lambda i, j, kk: (i, j)),
                  pl.BlockSpec((1, tn), lambda i, j, kk: (0, j))],
        out_specs=[pl.BlockSpec((tm, tn), lambda i, j, kk: (i, j)),
                   pl.BlockSpec((tm, tn), lambda i, j, kk: (i, j)),
                   pl.BlockSpec((tm, LANES), lambda i, j, kk: (i, j))],
        out_shape=[jax.ShapeDtypeStruct((m, n), F32),
                   jax.ShapeDtypeStruct((m, n), BF16),
                   jax.ShapeDtypeStruct((m, (n // tn) * LANES), F32)],
        compiler_params=_params(("parallel", "arbitrary", "arbitrary")),
        name=name,
    )(a, w, res, g_next.reshape(1, n))


def _headproj_kernel(*refs, heads, norm, post_scale):
    a_ref, w_ref = refs[0], refs[1]
    o_ref = refs[-1]
    acc = jnp.dot(a_ref[...], w_ref[...], preferred_element_type=F32)
    for hh in range(heads):
        blk = acc[:, hh * LANES:(hh + 1) * LANES]
        if norm:
            ms = jnp.mean(blk * blk, axis=-1, keepdims=True)
            blk = blk * lax.rsqrt(ms + EPS) * refs[2][...]
        if post_scale is not None:
            blk = blk * post_scale
        o_ref[hh] = blk.astype(o_ref.dtype)


def _headproj(h, w, g, name, post_scale=None, tm=1024, tn=1024):
    m, k = h.shape
    n_heads = w.shape[1] // LANES
    heads = tn // LANES
    norm = g is not None
    extra_specs = [pl.BlockSpec((1, LANES), lambda i, j: (0, 0))] if norm else []
    extra_args = [g.reshape(1, LANES)] if norm else []
    return pl.pallas_call(
        functools.partial(_headproj_kernel, heads=heads, norm=norm, post_scale=post_scale),
        grid=(m // tm, w.shape[1] // tn),
        in_specs=[pl.BlockSpec((tm, k), lambda i, j: (i, 0)),
                  pl.BlockSpec((k, tn), lambda i, j: (0, j))] + extra_specs,
        out_specs=pl.BlockSpec((heads, tm, LANES), lambda i, j: (j, i, 0)),
        out_shape=jax.ShapeDtypeStruct((n_heads, m, LANES), BF16),
        compiler_params=_params(("parallel", "arbitrary")),
        name=name,
    )(h, w, *extra_args)


def _small_kernel(a_ref, w_ref, g_ref, k_ref, v_ref, ki_ref, wi_ref):
    acc = jnp.dot(a_ref[...], w_ref[...], preferred_element_type=F32)
    kk = acc[:, 0:LANES]
    ms = jnp.mean(kk * kk, axis=-1, keepdims=True)
    k_ref[...] = (kk * lax.rsqrt(ms + EPS) * g_ref[...]).astype(k_ref.dtype)
    v_ref[...] = acc[:, LANES:2 * LANES].astype(v_ref.dtype)
    ki_ref[...] = acc[:, 2 * LANES:3 * LANES].astype(ki_ref.dtype)
    wi_ref[...] = acc[:, 3 * LANES:4 * LANES]


def _small_proj(h, w_small, g, tm=1024):
    m, k = h.shape
    row = lambda i: (i, 0)
    return pl.pallas_call(
        _small_kernel,
        grid=(m // tm,),
        in_specs=[pl.BlockSpec((tm, k), row),
                  pl.BlockSpec((k, 4 * LANES), lambda i: (0, 0)),
                  pl.BlockSpec((1, LANES), lambda i: (0, 0))],
        out_specs=[pl.BlockSpec((tm, LANES), row)] * 4,
        out_shape=[jax.ShapeDtypeStruct((m, LANES), BF16)] * 3
                  + [jax.ShapeDtypeStruct((m, LANES), F32)],
        compiler_params=_params(("parallel",)),
        name="kv_idx_proj",
    )(h, w_small, g.reshape(1, LANES))


def _attn_kernel(qi_ref, kidx_ref, wt_ref, q_ref, k_ref, vt_ref, near_ref, cvec_ref, eye_ref, o_ref,
                 key_ref, m_ref, acc_ref, qa_ref, s_ref, lim_ref,
                 *, k_sel, idx_scale, pos_bits):
    j = pl.program_id(1)
    nt = (((1,), (1,)), ((), ()))
    n_chunks = (j + 2) // 2

    key_row = lax.broadcasted_iota(I32, (SCORE_CHUNK, Q_BLOCK), 0)
    q_col = lax.broadcasted_iota(I32, (SCORE_CHUNK, Q_BLOCK), 1)

    def score_chunk(c, carry):
        start = pl.multiple_of(c * SCORE_CHUNK, SCORE_CHUNK)
        kc = kidx_ref[pl.ds(start, SCORE_CHUNK), :]
        acc = jnp.zeros((SCORE_CHUNK, Q_BLOCK), F32)
        for hp in range(N_IDX_HEADS // 2):
            qpair = qi_ref[2 * hp:2 * hp + 2].reshape(2 * Q_BLOCK, IDX_DIM)
            d = lax.dot_general(kc, qpair, nt, preferred_element_type=F32)
            acc = acc + jnp.maximum(d[:, :Q_BLOCK], 0.0) * wt_ref[2 * hp:2 * hp + 1, :]
            acc = acc + jnp.maximum(d[:, Q_BLOCK:], 0.0) * wt_ref[2 * hp + 1:2 * hp + 2, :]
        s = acc * idx_scale
        causal = (start + key_row) <= (j * Q_BLOCK + q_col)
        s = jnp.where(causal, s, -jnp.inf)
        bits = lax.bitcast_convert_type(s, I32)
        key_ref[pl.ds(start, SCORE_CHUNK), :] = bits ^ ((bits >> 31) & 0x7FFFFFFF)
        return carry

    lax.fori_loop(0, n_chunks, score_chunk, 0)

    @pl.when(n_chunks % 2 == 1)
    def _():
        key_ref[pl.ds(pl.multiple_of(n_chunks * SCORE_CHUNK, SCORE_CHUNK), SCORE_CHUNK), :] = (
            jnp.full((SCORE_CHUNK, Q_BLOCK), KEY_NEG_INF, I32))

    n_count = (n_chunks + 1) // 2
    count_row = lax.broadcasted_iota(I32, (COUNT_CHUNK, Q_BLOCK), 0)

    def count_keys(pred):
        def body(c, a):
            start = pl.multiple_of(c * COUNT_CHUNK, COUNT_CHUNK)
            hit = jnp.where(pred(key_ref[pl.ds(start, COUNT_CHUNK), :], start + count_row), 1, 0)
            return a + jnp.sum(hit.reshape(COUNT_CHUNK // SUBLANES, SUBLANES, Q_BLOCK), axis=0)

        a = lax.fori_loop(0, n_count, body, jnp.zeros((SUBLANES, Q_BLOCK), I32))
        return jnp.sum(a, axis=0, keepdims=True)

    def bit_body(i, carry):
        tu, cnt_tu = carry
        cu = tu | jnp.left_shift(jnp.int32(1), 31 - i)
        cs = cu ^ INT_MIN
        cnt = count_keys(lambda keys, pos: keys >= cs)
        take = cnt >= k_sel
        return jnp.where(take, cu, tu), jnp.where(take, cnt, cnt_tu)

    tu, cnt_ge = lax.fori_loop(
        0, 32, bit_body,
        (jnp.zeros((1, Q_BLOCK), I32), jnp.full((1, Q_BLOCK), n_count * COUNT_CHUNK, I32)))
    thr = tu ^ INT_MIN

    tied = (cnt_ge > k_sel) & (thr > KEY_NEG_INF)
    lim_ref[...] = jnp.full(lim_ref.shape, INT_MAX, I32)

    @pl.when(jnp.max(jnp.where(tied, 1, 0)) > 0)
    def _():
        n_tied_kept = k_sel - count_keys(lambda keys, pos: keys > thr)

        def pos_body(i, x):
            cand = x | jnp.left_shift(jnp.int32(1), pos_bits - 1 - i)
            cnt = count_keys(lambda keys, pos: (keys == thr) & (pos < cand))
            return jnp.where(cnt <= n_tied_kept, cand, x)

        x = lax.fori_loop(0, pos_bits, pos_body, jnp.zeros((1, Q_BLOCK), I32))
        lim_ref[...] = jnp.where(tied, x, INT_MAX)

    pos_lim = lim_ref[...]

    def selected(keys, pos):
        return (keys > thr) | ((keys == thr) & (pos < pos_lim))

    m_ref[...] = jnp.full(m_ref.shape, NEG, F32)
    acc_ref[...] = jnp.zeros(acc_ref.shape, F32)
    qa_ref[:, :HEAD_DIM] = q_ref[...].reshape(N_HEADS * Q_BLOCK, HEAD_DIM)
    qa_ref[:, HEAD_DIM:] = eye_ref[...]

    def softmax_step(h, t):
        hs = slice(h * Q_BLOCK, (h + 1) * Q_BLOCK)
        m_old = m_ref[:, hs]
        m_new = jnp.maximum(m_old, jnp.max(t, axis=0, keepdims=True))
        m_ref[:, hs] = m_new
        return jnp.exp2(t - m_new).astype(BF16), jnp.exp2(m_old - m_new)

    def attend(keys, vt, logits_of):
        groups = [slice(g * HEAD_GROUP * Q_BLOCK, (g + 1) * HEAD_GROUP * Q_BLOCK)
                  for g in range(N_HEADS // HEAD_GROUP)]
        n_keys = keys.shape[0]
        for g, gs in enumerate(groups):
            s_ref[g, 0:n_keys, :] = lax.dot_general(keys, qa_ref[gs, 0:keys.shape[1]], nt,
                                                    preferred_element_type=F32)
        for g, gs in enumerate(groups):
            pa = [softmax_step(g * HEAD_GROUP + e,
                               logits_of(g * HEAD_GROUP + e,
                                         s_ref[g, 0:n_keys, e * Q_BLOCK:(e + 1) * Q_BLOCK]))
                  for e in range(HEAD_GROUP)]
            p = jnp.concatenate([x[0] for x in pa], axis=1)
            alpha = jnp.concatenate([x[1] for x in pa], axis=1)
            pv = jnp.dot(vt, p, preferred_element_type=F32)
            acc_ref[:, gs] = acc_ref[:, gs] * alpha + pv

    far_end = (j - 1) * Q_BLOCK
    far_row = lax.broadcasted_iota(I32, (FAR_CHUNK, Q_BLOCK), 0)

    def far_body(c, carry):
        lo = c * FAR_CHUNK
        start = pl.multiple_of(jnp.minimum(lo, jnp.maximum(far_end - FAR_CHUNK, 0)), LANES)
        pos = start + far_row
        sel = selected(key_ref[pl.ds(start, FAR_CHUNK), :], pos) & (pos >= lo) & (pos < far_end)
        maskb = jnp.where(sel, 0.0, NEG).astype(BF16)
        kaug = jnp.concatenate([k_ref[pl.ds(start, FAR_CHUNK), :], maskb], axis=1)
        vt = vt_ref[:, pl.ds(start, FAR_CHUNK)]
        attend(kaug, vt, lambda h, s: s)
        return carry

    lax.fori_loop(0, (jnp.maximum(far_end, 0) + FAR_CHUNK - 1) // FAR_CHUNK, far_body, 0)
    m_ref[...] = m_ref[...] + cvec_ref[...]

    near_start = pl.multiple_of(jnp.maximum(j - 1, 0) * Q_BLOCK, LANES)
    first = jnp.where(j == 0, 1, 0)
    near_pos = near_start + lax.broadcasted_iota(I32, (NEAR_KEYS, Q_BLOCK), 0)
    sel = selected(key_ref[pl.ds(near_start, NEAR_KEYS), :], near_pos)
    maskb = jnp.where(sel, 0.0, NEG)
    k_near = k_ref[pl.ds(near_start, NEAR_KEYS), :]
    vt_near = vt_ref[:, pl.ds(near_start, NEAR_KEYS)]

    def near_logits(h, s):
        return s + near_ref[first, :, h * Q_BLOCK:(h + 1) * Q_BLOCK] + maskb

    attend(k_near, vt_near, near_logits)

    inv_l = 1.0 / acc_ref[HEAD_DIM:HEAD_DIM + 1, :]
    for h in range(N_HEADS):
        hs = slice(h * Q_BLOCK, (h + 1) * Q_BLOCK)
        o_ref[:, h * HEAD_DIM:(h + 1) * HEAD_DIM] = (
            acc_ref[0:HEAD_DIM, hs] * inv_l[:, hs]).T.astype(o_ref.dtype)


def _t5_bucket_static(n):
    max_exact = N_BUCKETS // 2
    nf = np.maximum(n, 1).astype(np.float32)
    large = max_exact + (np.log(nf / np.float32(max_exact)) / np.float32(math.log(MAX_DISTANCE / max_exact))
                         * np.float32(N_BUCKETS - max_exact)).astype(np.int32)
    large = np.minimum(large, N_BUCKETS - 1)
    return np.where(n < max_exact, n, large)


def _bias_tables(rel_bias, seq):
    buckets = _t5_bucket_static(np.arange(seq, dtype=np.int32))
    far = buckets[Q_BLOCK + 1:]
    assert (far == far[0]).all()
    bias2 = rel_bias.astype(F32) * LOG2E
    span, origin = 4 * NEAR_KEYS, 2 * NEAR_KEYS
    d = np.arange(span) - origin
    r = jnp.where(jnp.asarray(d >= 0)[None, :], bias2[buckets[np.clip(d, 0, seq - 1)]].T, NEG)
    y = jnp.tile(r, (1, NEAR_KEYS))[:, :NEAR_KEYS * (span - 1)].reshape(N_HEADS, NEAR_KEYS, span - 1)

    def table(off):
        t = y[:, :, origin + off:origin + off + Q_BLOCK]
        return jnp.transpose(t, (1, 0, 2)).reshape(NEAR_KEYS, N_HEADS * Q_BLOCK)

    near = jnp.stack([table(Q_BLOCK), table(0)])
    cvec = jnp.repeat(bias2[int(far[0])], Q_BLOCK).reshape(1, N_HEADS * Q_BLOCK)
    return near, cvec


def _attention(qi_hm, kidx, w_t, q_hm, k, v_t, near, cvec, batch, seq):
    m = batch * seq
    n_blk = seq // Q_BLOCK
    k_sel = min(TOPK_MAX, seq // 4)
    hq = N_HEADS * Q_BLOCK
    kern = functools.partial(
        _attn_kernel, k_sel=k_sel, idx_scale=IDX_DIM ** -0.5 * N_IDX_HEADS ** -0.5,
        pos_bits=seq.bit_length())
    blk = lambda b, j: (b * n_blk + j)
    v_t = jnp.concatenate([v_t, jnp.ones((VT_ROWS - HEAD_DIM, m), v_t.dtype)], axis=0)
    assert seq % COUNT_CHUNK == 0 and seq % FAR_CHUNK == 0
    eye = jnp.tile(jnp.eye(Q_BLOCK, dtype=BF16), (N_HEADS, 1))
    return pl.pallas_call(
        kern,
        grid=(batch, n_blk),
        in_specs=[pl.BlockSpec((N_IDX_HEADS, Q_BLOCK, IDX_DIM), lambda b, j: (0, blk(b, j), 0)),
                  pl.BlockSpec((seq, IDX_DIM), lambda b, j: (b, 0)),
                  pl.BlockSpec((N_IDX_HEADS, Q_BLOCK), lambda b, j: (0, blk(b, j))),
                  pl.BlockSpec((N_HEADS, Q_BLOCK, HEAD_DIM), lambda b, j: (0, blk(b, j), 0)),
                  pl.BlockSpec((seq, HEAD_DIM), lambda b, j: (b, 0)),
                  pl.BlockSpec((VT_ROWS, seq), lambda b, j: (0, b)),
                  pl.BlockSpec((2, NEAR_KEYS, hq), lambda b, j: (0, 0, 0)),
                  pl.BlockSpec((1, hq), lambda b, j: (0, 0)),
                  pl.BlockSpec((hq, Q_BLOCK), lambda b, j: (0, 0))],
        out_specs=pl.BlockSpec((Q_BLOCK, N_HEADS * HEAD_DIM), lambda b, j: (blk(b, j), 0)),
        out_shape=jax.ShapeDtypeStruct((m, N_HEADS * HEAD_DIM), BF16),
        scratch_shapes=[pltpu.VMEM((seq, Q_BLOCK), I32),
                        pltpu.VMEM((1, hq), F32),
                        pltpu.VMEM((VT_ROWS, hq), F32),
                        pltpu.VMEM((hq, HEAD_DIM + Q_BLOCK), BF16),
                        pltpu.VMEM((N_HEADS // HEAD_GROUP, FAR_CHUNK, HEAD_GROUP * Q_BLOCK), F32),
                        pltpu.VMEM((1, Q_BLOCK), I32)],
        compiler_params=_params(("parallel", "arbitrary")),
        name="dsa_attention",
    )(qi_hm, kidx, w_t, q_hm, k, v_t, near, cvec, eye)


def _sgu_kernel(uv_ref, g_ref, b_ref, ws_ref, bs_ref, o_ref):
    uv = jax.nn.gelu(uv_ref[...])
    u = uv[:, :GMLP_WIDTH]
    v = uv[:, GMLP_WIDTH:]
    mu = jnp.mean(v, axis=-1, keepdims=True)
    vc = v - mu
    var = jnp.mean(vc * vc, axis=-1, keepdims=True)
    vn = (vc * lax.rsqrt(var + EPS) * g_ref[...] + b_ref[...]).astype(BF16)
    r = lax.broadcasted_iota(I32, (CHUNK, CHUNK), 0)
    c = lax.broadcasted_iota(I32, (CHUNK, CHUNK), 1)
    tril = c <= r
    bs = bs_ref[...]
    for g in range(N_GROUPS):
        ws = jnp.where(tril, ws_ref[g], 0.0).astype(BF16)
        mixed = jnp.dot(ws, vn[:, g * GROUP_DIM:(g + 1) * GROUP_DIM],
                        preferred_element_type=F32) + bs[:, g:g + 1]
        o_ref[:, g * GROUP_DIM:(g + 1) * GROUP_DIM] = (
            u[:, g * GROUP_DIM:(g + 1) * GROUP_DIM] * mixed).astype(o_ref.dtype)


def _sgu(uv, ln_g, ln_b, w_s, b_s):
    m = uv.shape[0]
    bs_t = jnp.zeros((CHUNK, LANES), F32).at[:, :N_GROUPS].set(jnp.transpose(b_s))
    return pl.pallas_call(
        _sgu_kernel,
        grid=(m // CHUNK,),
        in_specs=[pl.BlockSpec((CHUNK, 2 * GMLP_WIDTH), lambda i: (i, 0)),
                  pl.BlockSpec((1, GMLP_WIDTH), lambda i: (0, 0)),
                  pl.BlockSpec((1, GMLP_WIDTH), lambda i: (0, 0)),
                  pl.BlockSpec((N_GROUPS, CHUNK, CHUNK), lambda i: (0, 0, 0)),
                  pl.BlockSpec((CHUNK, LANES), lambda i: (0, 0))],
        out_specs=pl.BlockSpec((CHUNK, GMLP_WIDTH), lambda i: (i, 0)),
        out_shape=jax.ShapeDtypeStruct((m, GMLP_WIDTH), BF16),
        compiler_params=_params(("parallel",)),
        name="chunked_sgu",
    )(uv, ln_g.reshape(1, -1), ln_b.reshape(1, -1), w_s, bs_t)


def _merge_kernel(h_ref, ya_ref, yb_ref, wga_ref, wgb_ref, wa_ref, wb_ref, o_ref):
    h = h_ref[...]
    ga = jax.nn.sigmoid(jnp.dot(h, wga_ref[...], preferred_element_type=F32))
    a = jnp.dot(ya_ref[...], wa_ref[...], preferred_element_type=F32)
    out = ga * a
    gb = jax.nn.sigmoid(jnp.dot(h, wgb_ref[...], preferred_element_type=F32))
    b = jnp.dot(yb_ref[...], wb_ref[...], preferred_element_type=F32)
    o_ref[...] = (out + gb * b).astype(o_ref.dtype)


def _merge(h, ya, yb, wga, wgb, wa, wb, tm=512, tn=512):
    m, d = h.shape
    n = wga.shape[1]
    ka, kb = ya.shape[1], yb.shape[1]
    row = lambda i, j: (i, 0)
    colw = lambda i, j: (0, j)
    return pl.pallas_call(
        _merge_kernel,
        grid=(m // tm, n // tn),
        in_specs=[pl.BlockSpec((tm, d), row), pl.BlockSpec((tm, ka), row), pl.BlockSpec((tm, kb), row),
                  pl.BlockSpec((d, tn), colw), pl.BlockSpec((d, tn), colw),
                  pl.BlockSpec((ka, tn), colw), pl.BlockSpec((kb, tn), colw)],
        out_specs=pl.BlockSpec((tm, tn), lambda i, j: (i, j)),
        out_shape=jax.ShapeDtypeStruct((m, n), BF16),
        compiler_params=_params(("parallel", "arbitrary")),
        name="branch_merge",
    )(h, ya, yb, wga, wgb, wa, wb)


def _ple_kernel(p_ref, w_ref, g_ref, o_ref):
    acc = jnp.dot(p_ref[...].astype(BF16), w_ref[...], preferred_element_type=F32)
    ms = jnp.mean(acc * acc, axis=-1, keepdims=True)
    o_ref[...] = (acc * lax.rsqrt(ms + EPS) * g_ref[...]).astype(o_ref.dtype)


def _ple(p, w, g, tm=256):
    m, dp = p.shape
    d = w.shape[1]
    return pl.pallas_call(
        _ple_kernel,
        grid=(m // tm,),
        in_specs=[pl.BlockSpec((tm, dp), lambda i: (i, 0)),
                  pl.BlockSpec((dp, d), lambda i: (0, 0)),
                  pl.BlockSpec((1, d), lambda i: (0, 0))],
        out_specs=pl.BlockSpec((tm, d), lambda i: (i, 0)),
        out_shape=jax.ShapeDtypeStruct((m, d), F32),
        compiler_params=_params(("parallel",)),
        name="ple_embed",
    )(p, w, g.reshape(1, d))


def kernel(x, p, w_in, q_norm_g, k_norm_g, rel_bias, sgu_ln_g, sgu_ln_b, sgu_w, sgu_b, w_branch_a, w_branch_b, w_out, norm_mix_g, norm_ffn_g, w_gate_ffn, w_up_ffn, w_down_ffn, w_ple, ple_norm_g, w_ple_gate, norm_ple_g):
    batch, seq, d_model = x.shape
    depth = w_in.shape[0]
    m = batch * seq
    a_width = N_HEADS * HEAD_DIM
    sizes = (a_width, HEAD_DIM, HEAD_DIM, N_IDX_HEADS * IDX_DIM, IDX_DIM, N_IDX_HEADS,
             2 * GMLP_WIDTH, d_model, d_model)
    offs = np.concatenate([[0], np.cumsum(sizes)])
    near, cvec = _bias_tables(rel_bias, seq)

    xf = x.reshape(m, d_model)
    for i in range(depth):
        seg = lambda s: w_in[i][:, int(offs[s]):int(offs[s + 1])].astype(BF16)
        w_q, w_k, w_v, w_qi, w_ki, w_wi, w_uv, w_ga, w_gb = [seg(s) for s in range(9)]
        w_small = jnp.concatenate(
            [w_k, w_v, w_ki, jnp.pad(w_wi, ((0, 0), (0, LANES - N_IDX_HEADS)))], axis=1)

        h = _rmsnorm(xf, norm_mix_g[i])
        q_hm = _headproj(h, w_q, q_norm_g[i], "q_proj", post_scale=HEAD_DIM ** -0.5 * LOG2E)
        qi_hm = _headproj(h, w_qi, None, "q_idx_proj")
        k_, v_, kidx, widx = _small_proj(h, w_small, k_norm_g[i])
        uv = _matmul(h, [w_uv], [], lambda a, e: a[0], F32, 1024, 1024, "uv_proj")
        y_a = _attention(qi_hm, kidx, jnp.transpose(widx[:, :N_IDX_HEADS]), q_hm, k_,
                         jnp.transpose(v_), near, cvec, batch, seq)
        y_b = _sgu(uv, sgu_ln_g[i], sgu_ln_b[i], sgu_w[i], sgu_b[i])
        merged = _merge(h, y_a, y_b, w_ga, w_gb,
                        w_branch_a[i].astype(BF16), w_branch_b[i].astype(BF16))
        x1 = _matmul(merged, [w_out[i].astype(BF16)], [xf],
                     lambda a, e: e[0] + a[0], F32, 1024, 512, "out_proj")

        h2 = _rmsnorm(x1, norm_ffn_g[i])
        t = _matmul(h2, [w_gate_ffn[i].astype(BF16), w_up_ffn[i].astype(BF16)], [],
                    lambda a, e: jax.nn.silu(a[0]) * a[1], BF16, 1024, 256, "ffn_gate_up")
        d_ff = t.shape[1]
        x2, x2g, ssq2 = _res_matmul(t, w_down_ffn[i].astype(BF16), x1, norm_ple_g[i],
                                    512, 512, d_ff, "ffn_down")

        pe = _ple(p[i].reshape(m, -1), w_ple[i].astype(BF16), ple_norm_g[i])
        xf = _matmul(x2g, [w_ple_gate[i].astype(BF16)], [x2, pe],
                     lambda a, e: e[0] + jax.nn.sigmoid(a[0]) * e[1], F32, 1024, 512, "ple_gate",
                     row_ssq=ssq2)
    return xf.reshape(batch, seq, d_model)
```

```python
import functools
import math

import numpy as np
import jax
import jax.numpy as jnp
from jax import lax
from jax.experimental import pallas as pl
from jax.experimental.pallas import tpu as pltpu

F32 = jnp.float32
BF16 = jnp.bfloat16
I32 = jnp.int32

N_HEADS = 16
HEAD_DIM = 128
N_IDX_HEADS = 32
IDX_DIM = 128
TOPK_MAX = 256
Q_BLOCK = 128
N_BUCKETS = 32
MAX_DISTANCE = 128
GMLP_WIDTH = 2048
N_GROUPS = 8
GROUP_DIM = GMLP_WIDTH // N_GROUPS
CHUNK = 128
EPS = 1e-6

LANES = 128
SUBLANES = 8
VMEM_LIMIT = 56 * 1024 * 1024
NEG = -(2.0 ** 100)
INT_MIN = -2 ** 31
INT_MAX = 2 ** 31 - 1
KEY_NEG_INF = -2139095041
LOG2E = math.log2(math.e)

SCORE_CHUNK = 4 * LANES
COUNT_CHUNK = SCORE_CHUNK
FAR_CHUNK = 4 * LANES
NEAR_KEYS = 2 * Q_BLOCK
HEAD_GROUP = 4
VT_ROWS = HEAD_DIM + 16
MM_SUB_ROWS = 256


def _params(sem):
    return pltpu.CompilerParams(dimension_semantics=sem, vmem_limit_bytes=VMEM_LIMIT)


def _rmsnorm_kernel(x_ref, g_ref, o_ref):
    x = x_ref[...]
    ms = jnp.mean(x * x, axis=-1, keepdims=True)
    o_ref[...] = (x * lax.rsqrt(ms + EPS) * g_ref[...]).astype(o_ref.dtype)


def _rmsnorm(x, g, tm=256):
    m, d = x.shape
    return pl.pallas_call(
        _rmsnorm_kernel,
        grid=(m // tm,),
        in_specs=[pl.BlockSpec((tm, d), lambda i: (i, 0)),
                  pl.BlockSpec((1, d), lambda i: (0, 0))],
        out_specs=pl.BlockSpec((tm, d), lambda i: (i, 0)),
        out_shape=jax.ShapeDtypeStruct((m, d), BF16),
        compiler_params=_params(("parallel",)),
        name="rmsnorm",
    )(x, g.reshape(1, d))


def _lane_tiles(x):
    return [x[:, c * LANES:(c + 1) * LANES] for c in range(x.shape[1] // LANES)]


def _mm_kernel(*refs, n_w, n_x, epi, norm_dim):
    a_ref = refs[0]
    w_refs = refs[1:1 + n_w]
    x_refs = refs[1 + n_w:1 + n_w + n_x]
    if norm_dim:
        ssq_ref, o_ref, r_ref = refs[1 + n_w + n_x:]

        @pl.when(pl.program_id(1) == 0)
        def _():
            ms = jnp.sum(ssq_ref[...], axis=-1, keepdims=True) * (1.0 / norm_dim)
            r_ref[...] = jnp.broadcast_to(lax.rsqrt(ms + EPS), r_ref.shape)
    else:
        o_ref = refs[1 + n_w + n_x]
    for s in range(a_ref.shape[0] // MM_SUB_ROWS):
        rows = slice(s * MM_SUB_ROWS, (s + 1) * MM_SUB_ROWS)
        a = a_ref[rows, :]
        accs = [jnp.dot(a, w[...], preferred_element_type=F32) for w in w_refs]
        if norm_dim:
            accs = [acc * r_ref[rows, :] for acc in accs]
        o_ref[rows, :] = epi(accs, [x[rows, :] for x in x_refs]).astype(o_ref.dtype)


def _matmul(a, ws, extras, epi, out_dtype, tm, tn, name, row_ssq=None):
    m, k = a.shape
    n = ws[0].shape[1]
    normed = row_ssq is not None
    kern = functools.partial(_mm_kernel, n_w=len(ws), n_x=len(extras), epi=epi,
                             norm_dim=k if normed else 0)
    return pl.pallas_call(
        kern,
        grid=(m // tm, n // tn),
        in_specs=([pl.BlockSpec((tm, k), lambda i, j: (i, 0))]
                  + [pl.BlockSpec((k, tn), lambda i, j: (0, j)) for _ in ws]
                  + [pl.BlockSpec((tm, tn), lambda i, j: (i, j)) for _ in extras]
                  + ([pl.BlockSpec((tm, row_ssq.shape[1]), lambda i, j: (i, 0))] if normed else [])),
        out_specs=pl.BlockSpec((tm, tn), lambda i, j: (i, j)),
        out_shape=jax.ShapeDtypeStruct((m, n), out_dtype),
        scratch_shapes=[pltpu.VMEM((tm, tn), F32)] if normed else [],
        compiler_params=_params(("parallel", "arbitrary")),
        name=name,
    )(a, *ws, *extras, *([row_ssq] if normed else []))


def _res_mm_kernel(a_ref, w_ref, r_ref, g_ref, o_ref, og_ref, ssq_ref, *, nk):
    def sub_blocks(base_ref, last):
        for s in range(a_ref.shape[0] // MM_SUB_ROWS):
            rows = slice(s * MM_SUB_ROWS, (s + 1) * MM_SUB_ROWS)
            x = base_ref[rows, :] + jnp.dot(a_ref[rows, :], w_ref[...],
                                            preferred_element_type=F32)
            o_ref[rows, :] = x
            if last:
                og_ref[rows, :] = (x * g_ref[...]).astype(og_ref.dtype)
                ssq_ref[rows, :] = functools.reduce(lambda u, v: u + v, _lane_tiles(x * x))

    if nk == 1:
        sub_blocks(r_ref, True)
    else:
        @pl.when(pl.program_id(2) == 0)
        def _():
            sub_blocks(r_ref, False)

        @pl.when(pl.program_id(2) == 1)
        def _():
            sub_blocks(o_ref, True)


def _res_matmul(a, w, res, g_next, tm, tn, tk, name):
    m, k = a.shape
    n = w.shape[1]
    nk = k // tk
    assert nk in (1, 2) and nk * tk == k
    return pl.pallas_call(
        functools.partial(_res_mm_kernel, nk=nk),
        grid=(m // tm, n // tn, nk),
        in_specs=[pl.BlockSpec((tm, tk), lambda i, j, kk: (i, kk)),
                  pl.BlockSpec((tk, tn), lambda i, j, kk: (kk, j)),
                  pl.BlockSpec((tm, tn), lambda i, j, kk: (i, j)),
                  pl.BlockSpec((1, tn), lambda i, j, kk: (0, j))],
        out_specs=[pl.BlockSpec((tm, tn), lambda i, j, kk: (i, j)),
                   pl.BlockSpec((tm, tn), lambda i, j, kk: (i, j)),
                   pl.BlockSpec((tm, LANES), lambda i, j, kk: (i, j))],
        out_shape=[jax.ShapeDtypeStruct((m, n), F32),
                   jax.ShapeDtypeStruct((m, n), BF16),
                   jax.ShapeDtypeStruct((m, (n // tn) * LANES), F32)],
        compiler_params=_params(("parallel", "arbitrary", "arbitrary")),
        name=name,
    )(a, w, res, g_next.reshape(1, n))


def _headproj_kernel(*refs, heads, norm, post_scale):
    a_ref, w_ref = refs[0], refs[1]
    o_ref = refs[-1]
    acc = jnp.dot(a_ref[...], w_ref[...], preferred_element_type=F32)
    for hh in range(heads):
        blk = acc[:, hh * LANES:(hh + 1) * LANES]
        if norm:
            ms = jnp.mean(blk * blk, axis=-1, keepdims=True)
            blk = blk * lax.rsqrt(ms + EPS) * refs[2][...]
        if post_scale is not None:
            blk = blk * post_scale
        o_ref[hh] = blk.astype(o_ref.dtype)


def _headproj(h, w, g, name, post_scale=None, tm=1024, tn=1024):
    m, k = h.shape
    n_heads = w.shape[1] // LANES
    heads = tn // LANES
    norm = g is not None
    extra_specs = [pl.BlockSpec((1, LANES), lambda i, j: (0, 0))] if norm else []
    extra_args = [g.reshape(1, LANES)] if norm else []
    return pl.pallas_call(
        functools.partial(_headproj_kernel, heads=heads, norm=norm, post_scale=post_scale),
        grid=(m // tm, w.shape[1] // tn),
        in_specs=[pl.BlockSpec((tm, k), lambda i, j: (i, 0)),
                  pl.BlockSpec((k, tn), lambda i, j: (0, j))] + extra_specs,
        out_specs=pl.BlockSpec((heads, tm, LANES), lambda i, j: (j, i, 0)),
        out_shape=jax.ShapeDtypeStruct((n_heads, m, LANES), BF16),
        compiler_params=_params(("parallel", "arbitrary")),
        name=name,
    )(h, w, *extra_args)


def _small_kernel(a_ref, w_ref, g_ref, k_ref, v_ref, ki_ref, wi_ref):
    acc = jnp.dot(a_ref[...], w_ref[...], preferred_element_type=F32)
    kk = acc[:, 0:LANES]
    ms = jnp.mean(kk * kk, axis=-1, keepdims=True)
    k_ref[...] = (kk * lax.rsqrt(ms + EPS) * g_ref[...]).astype(k_ref.dtype)
    v_ref[...] = acc[:, LANES:2 * LANES].astype(v_ref.dtype)
    ki_ref[...] = acc[:, 2 * LANES:3 * LANES].astype(ki_ref.dtype)
    wi_ref[...] = acc[:, 3 * LANES:4 * LANES]


def _small_proj(h, w_small, g, tm=1024):
    m, k = h.shape
    row = lambda i: (i, 0)
    return pl.pallas_call(
        _small_kernel,
        grid=(m // tm,),
        in_specs=[pl.BlockSpec((tm, k), row),
                  pl.BlockSpec((k, 4 * LANES), lambda i: (0, 0)),
                  pl.BlockSpec((1, LANES), lambda i: (0, 0))],
        out_specs=[pl.BlockSpec((tm, LANES), row)] * 4,
        out_shape=[jax.ShapeDtypeStruct((m, LANES), BF16)] * 3
                  + [jax.ShapeDtypeStruct((m, LANES), F32)],
        compiler_params=_params(("parallel",)),
        name="kv_idx_proj",
    )(h, w_small, g.reshape(1, LANES))


def _attn_kernel(qi_ref, kidx_ref, wt_ref, q_ref, k_ref, vt_ref, near_ref, cvec_ref, eye_ref, o_ref,
                 key_ref, m_ref, acc_ref, qa_ref, s_ref, lim_ref,
                 *, k_sel, idx_scale, pos_bits):
    j = pl.program_id(1)
    nt = (((1,), (1,)), ((), ()))
    n_chunks = (j + 4) // 4

    key_row = lax.broadcasted_iota(I32, (SCORE_CHUNK, Q_BLOCK), 0)
    q_col = lax.broadcasted_iota(I32, (SCORE_CHUNK, Q_BLOCK), 1)

    def score_chunk(c, carry):
        start = pl.multiple_of(c * SCORE_CHUNK, SCORE_CHUNK)
        kc = kidx_ref[pl.ds(start, SCORE_CHUNK), :]
        acc = jnp.zeros((SCORE_CHUNK, Q_BLOCK), F32)
        for hp in range(N_IDX_HEADS // 2):
            qpair = qi_ref[2 * hp:2 * hp + 2].reshape(2 * Q_BLOCK, IDX_DIM)
            d = lax.dot_general(kc, qpair, nt, preferred_element_type=F32)
            acc = acc + jnp.maximum(d[:, :Q_BLOCK], 0.0) * wt_ref[2 * hp:2 * hp + 1, :]
            acc = acc + jnp.maximum(d[:, Q_BLOCK:], 0.0) * wt_ref[2 * hp + 1:2 * hp + 2, :]
        s = acc * idx_scale
        causal = (start + key_row) <= (j * Q_BLOCK + q_col)
        s = jnp.where(causal, s, -jnp.inf)
        bits = lax.bitcast_convert_type(s, I32)
        key_ref[pl.ds(start, SCORE_CHUNK), :] = bits ^ ((bits >> 31) & 0x7FFFFFFF)
        return carry

    lax.fori_loop(0, n_chunks, score_chunk, 0)

    n_count = n_chunks
    count_row = lax.broadcasted_iota(I32, (COUNT_CHUNK, Q_BLOCK), 0)

    def count_keys(pred):
        def body(c, a):
            start = pl.multiple_of(c * COUNT_CHUNK, COUNT_CHUNK)
            hit = jnp.where(pred(key_ref[pl.ds(start, COUNT_CHUNK), :], start + count_row), 1, 0)
            return a + jnp.sum(hit.reshape(COUNT_CHUNK // SUBLANES, SUBLANES, Q_BLOCK), axis=0)

        a = lax.fori_loop(0, n_count, body, jnp.zeros((SUBLANES, Q_BLOCK), I32))
        return jnp.sum(a, axis=0, keepdims=True)

    def bit_body(i, carry):
        tu, cnt_tu = carry
        cu = tu | jnp.left_shift(jnp.int32(1), 31 - i)
        cs = cu ^ INT_MIN
        cnt = count_keys(lambda keys, pos: keys >= cs)
        take = cnt >= k_sel
        return jnp.where(take, cu, tu), jnp.where(take, cnt, cnt_tu)

    tu, cnt_ge = lax.fori_loop(
        0, 32, bit_body,
        (jnp.zeros((1, Q_BLOCK), I32), jnp.full((1, Q_BLOCK), n_count * COUNT_CHUNK, I32)))
    thr = tu ^ INT_MIN

    tied = (cnt_ge > k_sel) & (thr > KEY_NEG_INF)
    lim_ref[...] = jnp.full(lim_ref.shape, INT_MAX, I32)

    @pl.when(jnp.max(jnp.where(tied, 1, 0)) > 0)
    def _():
        n_tied_kept = k_sel - count_keys(lambda keys, pos: keys > thr)

        def pos_body(i, x):
            cand = x | jnp.left_shift(jnp.int32(1), pos_bits - 1 - i)
            cnt = count_keys(lambda keys, pos: (keys == thr) & (pos < cand))
            return jnp.where(cnt <= n_tied_kept, cand, x)

        x = lax.fori_loop(0, pos_bits, pos_body, jnp.zeros((1, Q_BLOCK), I32))
        lim_ref[...] = jnp.where(tied, x, INT_MAX)

    pos_lim = lim_ref[...]

    def selected(keys, pos):
        return (keys > thr) | ((keys == thr) & (pos < pos_lim))

    m_ref[...] = jnp.full(m_ref.shape, NEG, F32)
    acc_ref[...] = jnp.zeros(acc_ref.shape, F32)
    qa_ref[:, :HEAD_DIM] = q_ref[...].reshape(N_HEADS * Q_BLOCK, HEAD_DIM)
    qa_ref[:, HEAD_DIM:] = eye_ref[...]

    def softmax_step(h, t):
        hs = slice(h * Q_BLOCK, (h + 1) * Q_BLOCK)
        m_old = m_ref[:, hs]
        m_new = jnp.maximum(m_old, jnp.max(t, axis=0, keepdims=True))
        m_ref[:, hs] = m_new
        return jnp.exp2(t - m_new).astype(BF16), jnp.exp2(m_old - m_new)

    def attend(keys, vt, logits_of):
        groups = [slice(g * HEAD_GROUP * Q_BLOCK, (g + 1) * HEAD_GROUP * Q_BLOCK)
                  for g in range(N_HEADS // HEAD_GROUP)]
        n_keys = keys.shape[0]
        for g, gs in enumerate(groups):
            s_ref[g, 0:n_keys, :] = lax.dot_general(keys, qa_ref[gs, 0:keys.shape[1]], nt,
                                                    preferred_element_type=F32)
        for g, gs in enumerate(groups):
            pa = [softmax_step(g * HEAD_GROUP + e,
                               logits_of(g * HEAD_GROUP + e,
                                         s_ref[g, 0:n_keys, e * Q_BLOCK:(e + 1) * Q_BLOCK]))
                  for e in range(HEAD_GROUP)]
            p = jnp.concatenate([x[0] for x in pa], axis=1)
            alpha = jnp.concatenate([x[1] for x in pa], axis=1)
            pv = jnp.dot(vt, p, preferred_element_type=F32)
            acc_ref[:, gs] = acc_ref[:, gs] * alpha + pv

    far_end = (j - 1) * Q_BLOCK
    far_row = lax.broadcasted_iota(I32, (FAR_CHUNK, Q_BLOCK), 0)

    def far_body(c, carry):
        lo = c * FAR_CHUNK
        start = pl.multiple_of(jnp.minimum(lo, jnp.maximum(far_end - FAR_CHUNK, 0)), LANES)
        pos = start + far_row
        sel = selected(key_ref[pl.ds(start, FAR_CHUNK), :], pos) & (pos >= lo) & (pos < far_end)
        maskb = jnp.where(sel, 0.0, NEG).astype(BF16)
        kaug = jnp.concatenate([k_ref[pl.ds(start, FAR_CHUNK), :], maskb], axis=1)
        vt = vt_ref[:, pl.ds(start, FAR_CHUNK)]
        attend(kaug, vt, lambda h, s: s)
        return carry

    lax.fori_loop(0, (jnp.maximum(far_end, 0) + FAR_CHUNK - 1) // FAR_CHUNK, far_body, 0)
    m_ref[...] = m_ref[...] + cvec_ref[...]

    near_start = pl.multiple_of(jnp.maximum(j - 1, 0) * Q_BLOCK, LANES)
    first = jnp.where(j == 0, 1, 0)
    near_pos = near_start + lax.broadcasted_iota(I32, (NEAR_KEYS, Q_BLOCK), 0)
    sel = selected(key_ref[pl.ds(near_start, NEAR_KEYS), :], near_pos)
    maskb = jnp.where(sel, 0.0, NEG)
    k_near = k_ref[pl.ds(near_start, NEAR_KEYS), :]
    vt_near = vt_ref[:, pl.ds(near_start, NEAR_KEYS)]

    def near_logits(h, s):
        return s + near_ref[first, :, h * Q_BLOCK:(h + 1) * Q_BLOCK] + maskb

    attend(k_near, vt_near, near_logits)

    inv_l = 1.0 / acc_ref[HEAD_DIM:HEAD_DIM + 1, :]
    for h in range(N_HEADS):
        hs = slice(h * Q_BLOCK, (h + 1) * Q_BLOCK)
        o_ref[:, h * HEAD_DIM:(h + 1) * HEAD_DIM] = (
            acc_ref[0:HEAD_DIM, hs] * inv_l[:, hs]).T.astype(o_ref.dtype)


def _t5_bucket_static(n):
    max_exact = N_BUCKETS // 2
    nf = np.maximum(n, 1).astype(np.float32)
    large = max_exact + (np.log(nf / np.float32(max_exact)) / np.float32(math.log(MAX_DISTANCE / max_exact))
                         * np.float32(N_BUCKETS - max_exact)).astype(np.int32)
    large = np.minimum(large, N_BUCKETS - 1)
    return np.where(n < max_exact, n, large)


def _bias_tables(rel_bias, seq):
    buckets = _t5_bucket_static(np.arange(seq, dtype=np.int32))
    far = buckets[Q_BLOCK + 1:]
    assert (far == far[0]).all()
    bias2 = rel_bias.astype(F32) * LOG2E
    span, origin = 4 * NEAR_KEYS, 2 * NEAR_KEYS
    d = np.arange(span) - origin
    r = jnp.where(jnp.asarray(d >= 0)[None, :], bias2[buckets[np.clip(d, 0, seq - 1)]].T, NEG)
    y = jnp.tile(r, (1, NEAR_KEYS))[:, :NEAR_KEYS * (span - 1)].reshape(N_HEADS, NEAR_KEYS, span - 1)

    def table(off):
        t = y[:, :, origin + off:origin + off + Q_BLOCK]
        return jnp.transpose(t, (1, 0, 2)).reshape(NEAR_KEYS, N_HEADS * Q_BLOCK)

    near = jnp.stack([table(Q_BLOCK), table(0)])
    cvec = jnp.repeat(bias2[int(far[0])], Q_BLOCK).reshape(1, N_HEADS * Q_BLOCK)
    return near, cvec


def _attention(qi_hm, kidx, w_t, q_hm, k, v_t, near, cvec, batch, seq):
    m = batch * seq
    n_blk = seq // Q_BLOCK
    k_sel = min(TOPK_MAX, seq // 4)
    hq = N_HEADS * Q_BLOCK
    kern = functools.partial(
        _attn_kernel, k_sel=k_sel, idx_scale=IDX_DIM ** -0.5 * N_IDX_HEADS ** -0.5,
        pos_bits=seq.bit_length())
    blk = lambda b, j: (b * n_blk + j)
    v_t = jnp.concatenate([v_t, jnp.ones((VT_ROWS - HEAD_DIM, m), v_t.dtype)], axis=0)
    assert seq % COUNT_CHUNK == 0 and seq % FAR_CHUNK == 0
    eye = jnp.tile(jnp.eye(Q_BLOCK, dtype=BF16), (N_HEADS, 1))
    return pl.pallas_call(
        kern,
        grid=(batch, n_blk),
        in_specs=[pl.BlockSpec((N_IDX_HEADS, Q_BLOCK, IDX_DIM), lambda b, j: (0, blk(b, j), 0)),
                  pl.BlockSpec((seq, IDX_DIM), lambda b, j: (b, 0)),
                  pl.BlockSpec((N_IDX_HEADS, Q_BLOCK), lambda b, j: (0, blk(b, j))),
                  pl.BlockSpec((N_HEADS, Q_BLOCK, HEAD_DIM), lambda b, j: (0, blk(b, j), 0)),
                  pl.BlockSpec((seq, HEAD_DIM), lambda b, j: (b, 0)),
                  pl.BlockSpec((VT_ROWS, seq), lambda b, j: (0, b)),
                  pl.BlockSpec((2, NEAR_KEYS, hq), lambda b, j: (0, 0, 0)),
                  pl.BlockSpec((1, hq), lambda b, j: (0, 0)),
                  pl.BlockSpec((hq, Q_BLOCK), lambda b, j: (0, 0))],
        out_specs=pl.BlockSpec((Q_BLOCK, N_HEADS * HEAD_DIM), lambda b, j: (blk(b, j), 0)),
        out_shape=jax.ShapeDtypeStruct((m, N_HEADS * HEAD_DIM), BF16),
        scratch_shapes=[pltpu.VMEM((seq, Q_BLOCK), I32),
                        pltpu.VMEM((1, hq), F32),
                        pltpu.VMEM((VT_ROWS, hq), F32),
                        pltpu.VMEM((hq, HEAD_DIM + Q_BLOCK), BF16),
                        pltpu.VMEM((N_HEADS // HEAD_GROUP, FAR_CHUNK, HEAD_GROUP * Q_BLOCK), F32),
                        pltpu.VMEM((1, Q_BLOCK), I32)],
        compiler_params=_params(("parallel", "arbitrary")),
        name="dsa_attention",
    )(qi_hm, kidx, w_t, q_hm, k, v_t, near, cvec, eye)


def _sgu_kernel(uv_ref, g_ref, b_ref, ws_ref, bs_ref, o_ref):
    uv = uv_ref[...]
    u = uv[:, :GMLP_WIDTH]
    v = uv[:, GMLP_WIDTH:]
    mu = jnp.mean(v, axis=-1, keepdims=True)
    vc = v - mu
    var = jnp.mean(vc * vc, axis=-1, keepdims=True)
    vn = (vc * lax.rsqrt(var + EPS) * g_ref[...] + b_ref[...]).astype(BF16)
    r = lax.broadcasted_iota(I32, (CHUNK, CHUNK), 0)
    c = lax.broadcasted_iota(I32, (CHUNK, CHUNK), 1)
    tril = c <= r
    bs = bs_ref[...]
    for g in range(N_GROUPS):
        ws = jnp.where(tril, ws_ref[g], 0.0).astype(BF16)
        mixed = jnp.dot(ws, vn[:, g * GROUP_DIM:(g + 1) * GROUP_DIM],
                        preferred_element_type=F32) + bs[:, g:g + 1]
        o_ref[:, g * GROUP_DIM:(g + 1) * GROUP_DIM] = (
            u[:, g * GROUP_DIM:(g + 1) * GROUP_DIM] * mixed).astype(o_ref.dtype)


def _sgu(uv, ln_g, ln_b, w_s, b_s):
    m = uv.shape[0]
    bs_t = jnp.zeros((CHUNK, LANES), F32).at[:, :N_GROUPS].set(jnp.transpose(b_s))
    return pl.pallas_call(
        _sgu_kernel,
        grid=(m // CHUNK,),
        in_specs=[pl.BlockSpec((CHUNK, 2 * GMLP_WIDTH), lambda i: (i, 0)),
                  pl.BlockSpec((1, GMLP_WIDTH), lambda i: (0, 0)),
                  pl.BlockSpec((1, GMLP_WIDTH), lambda i: (0, 0)),
                  pl.BlockSpec((N_GROUPS, CHUNK, CHUNK), lambda i: (0, 0, 0)),
                  pl.BlockSpec((CHUNK, LANES), lambda i: (0, 0))],
        out_specs=pl.BlockSpec((CHUNK, GMLP_WIDTH), lambda i: (i, 0)),
        out_shape=jax.ShapeDtypeStruct((m, GMLP_WIDTH), BF16),
        compiler_params=_params(("parallel",)),
        name="chunked_sgu",
    )(uv, ln_g.reshape(1, -1), ln_b.reshape(1, -1), w_s, bs_t)


def _merge_kernel(h_ref, ya_ref, yb_ref, wga_ref, wgb_ref, wa_ref, wb_ref, o_ref):
    h = h_ref[...]
    ga = jax.nn.sigmoid(jnp.dot(h, wga_ref[...], preferred_element_type=F32))
    a = jnp.dot(ya_ref[...], wa_ref[...], preferred_element_type=F32)
    out = ga * a
    gb = jax.nn.sigmoid(jnp.dot(h, wgb_ref[...], preferred_element_type=F32))
    b = jnp.dot(yb_ref[...], wb_ref[...], preferred_element_type=F32)
    o_ref[...] = (out + gb * b).astype(o_ref.dtype)


def _merge(h, ya, yb, wga, wgb, wa, wb, tm=512, tn=512):
    m, d = h.shape
    n = wga.shape[1]
    ka, kb = ya.shape[1], yb.shape[1]
    row = lambda i, j: (i, 0)
    colw = lambda i, j: (0, j)
    return pl.pallas_call(
        _merge_kernel,
        grid=(m // tm, n // tn),
        in_specs=[pl.BlockSpec((tm, d), row), pl.BlockSpec((tm, ka), row), pl.BlockSpec((tm, kb), row),
                  pl.BlockSpec((d, tn), colw), pl.BlockSpec((d, tn), colw),
                  pl.BlockSpec((ka, tn), colw), pl.BlockSpec((kb, tn), colw)],
        out_specs=pl.BlockSpec((tm, tn), lambda i, j: (i, j)),
        out_shape=jax.ShapeDtypeStruct((m, n), BF16),
        compiler_params=_params(("parallel", "arbitrary")),
        name="branch_merge",
    )(h, ya, yb, wga, wgb, wa, wb)


def _ple_kernel(p_ref, w_ref, g_ref, o_ref):
    acc = jnp.dot(p_ref[...].astype(BF16), w_ref[...], preferred_element_type=F32)
    ms = jnp.mean(acc * acc, axis=-1, keepdims=True)
    o_ref[...] = (acc * lax.rsqrt(ms + EPS) * g_ref[...]).astype(o_ref.dtype)


def _ple(p, w, g, tm=256):
    m, dp = p.shape
    d = w.shape[1]
    return pl.pallas_call(
        _ple_kernel,
        grid=(m // tm,),
        in_specs=[pl.BlockSpec((tm, dp), lambda i: (i, 0)),
                  pl.BlockSpec((dp, d), lambda i: (0, 0)),
                  pl.BlockSpec((1, d), lambda i: (0, 0))],
        out_specs=pl.BlockSpec((tm, d), lambda i: (i, 0)),
        out_shape=jax.ShapeDtypeStruct((m, d), F32),
        compiler_params=_params(("parallel",)),
        name="ple_embed",
    )(p, w, g.reshape(1, d))


def kernel(x, p, w_in, q_norm_g, k_norm_g, rel_bias, sgu_ln_g, sgu_ln_b, sgu_w, sgu_b, w_branch_a, w_branch_b, w_out, norm_mix_g, norm_ffn_g, w_gate_ffn, w_up_ffn, w_down_ffn, w_ple, ple_norm_g, w_ple_gate, norm_ple_g):
    batch, seq, d_model = x.shape
    depth = w_in.shape[0]
    m = batch * seq
    a_width = N_HEADS * HEAD_DIM
    sizes = (a_width, HEAD_DIM, HEAD_DIM, N_IDX_HEADS * IDX_DIM, IDX_DIM, N_IDX_HEADS,
             2 * GMLP_WIDTH, d_model, d_model)
    offs = np.concatenate([[0], np.cumsum(sizes)])
    near, cvec = _bias_tables(rel_bias, seq)

    xf = x.reshape(m, d_model)
    for i in range(depth):
        seg = lambda s: w_in[i][:, int(offs[s]):int(offs[s + 1])].astype(BF16)
        w_q, w_k, w_v, w_qi, w_ki, w_wi, w_uv, w_ga, w_gb = [seg(s) for s in range(9)]
        w_small = jnp.concatenate(
            [w_k, w_v, w_ki, jnp.pad(w_wi, ((0, 0), (0, LANES - N_IDX_HEADS)))], axis=1)

        h = _rmsnorm(xf, norm_mix_g[i])
        q_hm = _headproj(h, w_q, q_norm_g[i], "q_proj", post_scale=HEAD_DIM ** -0.5 * LOG2E)
        qi_hm = _headproj(h, w_qi, None, "q_idx_proj")
        k_, v_, kidx, widx = _small_proj(h, w_small, k_norm_g[i])
        uv = _matmul(h, [w_uv], [], lambda a, e: jax.nn.gelu(a[0]), F32, 1024, 1024, "uv_proj")
        y_a = _attention(qi_hm, kidx, jnp.transpose(widx[:, :N_IDX_HEADS]), q_hm, k_,
                         jnp.transpose(v_), near, cvec, batch, seq)
        y_b = _sgu(uv, sgu_ln_g[i], sgu_ln_b[i], sgu_w[i], sgu_b[i])
        merged = _merge(h, y_a, y_b, w_ga, w_gb,
                        w_branch_a[i].astype(BF16), w_branch_b[i].astype(BF16))
        x1 = _matmul(merged, [w_out[i].astype(BF16)], [xf],
                     lambda a, e: e[0] + a[0], F32, 1024, 512, "out_proj")

        h2 = _rmsnorm(x1, norm_ffn_g[i])
        t = _matmul(h2, [w_gate_ffn[i].astype(BF16), w_up_ffn[i].astype(BF16)], [],
                    lambda a, e: jax.nn.silu(a[0]) * a[1], BF16, 1024, 256, "ffn_gate_up")
        d_ff = t.shape[1]
        x2, x2g, ssq2 = _res_matmul(t, w_down_ffn[i].astype(BF16), x1, norm_ple_g[i],
                                    512, 512, d_ff, "ffn_down")

        pe = _ple(p[i].reshape(m, -1), w_ple[i].astype(BF16), ple_norm_g[i])
        xf = _matmul(x2g, [w_ple_gate[i].astype(BF16)], [x2, pe],
                     lambda a, e: e[0] + jax.nn.sigmoid(a[0]) * e[1], F32, 1024, 512, "ple_gate",
                     row_ssq=ssq2)
    return xf.reshape(batch, seq, d_model)
```

```python
import functools
import math

import numpy as np
import jax
import jax.numpy as jnp
from jax import lax
from jax.experimental import pallas as pl
from jax.experimental.pallas import tpu as pltpu

F32 = jnp.float32
BF16 = jnp.bfloat16
I32 = jnp.int32

N_HEADS = 16
HEAD_DIM = 128
N_IDX_HEADS = 32
IDX_DIM = 128
TOPK_MAX = 256
Q_BLOCK = 128
N_BUCKETS = 32
MAX_DISTANCE = 128
GMLP_WIDTH = 2048
N_GROUPS = 8
GROUP_DIM = GMLP_WIDTH // N_GROUPS
CHUNK = 128
EPS = 1e-6

LANES = 128
SUBLANES = 8
VMEM_LIMIT = 56 * 1024 * 1024
NEG = -(2.0 ** 100)
INT_MIN = -2 ** 31
INT_MAX = 2 ** 31 - 1
KEY_NEG_INF = -2139095041
LOG2E = math.log2(math.e)

SCORE_CHUNK = 4 * LANES
COUNT_CHUNK = SCORE_CHUNK
FAR_CHUNK = 4 * LANES
NEAR_KEYS = 2 * Q_BLOCK
HEAD_GROUP = 4
VT_ROWS = HEAD_DIM + 16
MM_SUB_ROWS = 256


def _params(sem):
    return pltpu.CompilerParams(dimension_semantics=sem, vmem_limit_bytes=VMEM_LIMIT)


def _rmsnorm_kernel(x_ref, g_ref, o_ref):
    x = x_ref[...]
    ms = jnp.mean(x * x, axis=-1, keepdims=True)
    o_ref[...] = (x * lax.rsqrt(ms + EPS) * g_ref[...]).astype(o_ref.dtype)


def _rmsnorm(x, g, tm=256):
    m, d = x.shape
    return pl.pallas_call(
        _rmsnorm_kernel,
        grid=(m // tm,),
        in_specs=[pl.BlockSpec((tm, d), lambda i: (i, 0)),
                  pl.BlockSpec((1, d), lambda i: (0, 0))],
        out_specs=pl.BlockSpec((tm, d), lambda i: (i, 0)),
        out_shape=jax.ShapeDtypeStruct((m, d), BF16),
        compiler_params=_params(("parallel",)),
        name="rmsnorm",
    )(x, g.reshape(1, d))


def _lane_tiles(x):
    return [x[:, c * LANES:(c + 1) * LANES] for c in range(x.shape[1] // LANES)]


def _mm_kernel(*refs, n_w, n_x, epi, norm_dim):
    a_ref = refs[0]
    w_refs = refs[1:1 + n_w]
    x_refs = refs[1 + n_w:1 + n_w + n_x]
    if norm_dim:
        ssq_ref, o_ref, r_ref = refs[1 + n_w + n_x:]

        @pl.when(pl.program_id(1) == 0)
        def _():
            ms = jnp.sum(ssq_ref[...], axis=-1, keepdims=True) * (1.0 / norm_dim)
            r_ref[...] = jnp.broadcast_to(lax.rsqrt(ms + EPS), r_ref.shape)
    else:
        o_ref = refs[1 + n_w + n_x]
    for s in range(a_ref.shape[0] // MM_SUB_ROWS):
        rows = slice(s * MM_SUB_ROWS, (s + 1) * MM_SUB_ROWS)
        a = a_ref[rows, :]
        accs = [jnp.dot(a, w[...], preferred_element_type=F32) for w in w_refs]
        if norm_dim:
            accs = [acc * r_ref[rows, :] for acc in accs]
        o_ref[rows, :] = epi(accs, [x[rows, :] for x in x_refs]).astype(o_ref.dtype)


def _matmul(a, ws, extras, epi, out_dtype, tm, tn, name, row_ssq=None):
    m, k = a.shape
    n = ws[0].shape[1]
    normed = row_ssq is not None
    kern = functools.partial(_mm_kernel, n_w=len(ws), n_x=len(extras), epi=epi,
                             norm_dim=k if normed else 0)
    return pl.pallas_call(
        kern,
        grid=(m // tm, n // tn),
        in_specs=([pl.BlockSpec((tm, k), lambda i, j: (i, 0))]
                  + [pl.BlockSpec((k, tn), lambda i, j: (0, j)) for _ in ws]
                  + [pl.BlockSpec((tm, tn), lambda i, j: (i, j)) for _ in extras]
                  + ([pl.BlockSpec((tm, row_ssq.shape[1]), lambda i, j: (i, 0))] if normed else [])),
        out_specs=pl.BlockSpec((tm, tn), lambda i, j: (i, j)),
        out_shape=jax.ShapeDtypeStruct((m, n), out_dtype),
        scratch_shapes=[pltpu.VMEM((tm, tn), F32)] if normed else [],
        compiler_params=_params(("parallel", "arbitrary")),
        name=name,
    )(a, *ws, *extras, *([row_ssq] if normed else []))


def _res_mm_kernel(a_ref, w_ref, r_ref, g_ref, o_ref, og_ref, ssq_ref, *, nk):
    def sub_blocks(base_ref, last):
        for s in range(a_ref.shape[0] // MM_SUB_ROWS):
            rows = slice(s * MM_SUB_ROWS, (s + 1) * MM_SUB_ROWS)
            x = base_ref[rows, :] + jnp.dot(a_ref[rows, :], w_ref[...],
                                            preferred_element_type=F32)
            o_ref[rows, :] = x
            if last:
                og_ref[rows, :] = (x * g_ref[...]).astype(og_ref.dtype)
                ssq_ref[rows, :] = functools.reduce(lambda u, v: u + v, _lane_tiles(x * x))

    if nk == 1:
        sub_blocks(r_ref, True)
    else:
        @pl.when(pl.program_id(2) == 0)
        def _():
            sub_blocks(r_ref, False)

        @pl.when(pl.program_id(2) == 1)
        def _():
            sub_blocks(o_ref, True)


def _res_matmul(a, w, res, g_next, tm, tn, tk, name):
    m, k = a.shape
    n = w.shape[1]
    nk = k // tk
    assert nk in (1, 2) and nk * tk == k
    return pl.pallas_call(
        functools.partial(_res_mm_kernel, nk=nk),
        grid=(m // tm, n // tn, nk),
        in_specs=[pl.BlockSpec((tm, tk), lambda i, j, kk: (i, kk)),
                  pl.BlockSpec((tk, tn), lambda i, j, kk: (kk, j)),
                  pl.BlockSpec((tm, tn), lambda i, j, kk: (i, j)),
                  pl.BlockSpec((1, tn), lambda i, j, kk: (0, j))],
        out_specs=[pl.BlockSpec((tm, tn), lambda i, j, kk: (i, j)),
                   pl.BlockSpec((tm, tn), lambda i, j, kk: (i, j)),
                   pl.BlockSpec((tm, LANES), lambda i, j, kk: (i, j))],
        out_shape=[jax.ShapeDtypeStruct((m, n), F32),
                   jax.ShapeDtypeStruct((m, n), BF16),
                   jax.ShapeDtypeStruct((m, (n // tn) * LANES), F32)],
        compiler_params=_params(("parallel", "arbitrary", "arbitrary")),
        name=name,
    )(a, w, res, g_next.reshape(1, n))


def _headproj_kernel(*refs, heads, norm, post_scale):
    a_ref, w_ref = refs[0], refs[1]
    o_ref = refs[-1]
    acc = jnp.dot(a_ref[...], w_ref[...], preferred_element_type=F32)
    for hh in range(heads):
        blk = acc[:, hh * LANES:(hh + 1) * LANES]
        if norm:
            ms = jnp.mean(blk * blk, axis=-1, keepdims=True)
            blk = blk * lax.rsqrt(ms + EPS) * refs[2][...]
        if post_scale is not None:
            blk = blk * post_scale
        o_ref[hh] = blk.astype(o_ref.dtype)


def _headproj(h, w, g, name, post_scale=None, tm=1024, tn=1024):
    m, k = h.shape
    n_heads = w.shape[1] // LANES
    heads = tn // LANES
    norm = g is not None
    extra_specs = [pl.BlockSpec((1, LANES), lambda i, j: (0, 0))] if norm else []
    extra_args = [g.reshape(1, LANES)] if norm else []
    return pl.pallas_call(
        functools.partial(_headproj_kernel, heads=heads, norm=norm, post_scale=post_scale),
        grid=(m // tm, w.shape[1] // tn),
        in_specs=[pl.BlockSpec((tm, k), lambda i, j: (i, 0)),
                  pl.BlockSpec((k, tn), lambda i, j: (0, j))] + extra_specs,
        out_specs=pl.BlockSpec((heads, tm, LANES), lambda i, j: (j, i, 0)),
        out_shape=jax.ShapeDtypeStruct((n_heads, m, LANES), BF16),
        compiler_params=_params(("parallel", "arbitrary")),
        name=name,
    )(h, w, *extra_args)


def _small_kernel(a_ref, w_ref, g_ref, k_ref, v_ref, ki_ref, wi_ref):
    acc = jnp.dot(a_ref[...], w_ref[...], preferred_element_type=F32)
    kk = acc[:, 0:LANES]
    ms = jnp.mean(kk * kk, axis=-1, keepdims=True)
    k_ref[...] = (kk * lax.rsqrt(ms + EPS) * g_ref[...]).astype(k_ref.dtype)
    v_ref[...] = acc[:, LANES:2 * LANES].astype(v_ref.dtype)
    ki_ref[...] = acc[:, 2 * LANES:3 * LANES].astype(ki_ref.dtype)
    wi_ref[...] = acc[:, 3 * LANES:4 * LANES]


def _small_proj(h, w_small, g, tm=1024):
    m, k = h.shape
    row = lambda i: (i, 0)
    return pl.pallas_call(
        _small_kernel,
        grid=(m // tm,),
        in_specs=[pl.BlockSpec((tm, k), row),
                  pl.BlockSpec((k, 4 * LANES), lambda i: (0, 0)),
                  pl.BlockSpec((1, LANES), lambda i: (0, 0))],
        out_specs=[pl.BlockSpec((tm, LANES), row)] * 4,
        out_shape=[jax.ShapeDtypeStruct((m, LANES), BF16)] * 3
                  + [jax.ShapeDtypeStruct((m, LANES), F32)],
        compiler_params=_params(("parallel",)),
        name="kv_idx_proj",
    )(h, w_small, g.reshape(1, LANES))


def _attn_kernel(qi_ref, kidx_ref, wt_ref, q_ref, k_ref, vt_ref, near_ref, cvec_ref, eye_ref, o_ref,
                 key_ref, m_ref, acc_ref, qa_ref, s_ref, lim_ref,
                 *, k_sel, idx_scale, pos_bits):
    j = pl.program_id(1)
    nt = (((1,), (1,)), ((), ()))
    n_chunks = (j + 4) // 4

    key_row = lax.broadcasted_iota(I32, (SCORE_CHUNK, Q_BLOCK), 0)
    q_col = lax.broadcasted_iota(I32, (SCORE_CHUNK, Q_BLOCK), 1)

    def score_chunk(c, carry):
        start = pl.multiple_of(c * SCORE_CHUNK, SCORE_CHUNK)
        kc = kidx_ref[pl.ds(start, SCORE_CHUNK), :]
        acc = jnp.zeros((SCORE_CHUNK, Q_BLOCK), F32)
        for hp in range(N_IDX_HEADS // 2):
            qpair = qi_ref[2 * hp:2 * hp + 2].reshape(2 * Q_BLOCK, IDX_DIM)
            d = lax.dot_general(kc, qpair, nt, preferred_element_type=F32)
            acc = acc + jnp.maximum(d[:, :Q_BLOCK], 0.0) * wt_ref[2 * hp:2 * hp + 1, :]
            acc = acc + jnp.maximum(d[:, Q_BLOCK:], 0.0) * wt_ref[2 * hp + 1:2 * hp + 2, :]
        s = acc * idx_scale
        causal = (start + key_row) <= (j * Q_BLOCK + q_col)
        s = jnp.where(causal, s, -jnp.inf)
        bits = lax.bitcast_convert_type(s, I32)
        key_ref[pl.ds(start, SCORE_CHUNK), :] = bits ^ ((bits >> 31) & 0x7FFFFFFF)
        return carry

    lax.fori_loop(0, n_chunks, score_chunk, 0)

    n_count = n_chunks
    count_row = lax.broadcasted_iota(I32, (COUNT_CHUNK, Q_BLOCK), 0)

    def count_keys(pred):
        def chunk_hits(c):
            start = pl.multiple_of(c * COUNT_CHUNK, COUNT_CHUNK)
            hit = jnp.where(pred(key_ref[pl.ds(start, COUNT_CHUNK), :], start + count_row), 1, 0)
            return jnp.sum(hit.reshape(COUNT_CHUNK // SUBLANES, SUBLANES, Q_BLOCK), axis=0)

        a = lax.fori_loop(0, n_count // 2,
                          lambda c, a: a + chunk_hits(2 * c) + chunk_hits(2 * c + 1),
                          jnp.zeros((SUBLANES, Q_BLOCK), I32))
        a = lax.fori_loop(n_count & ~1, n_count, lambda c, a: a + chunk_hits(c), a)
        return jnp.sum(a, axis=0, keepdims=True)

    def bit_body(i, carry):
        tu, cnt_tu = carry
        cu = tu | jnp.left_shift(jnp.int32(1), 31 - i)
        cs = cu ^ INT_MIN
        cnt = count_keys(lambda keys, pos: keys >= cs)
        take = cnt >= k_sel
        return jnp.where(take, cu, tu), jnp.where(take, cnt, cnt_tu)

    tu, cnt_ge = lax.fori_loop(
        0, 32, bit_body,
        (jnp.zeros((1, Q_BLOCK), I32), jnp.full((1, Q_BLOCK), n_count * COUNT_CHUNK, I32)))
    thr = tu ^ INT_MIN

    tied = (cnt_ge > k_sel) & (thr > KEY_NEG_INF)
    lim_ref[...] = jnp.full(lim_ref.shape, INT_MAX, I32)

    @pl.when(jnp.max(jnp.where(tied, 1, 0)) > 0)
    def _():
        n_tied_kept = k_sel - count_keys(lambda keys, pos: keys > thr)

        def pos_body(i, x):
            cand = x | jnp.left_shift(jnp.int32(1), pos_bits - 1 - i)
            cnt = count_keys(lambda keys, pos: (keys == thr) & (pos < cand))
            return jnp.where(cnt <= n_tied_kept, cand, x)

        x = lax.fori_loop(0, pos_bits, pos_body, jnp.zeros((1, Q_BLOCK), I32))
        lim_ref[...] = jnp.where(tied, x, INT_MAX)

    pos_lim = lim_ref[...]

    def selected(keys, pos):
        return (keys > thr) | ((keys == thr) & (pos < pos_lim))

    m_ref[...] = jnp.full(m_ref.shape, NEG, F32)
    acc_ref[...] = jnp.zeros(acc_ref.shape, F32)
    qa_ref[:, :HEAD_DIM] = q_ref[...].reshape(N_HEADS * Q_BLOCK, HEAD_DIM)
    qa_ref[:, HEAD_DIM:] = eye_ref[...]

    def softmax_step(h, t):
        hs = slice(h * Q_BLOCK, (h + 1) * Q_BLOCK)
        m_old = m_ref[:, hs]
        m_new = jnp.maximum(m_old, jnp.max(t, axis=0, keepdims=True))
        m_ref[:, hs] = m_new
        return jnp.exp2(t - m_new).astype(BF16), jnp.exp2(m_old - m_new)

    def attend(keys, vt, logits_of):
        groups = [slice(g * HEAD_GROUP * Q_BLOCK, (g + 1) * HEAD_GROUP * Q_BLOCK)
                  for g in range(N_HEADS // HEAD_GROUP)]
        n_keys = keys.shape[0]
        for g, gs in enumerate(groups):
            s_ref[g, 0:n_keys, :] = lax.dot_general(keys, qa_ref[gs, 0:keys.shape[1]], nt,
                                                    preferred_element_type=F32)
        for g, gs in enumerate(groups):
            pa = [softmax_step(g * HEAD_GROUP + e,
                               logits_of(g * HEAD_GROUP + e,
                                         s_ref[g, 0:n_keys, e * Q_BLOCK:(e + 1) * Q_BLOCK]))
                  for e in range(HEAD_GROUP)]
            p = jnp.concatenate([x[0] for x in pa], axis=1)
            alpha = jnp.concatenate([x[1] for x in pa], axis=1)
            pv = jnp.dot(vt, p, preferred_element_type=F32)
            acc_ref[:, gs] = acc_ref[:, gs] * alpha + pv

    far_end = (j - 1) * Q_BLOCK
    far_row = lax.broadcasted_iota(I32, (FAR_CHUNK, Q_BLOCK), 0)

    def far_body(c, carry):
        lo = c * FAR_CHUNK
        start = pl.multiple_of(jnp.minimum(lo, jnp.maximum(far_end - FAR_CHUNK, 0)), LANES)
        pos = start + far_row
        sel = selected(key_ref[pl.ds(start, FAR_CHUNK), :], pos) & (pos >= lo) & (pos < far_end)
        maskb = jnp.where(sel, 0.0, NEG).astype(BF16)
        kaug = jnp.concatenate([k_ref[pl.ds(start, FAR_CHUNK), :], maskb], axis=1)
        vt = vt_ref[:, pl.ds(start, FAR_CHUNK)]
        attend(kaug, vt, lambda h, s: s)
        return carry

    lax.fori_loop(0, (jnp.maximum(far_end, 0) + FAR_CHUNK - 1) // FAR_CHUNK, far_body, 0)
    m_ref[...] = m_ref[...] + cvec_ref[...]

    near_start = pl.multiple_of(jnp.maximum(j - 1, 0) * Q_BLOCK, LANES)
    first = jnp.where(j == 0, 1, 0)
    near_pos = near_start + lax.broadcasted_iota(I32, (NEAR_KEYS, Q_BLOCK), 0)
    sel = selected(key_ref[pl.ds(near_start, NEAR_KEYS), :], near_pos)
    maskb = jnp.where(sel, 0.0, NEG)
    k_near = k_ref[pl.ds(near_start, NEAR_KEYS), :]
    vt_near = vt_ref[:, pl.ds(near_start, NEAR_KEYS)]

    def near_logits(h, s):
        return s + near_ref[first, :, h * Q_BLOCK:(h + 1) * Q_BLOCK] + maskb

    attend(k_near, vt_near, near_logits)

    inv_l = 1.0 / acc_ref[HEAD_DIM:HEAD_DIM + 1, :]
    for h in range(N_HEADS):
        hs = slice(h * Q_BLOCK, (h + 1) * Q_BLOCK)
        o_ref[:, h * HEAD_DIM:(h + 1) * HEAD_DIM] = (
            acc_ref[0:HEAD_DIM, hs] * inv_l[:, hs]).T.astype(o_ref.dtype)


def _t5_bucket_static(n):
    max_exact = N_BUCKETS // 2
    nf = np.maximum(n, 1).astype(np.float32)
    large = max_exact + (np.log(nf / np.float32(max_exact)) / np.float32(math.log(MAX_DISTANCE / max_exact))
                         * np.float32(N_BUCKETS - max_exact)).astype(np.int32)
    large = np.minimum(large, N_BUCKETS - 1)
    return np.where(n < max_exact, n, large)


def _bias_tables(rel_bias, seq):
    buckets = _t5_bucket_static(np.arange(seq, dtype=np.int32))
    far = buckets[Q_BLOCK + 1:]
    assert (far == far[0]).all()
    bias2 = rel_bias.astype(F32) * LOG2E
    span, origin = 4 * NEAR_KEYS, 2 * NEAR_KEYS
    d = np.arange(span) - origin
    r = jnp.where(jnp.asarray(d >= 0)[None, :], bias2[buckets[np.clip(d, 0, seq - 1)]].T, NEG)
    y = jnp.tile(r, (1, NEAR_KEYS))[:, :NEAR_KEYS * (span - 1)].reshape(N_HEADS, NEAR_KEYS, span - 1)

    def table(off):
        t = y[:, :, origin + off:origin + off + Q_BLOCK]
        return jnp.transpose(t, (1, 0, 2)).reshape(NEAR_KEYS, N_HEADS * Q_BLOCK)

    near = jnp.stack([table(Q_BLOCK), table(0)])
    cvec = jnp.repeat(bias2[int(far[0])], Q_BLOCK).reshape(1, N_HEADS * Q_BLOCK)
    return near, cvec


def _attention(qi_hm, kidx, w_t, q_hm, k, v_t, near, cvec, batch, seq):
    m = batch * seq
    n_blk = seq // Q_BLOCK
    k_sel = min(TOPK_MAX, seq // 4)
    hq = N_HEADS * Q_BLOCK
    kern = functools.partial(
        _attn_kernel, k_sel=k_sel, idx_scale=IDX_DIM ** -0.5 * N_IDX_HEADS ** -0.5,
        pos_bits=seq.bit_length())
    blk = lambda b, j: (b * n_blk + j)
    v_t = jnp.concatenate([v_t, jnp.ones((VT_ROWS - HEAD_DIM, m), v_t.dtype)], axis=0)
    assert seq % COUNT_CHUNK == 0 and seq % FAR_CHUNK == 0
    eye = jnp.tile(jnp.eye(Q_BLOCK, dtype=BF16), (N_HEADS, 1))
    return pl.pallas_call(
        kern,
        grid=(batch, n_blk),
        in_specs=[pl.BlockSpec((N_IDX_HEADS, Q_BLOCK, IDX_DIM), lambda b, j: (0, blk(b, j), 0)),
                  pl.BlockSpec((seq, IDX_DIM), lambda b, j: (b, 0)),
                  pl.BlockSpec((N_IDX_HEADS, Q_BLOCK), lambda b, j: (0, blk(b, j))),
                  pl.BlockSpec((N_HEADS, Q_BLOCK, HEAD_DIM), lambda b, j: (0, blk(b, j), 0)),
                  pl.BlockSpec((seq, HEAD_DIM), lambda b, j: (b, 0)),
                  pl.BlockSpec((VT_ROWS, seq), lambda b, j: (0, b)),
                  pl.BlockSpec((2, NEAR_KEYS, hq), lambda b, j: (0, 0, 0)),
                  pl.BlockSpec((1, hq), lambda b, j: (0, 0)),
                  pl.BlockSpec((hq, Q_BLOCK), lambda b, j: (0, 0))],
        out_specs=pl.BlockSpec((Q_BLOCK, N_HEADS * HEAD_DIM), lambda b, j: (blk(b, j), 0)),
        out_shape=jax.ShapeDtypeStruct((m, N_HEADS * HEAD_DIM), BF16),
        scratch_shapes=[pltpu.VMEM((seq, Q_BLOCK), I32),
                        pltpu.VMEM((1, hq), F32),
                        pltpu.VMEM((VT_ROWS, hq), F32),
                        pltpu.VMEM((hq, HEAD_DIM + Q_BLOCK), BF16),
                        pltpu.VMEM((N_HEADS // HEAD_GROUP, FAR_CHUNK, HEAD_GROUP * Q_BLOCK), F32),
                        pltpu.VMEM((1, Q_BLOCK), I32)],
        compiler_params=_params(("parallel", "arbitrary")),
        name="dsa_attention",
    )(qi_hm, kidx, w_t, q_hm, k, v_t, near, cvec, eye)


def _sgu_kernel(uv_ref, g_ref, b_ref, ws_ref, bs_ref, o_ref):
    uv = jax.nn.gelu(uv_ref[...])
    u = uv[:, :GMLP_WIDTH]
    v = uv[:, GMLP_WIDTH:]
    mu = jnp.mean(v, axis=-1, keepdims=True)
    vc = v - mu
    var = jnp.mean(vc * vc, axis=-1, keepdims=True)
    vn = (vc * lax.rsqrt(var + EPS) * g_ref[...] + b_ref[...]).astype(BF16)
    r = lax.broadcasted_iota(I32, (CHUNK, CHUNK), 0)
    c = lax.broadcasted_iota(I32, (CHUNK, CHUNK), 1)
    tril = c <= r
    bs = bs_ref[...]
    for g in range(N_GROUPS):
        ws = jnp.where(tril, ws_ref[g], 0.0).astype(BF16)
        mixed = jnp.dot(ws, vn[:, g * GROUP_DIM:(g + 1) * GROUP_DIM],
                        preferred_element_type=F32) + bs[:, g:g + 1]
        o_ref[:, g * GROUP_DIM:(g + 1) * GROUP_DIM] = (
            u[:, g * GROUP_DIM:(g + 1) * GROUP_DIM] * mixed).astype(o_ref.dtype)


def _sgu(uv, ln_g, ln_b, w_s, b_s):
    m = uv.shape[0]
    bs_t = jnp.zeros((CHUNK, LANES), F32).at[:, :N_GROUPS].set(jnp.transpose(b_s))
    return pl.pallas_call(
        _sgu_kernel,
        grid=(m // CHUNK,),
        in_specs=[pl.BlockSpec((CHUNK, 2 * GMLP_WIDTH), lambda i: (i, 0)),
                  pl.BlockSpec((1, GMLP_WIDTH), lambda i: (0, 0)),
                  pl.BlockSpec((1, GMLP_WIDTH), lambda i: (0, 0)),
                  pl.BlockSpec((N_GROUPS, CHUNK, CHUNK), lambda i: (0, 0, 0)),
                  pl.BlockSpec((CHUNK, LANES), lambda i: (0, 0))],
        out_specs=pl.BlockSpec((CHUNK, GMLP_WIDTH), lambda i: (i, 0)),
        out_shape=jax.ShapeDtypeStruct((m, GMLP_WIDTH), BF16),
        compiler_params=_params(("parallel",)),
        name="chunked_sgu",
    )(uv, ln_g.reshape(1, -1), ln_b.reshape(1, -1), w_s, bs_t)


def _merge_kernel(h_ref, ya_ref, yb_ref, wga_ref, wgb_ref, wa_ref, wb_ref, o_ref):
    h = h_ref[...]
    ga = jax.nn.sigmoid(jnp.dot(h, wga_ref[...], preferred_element_type=F32))
    a = jnp.dot(ya_ref[...], wa_ref[...], preferred_element_type=F32)
    out = ga * a
    gb = jax.nn.sigmoid(jnp.dot(h, wgb_ref[...], preferred_element_type=F32))
    b = jnp.dot(yb_ref[...], wb_ref[...], preferred_element_type=F32)
    o_ref[...] = (out + gb * b).astype(o_ref.dtype)


def _merge(h, ya, yb, wga, wgb, wa, wb, tm=512, tn=512):
    m, d = h.shape
    n = wga.shape[1]
    ka, kb = ya.shape[1], yb.shape[1]
    row = lambda i, j: (i, 0)
    colw = lambda i, j: (0, j)
    return pl.pallas_call(
        _merge_kernel,
        grid=(m // tm, n // tn),
        in_specs=[pl.BlockSpec((tm, d), row), pl.BlockSpec((tm, ka), row), pl.BlockSpec((tm, kb), row),
                  pl.BlockSpec((d, tn), colw), pl.BlockSpec((d, tn), colw),
                  pl.BlockSpec((ka, tn), colw), pl.BlockSpec((kb, tn), colw)],
        out_specs=pl.BlockSpec((tm, tn), lambda i, j: (i, j)),
        out_shape=jax.ShapeDtypeStruct((m, n), BF16),
        compiler_params=_params(("parallel", "arbitrary")),
        name="branch_merge",
    )(h, ya, yb, wga, wgb, wa, wb)


def _ple_kernel(p_ref, w_ref, g_ref, o_ref):
    acc = jnp.dot(p_ref[...].astype(BF16), w_ref[...], preferred_element_type=F32)
    ms = jnp.mean(acc * acc, axis=-1, keepdims=True)
    o_ref[...] = (acc * lax.rsqrt(ms + EPS) * g_ref[...]).astype(o_ref.dtype)


def _ple(p, w, g, tm=256):
    m, dp = p.shape
    d = w.shape[1]
    return pl.pallas_call(
        _ple_kernel,
        grid=(m // tm,),
        in_specs=[pl.BlockSpec((tm, dp), lambda i: (i, 0)),
                  pl.BlockSpec((dp, d), lambda i: (0, 0)),
                  pl.BlockSpec((1, d), lambda i: (0, 0))],
        out_specs=pl.BlockSpec((tm, d), lambda i: (i, 0)),
        out_shape=jax.ShapeDtypeStruct((m, d), F32),
        compiler_params=_params(("parallel",)),
        name="ple_embed",
    )(p, w, g.reshape(1, d))


def kernel(x, p, w_in, q_norm_g, k_norm_g, rel_bias, sgu_ln_g, sgu_ln_b, sgu_w, sgu_b, w_branch_a, w_branch_b, w_out, norm_mix_g, norm_ffn_g, w_gate_ffn, w_up_ffn, w_down_ffn, w_ple, ple_norm_g, w_ple_gate, norm_ple_g):
    batch, seq, d_model = x.shape
    depth = w_in.shape[0]
    m = batch * seq
    a_width = N_HEADS * HEAD_DIM
    sizes = (a_width, HEAD_DIM, HEAD_DIM, N_IDX_HEADS * IDX_DIM, IDX_DIM, N_IDX_HEADS,
             2 * GMLP_WIDTH, d_model, d_model)
    offs = np.concatenate([[0], np.cumsum(sizes)])
    near, cvec = _bias_tables(rel_bias, seq)

    xf = x.reshape(m, d_model)
    for i in range(depth):
        seg = lambda s: w_in[i][:, int(offs[s]):int(offs[s + 1])].astype(BF16)
        w_q, w_k, w_v, w_qi, w_ki, w_wi, w_uv, w_ga, w_gb = [seg(s) for s in range(9)]
        w_small = jnp.concatenate(
            [w_k, w_v, w_ki, jnp.pad(w_wi, ((0, 0), (0, LANES - N_IDX_HEADS)))], axis=1)

        h = _rmsnorm(xf, norm_mix_g[i])
        q_hm = _headproj(h, w_q, q_norm_g[i], "q_proj", post_scale=HEAD_DIM ** -0.5 * LOG2E)
        qi_hm = _headproj(h, w_qi, None, "q_idx_proj")
        k_, v_, kidx, widx = _small_proj(h, w_small, k_norm_g[i])
        uv = _matmul(h, [w_uv], [], lambda a, e: a[0], F32, 1024, 1024, "uv_proj")
        y_a = _attention(qi_hm, kidx, jnp.transpose(widx[:, :N_IDX_HEADS]), q_hm, k_,
                         jnp.transpose(v_), near, cvec, batch, seq)
        y_b = _sgu(uv, sgu_ln_g[i], sgu_ln_b[i], sgu_w[i], sgu_b[i])
        merged = _merge(h, y_a, y_b, w_ga, w_gb,
                        w_branch_a[i].astype(BF16), w_branch_b[i].astype(BF16))
        x1 = _matmul(merged, [w_out[i].astype(BF16)], [xf],
                     lambda a, e: e[0] + a[0], F32, 1024, 512, "out_proj")

        h2 = _rmsnorm(x1, norm_ffn_g[i])
        t = _matmul(h2, [w_gate_ffn[i].astype(BF16), w_up_ffn[i].astype(BF16)], [],
                    lambda a, e: jax.nn.silu(a[0]) * a[1], BF16, 1024, 256, "ffn_gate_up")
        d_ff = t.shape[1]
        x2, x2g, ssq2 = _res_matmul(t, w_down_ffn[i].astype(BF16), x1, norm_ple_g[i],
                                    512, 512, d_ff, "ffn_down")

        pe = _ple(p[i].reshape(m, -1), w_ple[i].astype(BF16), ple_norm_g[i])
        xf = _matmul(x2g, [w_ple_gate[i].astype(BF16)], [x2, pe],
                     lambda a, e: e[0] + jax.nn.sigmoid(a[0]) * e[1], F32, 1024, 512, "ple_gate",
                     row_ssq=ssq2)
    return xf.reshape(batch, seq, d_model)
```

```python
import functools
import math

import numpy as np
import jax
import jax.numpy as jnp
from jax import lax
from jax.experimental import pallas as pl
from jax.experimental.pallas import tpu as pltpu

F32 = jnp.float32
BF16 = jnp.bfloat16
I32 = jnp.int32

N_HEADS = 16
HEAD_DIM = 128
N_IDX_HEADS = 32
IDX_DIM = 128
TOPK_MAX = 256
Q_BLOCK = 128
N_BUCKETS = 32
MAX_DISTANCE = 128
GMLP_WIDTH = 2048
N_GROUPS = 8
GROUP_DIM = GMLP_WIDTH // N_GROUPS
CHUNK = 128
EPS = 1e-6

LANES = 128
SUBLANES = 8
VMEM_LIMIT = 56 * 1024 * 1024
NEG = -(2.0 ** 100)
INT_MIN = -2 ** 31
INT_MAX = 2 ** 31 - 1
KEY_NEG_INF = -2139095041
LOG2E = math.log2(math.e)

SCORE_CHUNK = 4 * LANES
COUNT_CHUNK = SCORE_CHUNK
FAR_CHUNK = 4 * LANES
NEAR_KEYS = 2 * Q_BLOCK
HEAD_GROUP = 4
VT_ROWS = HEAD_DIM + 16
NORM_ROWS = 128
MM_SUB_ROWS = 256


def _params(sem):
    return pltpu.CompilerParams(dimension_semantics=sem, vmem_limit_bytes=VMEM_LIMIT)


def _rmsnorm_kernel(x_ref, g_ref, o_ref):
    x = x_ref[...]
    ms = jnp.mean(x * x, axis=-1, keepdims=True)
    o_ref[...] = (x * lax.rsqrt(ms + EPS) * g_ref[...]).astype(o_ref.dtype)


def _rmsnorm(x, g, tm=256):
    m, d = x.shape
    return pl.pallas_call(
        _rmsnorm_kernel,
        grid=(m // tm,),
        in_specs=[pl.BlockSpec((tm, d), lambda i: (i, 0)),
                  pl.BlockSpec((1, d), lambda i: (0, 0))],
        out_specs=pl.BlockSpec((tm, d), lambda i: (i, 0)),
        out_shape=jax.ShapeDtypeStruct((m, d), BF16),
        compiler_params=_params(("parallel",)),
        name="rmsnorm",
    )(x, g.reshape(1, d))


def _lane_tiles(x):
    return [x[:, c * LANES:(c + 1) * LANES] for c in range(x.shape[1] // LANES)]


def _mm_kernel(*refs, n_w, n_x, epi, norm_dim):
    a_ref = refs[0]
    w_refs = refs[1:1 + n_w]
    x_refs = refs[1 + n_w:1 + n_w + n_x]
    if norm_dim:
        ssq_ref, o_ref, r_ref = refs[1 + n_w + n_x:]

        @pl.when(pl.program_id(1) == 0)
        def _():
            ms = jnp.sum(ssq_ref[...], axis=-1, keepdims=True) * (1.0 / norm_dim)
            r_ref[...] = jnp.broadcast_to(lax.rsqrt(ms + EPS), r_ref.shape)
    else:
        o_ref = refs[1 + n_w + n_x]
    for s in range(a_ref.shape[0] // MM_SUB_ROWS):
        rows = slice(s * MM_SUB_ROWS, (s + 1) * MM_SUB_ROWS)
        a = a_ref[rows, :]
        accs = [jnp.dot(a, w[...], preferred_element_type=F32) for w in w_refs]
        if norm_dim:
            accs = [acc * r_ref[rows, :] for acc in accs]
        o_ref[rows, :] = epi(accs, [x[rows, :] for x in x_refs]).astype(o_ref.dtype)


def _matmul(a, ws, extras, epi, out_dtype, tm, tn, name, row_ssq=None):
    m, k = a.shape
    n = ws[0].shape[1]
    normed = row_ssq is not None
    kern = functools.partial(_mm_kernel, n_w=len(ws), n_x=len(extras), epi=epi,
                             norm_dim=k if normed else 0)
    return pl.pallas_call(
        kern,
        grid=(m // tm, n // tn),
        in_specs=([pl.BlockSpec((tm, k), lambda i, j: (i, 0))]
                  + [pl.BlockSpec((k, tn), lambda i, j: (0, j)) for _ in ws]
                  + [pl.BlockSpec((tm, tn), lambda i, j: (i, j)) for _ in extras]
                  + ([pl.BlockSpec((tm, row_ssq.shape[1]), lambda i, j: (i, 0))] if normed else [])),
        out_specs=pl.BlockSpec((tm, tn), lambda i, j: (i, j)),
        out_shape=jax.ShapeDtypeStruct((m, n), out_dtype),
        scratch_shapes=[pltpu.VMEM((tm, tn), F32)] if normed else [],
        compiler_params=_params(("parallel", "arbitrary")),
        name=name,
    )(a, *ws, *extras, *([row_ssq] if normed else []))


def _res_mm_kernel(a_ref, w_ref, r_ref, g_ref, o_ref, og_ref, ssq_ref, *, nk):
    def sub_blocks(base_ref, last):
        for s in range(a_ref.shape[0] // MM_SUB_ROWS):
            rows = slice(s * MM_SUB_ROWS, (s + 1) * MM_SUB_ROWS)
            x = base_ref[rows, :] + jnp.dot(a_ref[rows, :], w_ref[...],
                                            preferred_element_type=F32)
            o_ref[rows, :] = x
            if last:
                og_ref[rows, :] = (x * g_ref[...]).astype(og_ref.dtype)
                ssq_ref[rows, :] = functools.reduce(lambda u, v: u + v, _lane_tiles(x * x))

    if nk == 1:
        sub_blocks(r_ref, True)
    else:
        @pl.when(pl.program_id(2) == 0)
        def _():
            sub_blocks(r_ref, False)

        @pl.when(pl.program_id(2) == 1)
        def _():
            sub_blocks(o_ref, True)


def _res_matmul(a, w, res, g_next, tm, tn, tk, name):
    m, k = a.shape
    n = w.shape[1]
    nk = k // tk
    assert nk in (1, 2) and nk * tk == k
    return pl.pallas_call(
        functools.partial(_res_mm_kernel, nk=nk),
        grid=(m // tm, n // tn, nk),
        in_specs=[pl.BlockSpec((tm, tk), lambda i, j, kk: (i, kk)),
                  pl.BlockSpec((tk, tn), lambda i, j, kk: (kk, j)),
                  pl.BlockSpec((tm, tn), lambda i, j, kk: (i, j)),
                  pl.BlockSpec((1, tn), lambda i, j, kk: (0, j))],
        out_specs=[pl.BlockSpec((tm, tn), lambda i, j, kk: (i, j)),
                   pl.BlockSpec((tm, tn), lambda i, j, kk: (i, j)),
                   pl.BlockSpec((tm, LANES), lambda i, j, kk: (i, j))],
        out_shape=[jax.ShapeDtypeStruct((m, n), F32),
                   jax.ShapeDtypeStruct((m, n), BF16),
                   jax.ShapeDtypeStruct((m, (n // tn) * LANES), F32)],
        compiler_params=_params(("parallel", "arbitrary", "arbitrary")),
        name=name,
    )(a, w, res, g_next.reshape(1, n))


def _headproj_kernel(*refs, heads, norm, post_scale):
    a_ref, w_ref = refs[0], refs[1]
    o_ref = refs[-1]
    acc = jnp.dot(a_ref[...], w_ref[...], preferred_element_type=F32)
    for hh in range(heads):
        blk = acc[:, hh * LANES:(hh + 1) * LANES]
        if norm:
            ms = jnp.mean(blk * blk, axis=-1, keepdims=True)
            blk = blk * lax.rsqrt(ms + EPS) * refs[2][...]
        if post_scale is not None:
            blk = blk * post_scale
        o_ref[hh] = blk.astype(o_ref.dtype)


def _headproj(h, w, g, name, post_scale=None, tm=1024, tn=1024):
    m, k = h.shape
    n_heads = w.shape[1] // LANES
    heads = tn // LANES
    norm = g is not None
    extra_specs = [pl.BlockSpec((1, LANES), lambda i, j: (0, 0))] if norm else []
    extra_args = [g.reshape(1, LANES)] if norm else []
    return pl.pallas_call(
        functools.partial(_headproj_kernel, heads=heads, norm=norm, post_scale=post_scale),
        grid=(m // tm, w.shape[1] // tn),
        in_specs=[pl.BlockSpec((tm, k), lambda i, j: (i, 0)),
                  pl.BlockSpec((k, tn), lambda i, j: (0, j))] + extra_specs,
        out_specs=pl.BlockSpec((heads, tm, LANES), lambda i, j: (j, i, 0)),
        out_shape=jax.ShapeDtypeStruct((n_heads, m, LANES), BF16),
        compiler_params=_params(("parallel", "arbitrary")),
        name=name,
    )(h, w, *extra_args)


def _small_kernel(x_ref, gx_ref, w_ref, g_ref, h_ref, k_ref, v_ref, ki_ref, wi_ref):
    def norm_rows(r, carry):
        rows = pl.ds(pl.multiple_of(r * NORM_ROWS, NORM_ROWS), NORM_ROWS)
        x = x_ref[rows, :]
        ms = jnp.mean(x * x, axis=-1, keepdims=True)
        h_ref[rows, :] = (x * lax.rsqrt(ms + EPS) * gx_ref[...]).astype(h_ref.dtype)
        return carry

    lax.fori_loop(0, x_ref.shape[0] // NORM_ROWS, norm_rows, 0)
    acc = jnp.dot(h_ref[...], w_ref[...], preferred_element_type=F32)
    kk = acc[:, 0:LANES]
    ms = jnp.mean(kk * kk, axis=-1, keepdims=True)
    k_ref[...] = (kk * lax.rsqrt(ms + EPS) * g_ref[...]).astype(k_ref.dtype)
    v_ref[...] = acc[:, LANES:2 * LANES].astype(v_ref.dtype)
    ki_ref[...] = acc[:, 2 * LANES:3 * LANES].astype(ki_ref.dtype)
    wi_ref[...] = acc[:, 3 * LANES:4 * LANES]


def _norm_small_proj(x, gx, w_small, g, tm=512):
    m, k = x.shape
    row = lambda i: (i, 0)
    return pl.pallas_call(
        _small_kernel,
        grid=(m // tm,),
        in_specs=[pl.BlockSpec((tm, k), row),
                  pl.BlockSpec((1, k), lambda i: (0, 0)),
                  pl.BlockSpec((k, 4 * LANES), lambda i: (0, 0)),
                  pl.BlockSpec((1, LANES), lambda i: (0, 0))],
        out_specs=[pl.BlockSpec((tm, k), row)] + [pl.BlockSpec((tm, LANES), row)] * 4,
        out_shape=[jax.ShapeDtypeStruct((m, k), BF16)]
                  + [jax.ShapeDtypeStruct((m, LANES), BF16)] * 3
                  + [jax.ShapeDtypeStruct((m, LANES), F32)],
        compiler_params=_params(("parallel",)),
        name="norm_kv_idx_proj",
    )(x, gx.reshape(1, k), w_small, g.reshape(1, LANES))


def _attn_kernel(qi_ref, kidx_ref, wt_ref, q_ref, k_ref, vt_ref, near_ref, cvec_ref, eye_ref, o_ref,
                 key_ref, m_ref, acc_ref, qa_ref, s_ref, lim_ref,
                 *, k_sel, idx_scale, pos_bits):
    j = pl.program_id(1)
    nt = (((1,), (1,)), ((), ()))
    n_chunks = (j + 4) // 4

    key_row = lax.broadcasted_iota(I32, (SCORE_CHUNK, Q_BLOCK), 0)
    q_col = lax.broadcasted_iota(I32, (SCORE_CHUNK, Q_BLOCK), 1)

    def score_chunk(c, carry):
        start = pl.multiple_of(c * SCORE_CHUNK, SCORE_CHUNK)
        kc = kidx_ref[pl.ds(start, SCORE_CHUNK), :]
        acc = jnp.zeros((SCORE_CHUNK, Q_BLOCK), F32)
        for hp in range(N_IDX_HEADS // 2):
            qpair = qi_ref[2 * hp:2 * hp + 2].reshape(2 * Q_BLOCK, IDX_DIM)
            d = lax.dot_general(kc, qpair, nt, preferred_element_type=F32)
            acc = acc + jnp.maximum(d[:, :Q_BLOCK], 0.0) * wt_ref[2 * hp:2 * hp + 1, :]
            acc = acc + jnp.maximum(d[:, Q_BLOCK:], 0.0) * wt_ref[2 * hp + 1:2 * hp + 2, :]
        s = acc * idx_scale
        causal = (start + key_row) <= (j * Q_BLOCK + q_col)
        s = jnp.where(causal, s, -jnp.inf)
        bits = lax.bitcast_convert_type(s, I32)
        key_ref[pl.ds(start, SCORE_CHUNK), :] = bits ^ ((bits >> 31) & 0x7FFFFFFF)
        return carry

    lax.fori_loop(0, n_chunks, score_chunk, 0)

    n_count = n_chunks
    count_row = lax.broadcasted_iota(I32, (COUNT_CHUNK, Q_BLOCK), 0)

    def count_keys(pred):
        def body(c, a):
            start = pl.multiple_of(c * COUNT_CHUNK, COUNT_CHUNK)
            hit = jnp.where(pred(key_ref[pl.ds(start, COUNT_CHUNK), :], start + count_row), 1, 0)
            return a + jnp.sum(hit.reshape(COUNT_CHUNK // SUBLANES, SUBLANES, Q_BLOCK), axis=0)

        a = lax.fori_loop(0, n_count, body, jnp.zeros((SUBLANES, Q_BLOCK), I32))
        return jnp.sum(a, axis=0, keepdims=True)

    def bit_body(i, carry):
        tu, cnt_tu = carry
        cu = tu | jnp.left_shift(jnp.int32(1), 31 - i)
        cs = cu ^ INT_MIN
        cnt = count_keys(lambda keys, pos: keys >= cs)
        take = cnt >= k_sel
        return jnp.where(take, cu, tu), jnp.where(take, cnt, cnt_tu)

    tu, cnt_ge = lax.fori_loop(
        0, 32, bit_body,
        (jnp.zeros((1, Q_BLOCK), I32), jnp.full((1, Q_BLOCK), n_count * COUNT_CHUNK, I32)))
    thr = tu ^ INT_MIN

    tied = (cnt_ge > k_sel) & (thr > KEY_NEG_INF)
    lim_ref[...] = jnp.full(lim_ref.shape, INT_MAX, I32)

    @pl.when(jnp.max(jnp.where(tied, 1, 0)) > 0)
    def _():
        n_tied_kept = k_sel - count_keys(lambda keys, pos: keys > thr)

        def pos_body(i, x):
            cand = x | jnp.left_shift(jnp.int32(1), pos_bits - 1 - i)
            cnt = count_keys(lambda keys, pos: (keys == thr) & (pos < cand))
            return jnp.where(cnt <= n_tied_kept, cand, x)

        x = lax.fori_loop(0, pos_bits, pos_body, jnp.zeros((1, Q_BLOCK), I32))
        lim_ref[...] = jnp.where(tied, x, INT_MAX)

    pos_lim = lim_ref[...]

    def selected(keys, pos):
        return (keys > thr) | ((keys == thr) & (pos < pos_lim))

    m_ref[...] = jnp.full(m_ref.shape, NEG, F32)
    acc_ref[...] = jnp.zeros(acc_ref.shape, F32)
    qa_ref[:, :HEAD_DIM] = q_ref[...].reshape(N_HEADS * Q_BLOCK, HEAD_DIM)
    qa_ref[:, HEAD_DIM:] = eye_ref[...]

    def softmax_step(h, t):
        hs = slice(h * Q_BLOCK, (h + 1) * Q_BLOCK)
        m_old = m_ref[:, hs]
        m_new = jnp.maximum(m_old, jnp.max(t, axis=0, keepdims=True))
        m_ref[:, hs] = m_new
        return jnp.exp2(t - m_new).astype(BF16), jnp.exp2(m_old - m_new)

    def attend(keys, vt, logits_of):
        groups = [slice(g * HEAD_GROUP * Q_BLOCK, (g + 1) * HEAD_GROUP * Q_BLOCK)
                  for g in range(N_HEADS // HEAD_GROUP)]
        n_keys = keys.shape[0]
        for g, gs in enumerate(groups):
            s_ref[g, 0:n_keys, :] = lax.dot_general(keys, qa_ref[gs, 0:keys.shape[1]], nt,
                                                    preferred_element_type=F32)
        for g, gs in enumerate(groups):
            pa = [softmax_step(g * HEAD_GROUP + e,
                               logits_of(g * HEAD_GROUP + e,
                                         s_ref[g, 0:n_keys, e * Q_BLOCK:(e + 1) * Q_BLOCK]))
                  for e in range(HEAD_GROUP)]
            p = jnp.concatenate([x[0] for x in pa], axis=1)
            alpha = jnp.concatenate([x[1] for x in pa], axis=1)
            pv = jnp.dot(vt, p, preferred_element_type=F32)
            acc_ref[:, gs] = acc_ref[:, gs] * alpha + pv

    far_end = (j - 1) * Q_BLOCK
    far_row = lax.broadcasted_iota(I32, (FAR_CHUNK, Q_BLOCK), 0)

    def far_body(c, carry):
        lo = c * FAR_CHUNK
        start = pl.multiple_of(jnp.minimum(lo, jnp.maximum(far_end - FAR_CHUNK, 0)), LANES)
        pos = start + far_row
        sel = selected(key_ref[pl.ds(start, FAR_CHUNK), :], pos) & (pos >= lo) & (pos < far_end)
        maskb = jnp.where(sel, 0.0, NEG).astype(BF16)
        kaug = jnp.concatenate([k_ref[pl.ds(start, FAR_CHUNK), :], maskb], axis=1)
        vt = vt_ref[:, pl.ds(start, FAR_CHUNK)]
        attend(kaug, vt, lambda h, s: s)
        return carry

    lax.fori_loop(0, (jnp.maximum(far_end, 0) + FAR_CHUNK - 1) // FAR_CHUNK, far_body, 0)
    m_ref[...] = m_ref[...] + cvec_ref[...]

    near_start = pl.multiple_of(jnp.maximum(j - 1, 0) * Q_BLOCK, LANES)
    first = jnp.where(j == 0, 1, 0)
    near_pos = near_start + lax.broadcasted_iota(I32, (NEAR_KEYS, Q_BLOCK), 0)
    sel = selected(key_ref[pl.ds(near_start, NEAR_KEYS), :], near_pos)
    maskb = jnp.where(sel, 0.0, NEG)
    k_near = k_ref[pl.ds(near_start, NEAR_KEYS), :]
    vt_near = vt_ref[:, pl.ds(near_start, NEAR_KEYS)]

    def near_logits(h, s):
        return s + near_ref[first, :, h * Q_BLOCK:(h + 1) * Q_BLOCK] + maskb

    attend(k_near, vt_near, near_logits)

    inv_l = 1.0 / acc_ref[HEAD_DIM:HEAD_DIM + 1, :]
    for h in range(N_HEADS):
        hs = slice(h * Q_BLOCK, (h + 1) * Q_BLOCK)
        o_ref[:, h * HEAD_DIM:(h + 1) * HEAD_DIM] = (
            acc_ref[0:HEAD_DIM, hs] * inv_l[:, hs]).T.astype(o_ref.dtype)


def _t5_bucket_static(n):
    max_exact = N_BUCKETS // 2
    nf = np.maximum(n, 1).astype(np.float32)
    large = max_exact + (np.log(nf / np.float32(max_exact)) / np.float32(math.log(MAX_DISTANCE / max_exact))
                         * np.float32(N_BUCKETS - max_exact)).astype(np.int32)
    large = np.minimum(large, N_BUCKETS - 1)
    return np.where(n < max_exact, n, large)


def _bias_tables(rel_bias, seq):
    buckets = _t5_bucket_static(np.arange(seq, dtype=np.int32))
    far = buckets[Q_BLOCK + 1:]
    assert (far == far[0]).all()
    bias2 = rel_bias.astype(F32) * LOG2E
    span, origin = 4 * NEAR_KEYS, 2 * NEAR_KEYS
    d = np.arange(span) - origin
    r = jnp.where(jnp.asarray(d >= 0)[None, :], bias2[buckets[np.clip(d, 0, seq - 1)]].T, NEG)
    y = jnp.tile(r, (1, NEAR_KEYS))[:, :NEAR_KEYS * (span - 1)].reshape(N_HEADS, NEAR_KEYS, span - 1)

    def table(off):
        t = y[:, :, origin + off:origin + off + Q_BLOCK]
        return jnp.transpose(t, (1, 0, 2)).reshape(NEAR_KEYS, N_HEADS * Q_BLOCK)

    near = jnp.stack([table(Q_BLOCK), table(0)])
    cvec = jnp.repeat(bias2[int(far[0])], Q_BLOCK).reshape(1, N_HEADS * Q_BLOCK)
    return near, cvec


def _attention(qi_hm, kidx, w_t, q_hm, k, v_t, near, cvec, batch, seq):
    m = batch * seq
    n_blk = seq // Q_BLOCK
    k_sel = min(TOPK_MAX, seq // 4)
    hq = N_HEADS * Q_BLOCK
    kern = functools.partial(
        _attn_kernel, k_sel=k_sel, idx_scale=IDX_DIM ** -0.5 * N_IDX_HEADS ** -0.5,
        pos_bits=seq.bit_length())
    blk = lambda b, j: (b * n_blk + j)
    v_t = jnp.concatenate([v_t, jnp.ones((VT_ROWS - HEAD_DIM, m), v_t.dtype)], axis=0)
    assert seq % COUNT_CHUNK == 0 and seq % FAR_CHUNK == 0
    eye = jnp.tile(jnp.eye(Q_BLOCK, dtype=BF16), (N_HEADS, 1))
    return pl.pallas_call(
        kern,
        grid=(batch, n_blk),
        in_specs=[pl.BlockSpec((N_IDX_HEADS, Q_BLOCK, IDX_DIM), lambda b, j: (0, blk(b, j), 0)),
                  pl.BlockSpec((seq, IDX_DIM), lambda b, j: (b, 0)),
                  pl.BlockSpec((N_IDX_HEADS, Q_BLOCK), lambda b, j: (0, blk(b, j))),
                  pl.BlockSpec((N_HEADS, Q_BLOCK, HEAD_DIM), lambda b, j: (0, blk(b, j), 0)),
                  pl.BlockSpec((seq, HEAD_DIM), lambda b, j: (b, 0)),
                  pl.BlockSpec((VT_ROWS, seq), lambda b, j: (0, b)),
                  pl.BlockSpec((2, NEAR_KEYS, hq), lambda b, j: (0, 0, 0)),
                  pl.BlockSpec((1, hq), lambda b, j: (0, 0)),
                  pl.BlockSpec((hq, Q_BLOCK), lambda b, j: (0, 0))],
        out_specs=pl.BlockSpec((Q_BLOCK, N_HEADS * HEAD_DIM), lambda b, j: (blk(b, j), 0)),
        out_shape=jax.ShapeDtypeStruct((m, N_HEADS * HEAD_DIM), BF16),
        scratch_shapes=[pltpu.VMEM((seq, Q_BLOCK), I32),
                        pltpu.VMEM((1, hq), F32),
                        pltpu.VMEM((VT_ROWS, hq), F32),
                        pltpu.VMEM((hq, HEAD_DIM + Q_BLOCK), BF16),
                        pltpu.VMEM((N_HEADS // HEAD_GROUP, FAR_CHUNK, HEAD_GROUP * Q_BLOCK), F32),
                        pltpu.VMEM((1, Q_BLOCK), I32)],
        compiler_params=_params(("parallel", "arbitrary")),
        name="dsa_attention",
    )(qi_hm, kidx, w_t, q_hm, k, v_t, near, cvec, eye)


def _sgu_kernel(uv_ref, g_ref, b_ref, ws_ref, bs_ref, o_ref):
    uv = jax.nn.gelu(uv_ref[...])
    u = uv[:, :GMLP_WIDTH]
    v = uv[:, GMLP_WIDTH:]
    mu = jnp.mean(v, axis=-1, keepdims=True)
    vc = v - mu
    var = jnp.mean(vc * vc, axis=-1, keepdims=True)
    vn = (vc * lax.rsqrt(var + EPS) * g_ref[...] + b_ref[...]).astype(BF16)
    r = lax.broadcasted_iota(I32, (CHUNK, CHUNK), 0)
    c = lax.broadcasted_iota(I32, (CHUNK, CHUNK), 1)
    tril = c <= r
    bs = bs_ref[...]
    for g in range(N_GROUPS):
        ws = jnp.where(tril, ws_ref[g], 0.0).astype(BF16)
        mixed = jnp.dot(ws, vn[:, g * GROUP_DIM:(g + 1) * GROUP_DIM],
                        preferred_element_type=F32) + bs[:, g:g + 1]
        o_ref[:, g * GROUP_DIM:(g + 1) * GROUP_DIM] = (
            u[:, g * GROUP_DIM:(g + 1) * GROUP_DIM] * mixed).astype(o_ref.dtype)


def _sgu(uv, ln_g, ln_b, w_s, b_s):
    m = uv.shape[0]
    bs_t = jnp.zeros((CHUNK, LANES), F32).at[:, :N_GROUPS].set(jnp.transpose(b_s))
    return pl.pallas_call(
        _sgu_kernel,
        grid=(m // CHUNK,),
        in_specs=[pl.BlockSpec((CHUNK, 2 * GMLP_WIDTH), lambda i: (i, 0)),
                  pl.BlockSpec((1, GMLP_WIDTH), lambda i: (0, 0)),
                  pl.BlockSpec((1, GMLP_WIDTH), lambda i: (0, 0)),
                  pl.BlockSpec((N_GROUPS, CHUNK, CHUNK), lambda i: (0, 0, 0)),
                  pl.BlockSpec((CHUNK, LANES), lambda i: (0, 0))],
        out_specs=pl.BlockSpec((CHUNK, GMLP_WIDTH), lambda i: (i, 0)),
        out_shape=jax.ShapeDtypeStruct((m, GMLP_WIDTH), BF16),
        compiler_params=_params(("parallel",)),
        name="chunked_sgu",
    )(uv, ln_g.reshape(1, -1), ln_b.reshape(1, -1), w_s, bs_t)


def _merge_kernel(h_ref, ya_ref, yb_ref, wga_ref, wgb_ref, wa_ref, wb_ref, o_ref):
    h = h_ref[...]
    ga = jax.nn.sigmoid(jnp.dot(h, wga_ref[...], preferred_element_type=F32))
    a = jnp.dot(ya_ref[...], wa_ref[...], preferred_element_type=F32)
    out = ga * a
    gb = jax.nn.sigmoid(jnp.dot(h, wgb_ref[...], preferred_element_type=F32))
    b = jnp.dot(yb_ref[...], wb_ref[...], preferred_element_type=F32)
    o_ref[...] = (out + gb * b).astype(o_ref.dtype)


def _merge(h, ya, yb, wga, wgb, wa, wb, tm=512, tn=512):
    m, d = h.shape
    n = wga.shape[1]
    ka, kb = ya.shape[1], yb.shape[1]
    row = lambda i, j: (i, 0)
    colw = lambda i, j: (0, j)
    return pl.pallas_call(
        _merge_kernel,
        grid=(m // tm, n // tn),
        in_specs=[pl.BlockSpec((tm, d), row), pl.BlockSpec((tm, ka), row), pl.BlockSpec((tm, kb), row),
                  pl.BlockSpec((d, tn), colw), pl.BlockSpec((d, tn), colw),
                  pl.BlockSpec((ka, tn), colw), pl.BlockSpec((kb, tn), colw)],
        out_specs=pl.BlockSpec((tm, tn), lambda i, j: (i, j)),
        out_shape=jax.ShapeDtypeStruct((m, n), BF16),
        compiler_params=_params(("parallel", "arbitrary")),
        name="branch_merge",
    )(h, ya, yb, wga, wgb, wa, wb)


def _ple_kernel(p_ref, w_ref, g_ref, o_ref):
    acc = jnp.dot(p_ref[...].astype(BF16), w_ref[...], preferred_element_type=F32)
    ms = jnp.mean(acc * acc, axis=-1, keepdims=True)
    o_ref[...] = (acc * lax.rsqrt(ms + EPS) * g_ref[...]).astype(o_ref.dtype)


def _ple(p, w, g, tm=256):
    m, dp = p.shape
    d = w.shape[1]
    return pl.pallas_call(
        _ple_kernel,
        grid=(m // tm,),
        in_specs=[pl.BlockSpec((tm, dp), lambda i: (i, 0)),
                  pl.BlockSpec((dp, d), lambda i: (0, 0)),
                  pl.BlockSpec((1, d), lambda i: (0, 0))],
        out_specs=pl.BlockSpec((tm, d), lambda i: (i, 0)),
        out_shape=jax.ShapeDtypeStruct((m, d), F32),
        compiler_params=_params(("parallel",)),
        name="ple_embed",
    )(p, w, g.reshape(1, d))


def kernel(x, p, w_in, q_norm_g, k_norm_g, rel_bias, sgu_ln_g, sgu_ln_b, sgu_w, sgu_b, w_branch_a, w_branch_b, w_out, norm_mix_g, norm_ffn_g, w_gate_ffn, w_up_ffn, w_down_ffn, w_ple, ple_norm_g, w_ple_gate, norm_ple_g):
    batch, seq, d_model = x.shape
    depth = w_in.shape[0]
    m = batch * seq
    a_width = N_HEADS * HEAD_DIM
    sizes = (a_width, HEAD_DIM, HEAD_DIM, N_IDX_HEADS * IDX_DIM, IDX_DIM, N_IDX_HEADS,
             2 * GMLP_WIDTH, d_model, d_model)
    offs = np.concatenate([[0], np.cumsum(sizes)])
    near, cvec = _bias_tables(rel_bias, seq)

    xf = x.reshape(m, d_model)
    for i in range(depth):
        seg = lambda s: w_in[i][:, int(offs[s]):int(offs[s + 1])].astype(BF16)
        w_q, w_k, w_v, w_qi, w_ki, w_wi, w_uv, w_ga, w_gb = [seg(s) for s in range(9)]
        w_small = jnp.concatenate(
            [w_k, w_v, w_ki, jnp.pad(w_wi, ((0, 0), (0, LANES - N_IDX_HEADS)))], axis=1)

        h, k_, v_, kidx, widx = _norm_small_proj(xf, norm_mix_g[i], w_small, k_norm_g[i])
        q_hm = _headproj(h, w_q, q_norm_g[i], "q_proj", post_scale=HEAD_DIM ** -0.5 * LOG2E)
        qi_hm = _headproj(h, w_qi, None, "q_idx_proj")
        uv = _matmul(h, [w_uv], [], lambda a, e: a[0], F32, 1024, 1024, "uv_proj")
        y_a = _attention(qi_hm, kidx, jnp.transpose(widx[:, :N_IDX_HEADS]), q_hm, k_,
                         jnp.transpose(v_), near, cvec, batch, seq)
        y_b = _sgu(uv, sgu_ln_g[i], sgu_ln_b[i], sgu_w[i], sgu_b[i])
        merged = _merge(h, y_a, y_b, w_ga, w_gb,
                        w_branch_a[i].astype(BF16), w_branch_b[i].astype(BF16))
        x1 = _matmul(merged, [w_out[i].astype(BF16)], [xf],
                     lambda a, e: e[0] + a[0], F32, 1024, 512, "out_proj")

        h2 = _rmsnorm(x1, norm_ffn_g[i])
        t = _matmul(h2, [w_gate_ffn[i].astype(BF16), w_up_ffn[i].astype(BF16)], [],
                    lambda a, e: jax.nn.silu(a[0]) * a[1], BF16, 1024, 256, "ffn_gate_up")
        d_ff = t.shape[1]
        x2, x2g, ssq2 = _res_matmul(t, w_down_ffn[i].astype(BF16), x1, norm_ple_g[i],
                                    512, 512, d_ff, "ffn_down")

        pe = _ple(p[i].reshape(m, -1), w_ple[i].astype(BF16), ple_norm_g[i])
        xf = _matmul(x2g, [w_ple_gate[i].astype(BF16)], [x2, pe],
                     lambda a, e: e[0] + jax.nn.sigmoid(a[0]) * e[1], F32, 1024, 512, "ple_gate",
                     row_ssq=ssq2)
    return xf.reshape(batch, seq, d_model)
```

```python
import functools
import math

import numpy as np
import jax
import jax.numpy as jnp
from jax import lax
from jax.experimental import pallas as pl
from jax.experimental.pallas import tpu as pltpu

F32 = jnp.float32
BF16 = jnp.bfloat16
I32 = jnp.int32

N_HEADS = 16
HEAD_DIM = 128
N_IDX_HEADS = 32
IDX_DIM = 128
TOPK_MAX = 256
Q_BLOCK = 128
N_BUCKETS = 32
MAX_DISTANCE = 128
GMLP_WIDTH = 2048
N_GROUPS = 8
GROUP_DIM = GMLP_WIDTH // N_GROUPS
CHUNK = 128
EPS = 1e-6

LANES = 128
SUBLANES = 8
VMEM_LIMIT = 56 * 1024 * 1024
NEG = -(2.0 ** 100)
INT_MIN = -2 ** 31
INT_MAX = 2 ** 31 - 1
KEY_NEG_INF = -2139095041
LOG2E = math.log2(math.e)

SCORE_CHUNK = 4 * LANES
COUNT_CHUNK = SCORE_CHUNK
FAR_CHUNK = 4 * LANES
NEAR_KEYS = 2 * Q_BLOCK
HEAD_GROUP = 4
VT_ROWS = HEAD_DIM + 16
NORM_ROWS = 128
MM_SUB_ROWS = 256

TILES = {
    "norm_kv_idx_proj": (512, 4 * LANES),
    "q_proj": (1024, 1024),
    "q_idx_proj": (1024, 1024),
    "uv_proj": (1024, 1024),
    "branch_merge": (512, 512),
    "out_proj": (1024, 512),
    "ffn_gate_up": (1024, 256),
    "ffn_down": (512, 512),
    "ple_gate": (1024, 512),
}


def _params(sem):
    return pltpu.CompilerParams(dimension_semantics=sem, vmem_limit_bytes=VMEM_LIMIT)


def _rmsnorm_kernel(x_ref, g_ref, o_ref):
    x = x_ref[...]
    ms = jnp.mean(x * x, axis=-1, keepdims=True)
    o_ref[...] = (x * lax.rsqrt(ms + EPS) * g_ref[...]).astype(o_ref.dtype)


def _rmsnorm(x, g, tm=256):
    m, d = x.shape
    return pl.pallas_call(
        _rmsnorm_kernel,
        grid=(m // tm,),
        in_specs=[pl.BlockSpec((tm, d), lambda i: (i, 0)),
                  pl.BlockSpec((1, d), lambda i: (0, 0))],
        out_specs=pl.BlockSpec((tm, d), lambda i: (i, 0)),
        out_shape=jax.ShapeDtypeStruct((m, d), BF16),
        compiler_params=_params(("parallel",)),
        name="rmsnorm",
    )(x, g.reshape(1, d))


def _lane_tiles(x):
    return [x[:, c * LANES:(c + 1) * LANES] for c in range(x.shape[1] // LANES)]


def _mm_kernel(*refs, n_w, n_x, epi, norm_dim):
    a_ref = refs[0]
    w_refs = refs[1:1 + n_w]
    x_refs = refs[1 + n_w:1 + n_w + n_x]
    if norm_dim:
        ssq_ref, o_ref, r_ref = refs[1 + n_w + n_x:]

        @pl.when(pl.program_id(1) == 0)
        def _():
            ms = jnp.sum(ssq_ref[...], axis=-1, keepdims=True) * (1.0 / norm_dim)
            r_ref[...] = jnp.broadcast_to(lax.rsqrt(ms + EPS), r_ref.shape)
    else:
        o_ref = refs[1 + n_w + n_x]
    for s in range(a_ref.shape[0] // MM_SUB_ROWS):
        rows = slice(s * MM_SUB_ROWS, (s + 1) * MM_SUB_ROWS)
        a = a_ref[rows, :]
        accs = [jnp.dot(a, w[...], preferred_element_type=F32) for w in w_refs]
        if norm_dim:
            accs = [acc * r_ref[rows, :] for acc in accs]
        o_ref[rows, :] = epi(accs, [x[rows, :] for x in x_refs]).astype(o_ref.dtype)


def _matmul(a, ws, extras, epi, out_dtype, name, row_ssq=None):
    tm, tn = TILES[name]
    m, k = a.shape
    n = ws[0].shape[1]
    normed = row_ssq is not None
    kern = functools.partial(_mm_kernel, n_w=len(ws), n_x=len(extras), epi=epi,
                             norm_dim=k if normed else 0)
    return pl.pallas_call(
        kern,
        grid=(m // tm, n // tn),
        in_specs=([pl.BlockSpec((tm, k), lambda i, j: (i, 0))]
                  + [pl.BlockSpec((k, tn), lambda i, j: (0, j)) for _ in ws]
                  + [pl.BlockSpec((tm, tn), lambda i, j: (i, j)) for _ in extras]
                  + ([pl.BlockSpec((tm, row_ssq.shape[1]), lambda i, j: (i, 0))] if normed else [])),
        out_specs=pl.BlockSpec((tm, tn), lambda i, j: (i, j)),
        out_shape=jax.ShapeDtypeStruct((m, n), out_dtype),
        scratch_shapes=[pltpu.VMEM((tm, tn), F32)] if normed else [],
        compiler_params=_params(("parallel", "arbitrary")),
        name=name,
    )(a, *ws, *extras, *([row_ssq] if normed else []))


def _res_mm_kernel(a_ref, w_ref, r_ref, g_ref, o_ref, og_ref, ssq_ref, *, nk):
    def sub_blocks(base_ref, last):
        for s in range(a_ref.shape[0] // MM_SUB_ROWS):
            rows = slice(s * MM_SUB_ROWS, (s + 1) * MM_SUB_ROWS)
            x = base_ref[rows, :] + jnp.dot(a_ref[rows, :], w_ref[...],
                                            preferred_element_type=F32)
            o_ref[rows, :] = x
            if last:
                og_ref[rows, :] = (x * g_ref[...]).astype(og_ref.dtype)
                ssq_ref[rows, :] = functools.reduce(lambda u, v: u + v, _lane_tiles(x * x))

    if nk == 1:
        sub_blocks(r_ref, True)
    else:
        @pl.when(pl.program_id(2) == 0)
        def _():
            sub_blocks(r_ref, False)

        @pl.when(pl.program_id(2) == 1)
        def _():
            sub_blocks(o_ref, True)


def _res_matmul(a, w, res, g_next, name, nk=1):
    tm, tn = TILES[name]
    m, k = a.shape
    n = w.shape[1]
    assert nk in (1, 2) and k % nk == 0
    tk = k // nk
    return pl.pallas_call(
        functools.partial(_res_mm_kernel, nk=nk),
        grid=(m // tm, n // tn, nk),
        in_specs=[pl.BlockSpec((tm, tk), lambda i, j, kk: (i, kk)),
                  pl.BlockSpec((tk, tn), lambda i, j, kk: (kk, j)),
                  pl.BlockSpec((tm, tn), lambda i, j, kk: (i, j)),
                  pl.BlockSpec((1, tn), lambda i, j, kk: (0, j))],
        out_specs=[pl.BlockSpec((tm, tn), lambda i, j, kk: (i, j)),
                   pl.BlockSpec((tm, tn), lambda i, j, kk: (i, j)),
                   pl.BlockSpec((tm, LANES), lambda i, j, kk: (i, j))],
        out_shape=[jax.ShapeDtypeStruct((m, n), F32),
                   jax.ShapeDtypeStruct((m, n), BF16),
                   jax.ShapeDtypeStruct((m, (n // tn) * LANES), F32)],
        compiler_params=_params(("parallel", "arbitrary", "arbitrary")),
        name=name,
    )(a, w, res, g_next.reshape(1, n))


def _headproj_kernel(*refs, heads, norm, post_scale):
    a_ref, w_ref = refs[0], refs[1]
    o_ref = refs[-1]
    acc = jnp.dot(a_ref[...], w_ref[...], preferred_element_type=F32)
    for hh in range(heads):
        blk = acc[:, hh * LANES:(hh + 1) * LANES]
        if norm:
            ms = jnp.mean(blk * blk, axis=-1, keepdims=True)
            blk = blk * lax.rsqrt(ms + EPS) * refs[2][...]
        if post_scale is not None:
            blk = blk * post_scale
        o_ref[hh] = blk.astype(o_ref.dtype)


def _headproj(h, w, g, name, post_scale=None):
    tm, tn = TILES[name]
    m, k = h.shape
    n_heads = w.shape[1] // LANES
    heads = tn // LANES
    norm = g is not None
    extra_specs = [pl.BlockSpec((1, LANES), lambda i, j: (0, 0))] if norm else []
    extra_args = [g.reshape(1, LANES)] if norm else []
    return pl.pallas_call(
        functools.partial(_headproj_kernel, heads=heads, norm=norm, post_scale=post_scale),
        grid=(m // tm, w.shape[1] // tn),
        in_specs=[pl.BlockSpec((tm, k), lambda i, j: (i, 0)),
                  pl.BlockSpec((k, tn), lambda i, j: (0, j))] + extra_specs,
        out_specs=pl.BlockSpec((heads, tm, LANES), lambda i, j: (j, i, 0)),
        out_shape=jax.ShapeDtypeStruct((n_heads, m, LANES), BF16),
        compiler_params=_params(("parallel", "arbitrary")),
        name=name,
    )(h, w, *extra_args)


def _small_kernel(x_ref, gx_ref, w_ref, g_ref, h_ref, k_ref, v_ref, ki_ref, wi_ref):
    def norm_rows(r, carry):
        rows = pl.ds(pl.multiple_of(r * NORM_ROWS, NORM_ROWS), NORM_ROWS)
        x = x_ref[rows, :]
        ms = jnp.mean(x * x, axis=-1, keepdims=True)
        h_ref[rows, :] = (x * lax.rsqrt(ms + EPS) * gx_ref[...]).astype(h_ref.dtype)
        return carry

    lax.fori_loop(0, x_ref.shape[0] // NORM_ROWS, norm_rows, 0)
    acc = jnp.dot(h_ref[...], w_ref[...], preferred_element_type=F32)
    kk = acc[:, 0:LANES]
    ms = jnp.mean(kk * kk, axis=-1, keepdims=True)
    k_ref[...] = (kk * lax.rsqrt(ms + EPS) * g_ref[...]).astype(k_ref.dtype)
    v_ref[...] = acc[:, LANES:2 * LANES].astype(v_ref.dtype)
    ki_ref[...] = acc[:, 2 * LANES:3 * LANES].astype(ki_ref.dtype)
    wi_ref[...] = acc[:, 3 * LANES:4 * LANES]


def _norm_small_proj(x, gx, w_small, g):
    tm, _ = TILES["norm_kv_idx_proj"]
    m, k = x.shape
    row = lambda i: (i, 0)
    return pl.pallas_call(
        _small_kernel,
        grid=(m // tm,),
        in_specs=[pl.BlockSpec((tm, k), row),
                  pl.BlockSpec((1, k), lambda i: (0, 0)),
                  pl.BlockSpec((k, 4 * LANES), lambda i: (0, 0)),
                  pl.BlockSpec((1, LANES), lambda i: (0, 0))],
        out_specs=[pl.BlockSpec((tm, k), row)] + [pl.BlockSpec((tm, LANES), row)] * 4,
        out_shape=[jax.ShapeDtypeStruct((m, k), BF16)]
                  + [jax.ShapeDtypeStruct((m, LANES), BF16)] * 3
                  + [jax.ShapeDtypeStruct((m, LANES), F32)],
        compiler_params=_params(("parallel",)),
        name="norm_kv_idx_proj",
    )(x, gx.reshape(1, k), w_small, g.reshape(1, LANES))


def _attn_kernel(qi_ref, kidx_ref, wt_ref, q_ref, k_ref, vt_ref, near_ref, cvec_ref, eye_ref, o_ref,
                 key_ref, m_ref, acc_ref, qa_ref, s_ref, lim_ref,
                 *, k_sel, idx_scale, pos_bits):
    j = pl.program_id(1)
    nt = (((1,), (1,)), ((), ()))
    n_chunks = (j + 4) // 4

    key_row = lax.broadcasted_iota(I32, (SCORE_CHUNK, Q_BLOCK), 0)
    q_col = lax.broadcasted_iota(I32, (SCORE_CHUNK, Q_BLOCK), 1)

    def score_chunk(c, carry):
        start = pl.multiple_of(c * SCORE_CHUNK, SCORE_CHUNK)
        kc = kidx_ref[pl.ds(start, SCORE_CHUNK), :]
        acc = jnp.zeros((SCORE_CHUNK, Q_BLOCK), F32)
        for hp in range(N_IDX_HEADS // 2):
            qpair = qi_ref[2 * hp:2 * hp + 2].reshape(2 * Q_BLOCK, IDX_DIM)
            d = lax.dot_general(kc, qpair, nt, preferred_element_type=F32)
            acc = acc + jnp.maximum(d[:, :Q_BLOCK], 0.0) * wt_ref[2 * hp:2 * hp + 1, :]
            acc = acc + jnp.maximum(d[:, Q_BLOCK:], 0.0) * wt_ref[2 * hp + 1:2 * hp + 2, :]
        s = acc * idx_scale
        causal = (start + key_row) <= (j * Q_BLOCK + q_col)
        s = jnp.where(causal, s, -jnp.inf)
        bits = lax.bitcast_convert_type(s, I32)
        key_ref[pl.ds(start, SCORE_CHUNK), :] = bits ^ ((bits >> 31) & 0x7FFFFFFF)
        return carry

    lax.fori_loop(0, n_chunks, score_chunk, 0)

    n_count = n_chunks
    count_row = lax.broadcasted_iota(I32, (COUNT_CHUNK, Q_BLOCK), 0)

    def count_keys(pred):
        def body(c, a):
            start = pl.multiple_of(c * COUNT_CHUNK, COUNT_CHUNK)
            hit = jnp.where(pred(key_ref[pl.ds(start, COUNT_CHUNK), :], start + count_row), 1, 0)
            return a + jnp.sum(hit.reshape(COUNT_CHUNK // SUBLANES, SUBLANES, Q_BLOCK), axis=0)

        a = lax.fori_loop(0, n_count, body, jnp.zeros((SUBLANES, Q_BLOCK), I32))
        return jnp.sum(a, axis=0, keepdims=True)

    def bit_body(i, carry):
        tu, cnt_tu = carry
        cu = tu | jnp.left_shift(jnp.int32(1), 31 - i)
        cs = cu ^ INT_MIN
        cnt = count_keys(lambda keys, pos: keys >= cs)
        take = cnt >= k_sel
        return jnp.where(take, cu, tu), jnp.where(take, cnt, cnt_tu)

    tu, cnt_ge = lax.fori_loop(
        0, 32, bit_body,
        (jnp.zeros((1, Q_BLOCK), I32), jnp.full((1, Q_BLOCK), n_count * COUNT_CHUNK, I32)))
    thr = tu ^ INT_MIN

    tied = (cnt_ge > k_sel) & (thr > KEY_NEG_INF)
    lim_ref[...] = jnp.full(lim_ref.shape, INT_MAX, I32)

    @pl.when(jnp.max(jnp.where(tied, 1, 0)) > 0)
    def _():
        n_tied_kept = k_sel - count_keys(lambda keys, pos: keys > thr)

        def pos_body(i, x):
            cand = x | jnp.left_shift(jnp.int32(1), pos_bits - 1 - i)
            cnt = count_keys(lambda keys, pos: (keys == thr) & (pos < cand))
            return jnp.where(cnt <= n_tied_kept, cand, x)

        x = lax.fori_loop(0, pos_bits, pos_body, jnp.zeros((1, Q_BLOCK), I32))
        lim_ref[...] = jnp.where(tied, x, INT_MAX)

    pos_lim = lim_ref[...]

    def selected(keys, pos):
        return (keys > thr) | ((keys == thr) & (pos < pos_lim))

    m_ref[...] = jnp.full(m_ref.shape, NEG, F32)
    acc_ref[...] = jnp.zeros(acc_ref.shape, F32)
    qa_ref[:, :HEAD_DIM] = q_ref[...].reshape(N_HEADS * Q_BLOCK, HEAD_DIM)
    qa_ref[:, HEAD_DIM:] = eye_ref[...]

    def softmax_step(h, t):
        hs = slice(h * Q_BLOCK, (h + 1) * Q_BLOCK)
        m_old = m_ref[:, hs]
        m_new = jnp.maximum(m_old, jnp.max(t, axis=0, keepdims=True))
        m_ref[:, hs] = m_new
        return jnp.exp2(t - m_new).astype(BF16), jnp.exp2(m_old - m_new)

    def attend(keys, vt, logits_of):
        groups = [slice(g * HEAD_GROUP * Q_BLOCK, (g + 1) * HEAD_GROUP * Q_BLOCK)
                  for g in range(N_HEADS // HEAD_GROUP)]
        n_keys = keys.shape[0]
        for g, gs in enumerate(groups):
            s_ref[g, 0:n_keys, :] = lax.dot_general(keys, qa_ref[gs, 0:keys.shape[1]], nt,
                                                    preferred_element_type=F32)
        for g, gs in enumerate(groups):
            pa = [softmax_step(g * HEAD_GROUP + e,
                               logits_of(g * HEAD_GROUP + e,
                                         s_ref[g, 0:n_keys, e * Q_BLOCK:(e + 1) * Q_BLOCK]))
                  for e in range(HEAD_GROUP)]
            p = jnp.concatenate([x[0] for x in pa], axis=1)
            alpha = jnp.concatenate([x[1] for x in pa], axis=1)
            pv = jnp.dot(vt, p, preferred_element_type=F32)
            acc_ref[:, gs] = acc_ref[:, gs] * alpha + pv

    far_end = (j - 1) * Q_BLOCK
    far_row = lax.broadcasted_iota(I32, (FAR_CHUNK, Q_BLOCK), 0)

    def far_body(c, carry):
        lo = c * FAR_CHUNK
        start = pl.multiple_of(jnp.minimum(lo, jnp.maximum(far_end - FAR_CHUNK, 0)), LANES)
        pos = start + far_row
        sel = selected(key_ref[pl.ds(start, FAR_CHUNK), :], pos) & (pos >= lo) & (pos < far_end)
        maskb = jnp.where(sel, 0.0, NEG).astype(BF16)
        kaug = jnp.concatenate([k_ref[pl.ds(start, FAR_CHUNK), :], maskb], axis=1)
        vt = vt_ref[:, pl.ds(start, FAR_CHUNK)]
        attend(kaug, vt, lambda h, s: s)
        return carry

    lax.fori_loop(0, (jnp.maximum(far_end, 0) + FAR_CHUNK - 1) // FAR_CHUNK, far_body, 0)
    m_ref[...] = m_ref[...] + cvec_ref[...]

    near_start = pl.multiple_of(jnp.maximum(j - 1, 0) * Q_BLOCK, LANES)
    first = jnp.where(j == 0, 1, 0)
    near_pos = near_start + lax.broadcasted_iota(I32, (NEAR_KEYS, Q_BLOCK), 0)
    sel = selected(key_ref[pl.ds(near_start, NEAR_KEYS), :], near_pos)
    maskb = jnp.where(sel, 0.0, NEG)
    k_near = k_ref[pl.ds(near_start, NEAR_KEYS), :]
    vt_near = vt_ref[:, pl.ds(near_start, NEAR_KEYS)]

    def near_logits(h, s):
        return s + near_ref[first, :, h * Q_BLOCK:(h + 1) * Q_BLOCK] + maskb

    attend(k_near, vt_near, near_logits)

    inv_l = 1.0 / acc_ref[HEAD_DIM:HEAD_DIM + 1, :]
    for h in range(N_HEADS):
        hs = slice(h * Q_BLOCK, (h + 1) * Q_BLOCK)
        o_ref[:, h * HEAD_DIM:(h + 1) * HEAD_DIM] = (
            acc_ref[0:HEAD_DIM, hs] * inv_l[:, hs]).T.astype(o_ref.dtype)


def _t5_bucket_static(n):
    max_exact = N_BUCKETS // 2
    nf = np.maximum(n, 1).astype(np.float32)
    large = max_exact + (np.log(nf / np.float32(max_exact)) / np.float32(math.log(MAX_DISTANCE / max_exact))
                         * np.float32(N_BUCKETS - max_exact)).astype(np.int32)
    large = np.minimum(large, N_BUCKETS - 1)
    return np.where(n < max_exact, n, large)


def _bias_tables(rel_bias, seq):
    buckets = _t5_bucket_static(np.arange(seq, dtype=np.int32))
    far = buckets[Q_BLOCK + 1:]
    assert (far == far[0]).all()
    bias2 = rel_bias.astype(F32) * LOG2E
    span, origin = 4 * NEAR_KEYS, 2 * NEAR_KEYS
    d = np.arange(span) - origin
    r = jnp.where(jnp.asarray(d >= 0)[None, :], bias2[buckets[np.clip(d, 0, seq - 1)]].T, NEG)
    y = jnp.tile(r, (1, NEAR_KEYS))[:, :NEAR_KEYS * (span - 1)].reshape(N_HEADS, NEAR_KEYS, span - 1)

    def table(off):
        t = y[:, :, origin + off:origin + off + Q_BLOCK]
        return jnp.transpose(t, (1, 0, 2)).reshape(NEAR_KEYS, N_HEADS * Q_BLOCK)

    near = jnp.stack([table(Q_BLOCK), table(0)])
    cvec = jnp.repeat(bias2[int(far[0])], Q_BLOCK).reshape(1, N_HEADS * Q_BLOCK)
    return near, cvec


def _attention(qi_hm, kidx, w_t, q_hm, k, v_t, near, cvec, batch, seq):
    m = batch * seq
    n_blk = seq // Q_BLOCK
    k_sel = min(TOPK_MAX, seq // 4)
    hq = N_HEADS * Q_BLOCK
    kern = functools.partial(
        _attn_kernel, k_sel=k_sel, idx_scale=IDX_DIM ** -0.5 * N_IDX_HEADS ** -0.5,
        pos_bits=seq.bit_length())
    blk = lambda b, j: (b * n_blk + j)
    v_t = jnp.concatenate([v_t, jnp.ones((VT_ROWS - HEAD_DIM, m), v_t.dtype)], axis=0)
    assert seq % COUNT_CHUNK == 0 and seq % FAR_CHUNK == 0
    eye = jnp.tile(jnp.eye(Q_BLOCK, dtype=BF16), (N_HEADS, 1))
    return pl.pallas_call(
        kern,
        grid=(batch, n_blk),
        in_specs=[pl.BlockSpec((N_IDX_HEADS, Q_BLOCK, IDX_DIM), lambda b, j: (0, blk(b, j), 0)),
                  pl.BlockSpec((seq, IDX_DIM), lambda b, j: (b, 0)),
                  pl.BlockSpec((N_IDX_HEADS, Q_BLOCK), lambda b, j: (0, blk(b, j))),
                  pl.BlockSpec((N_HEADS, Q_BLOCK, HEAD_DIM), lambda b, j: (0, blk(b, j), 0)),
                  pl.BlockSpec((seq, HEAD_DIM), lambda b, j: (b, 0)),
                  pl.BlockSpec((VT_ROWS, seq), lambda b, j: (0, b)),
                  pl.BlockSpec((2, NEAR_KEYS, hq), lambda b, j: (0, 0, 0)),
                  pl.BlockSpec((1, hq), lambda b, j: (0, 0)),
                  pl.BlockSpec((hq, Q_BLOCK), lambda b, j: (0, 0))],
        out_specs=pl.BlockSpec((Q_BLOCK, N_HEADS * HEAD_DIM), lambda b, j: (blk(b, j), 0)),
        out_shape=jax.ShapeDtypeStruct((m, N_HEADS * HEAD_DIM), BF16),
        scratch_shapes=[pltpu.VMEM((seq, Q_BLOCK), I32),
                        pltpu.VMEM((1, hq), F32),
                        pltpu.VMEM((VT_ROWS, hq), F32),
                        pltpu.VMEM((hq, HEAD_DIM + Q_BLOCK), BF16),
                        pltpu.VMEM((N_HEADS // HEAD_GROUP, FAR_CHUNK, HEAD_GROUP * Q_BLOCK), F32),
                        pltpu.VMEM((1, Q_BLOCK), I32)],
        compiler_params=_params(("parallel", "arbitrary")),
        name="dsa_attention",
    )(qi_hm, kidx, w_t, q_hm, k, v_t, near, cvec, eye)


def _gelu_tanh(x):
    c = -2.0 * math.sqrt(2.0 / math.pi) * LOG2E
    return x / (1.0 + jnp.exp2(x * (x * x * (0.044715 * c) + c)))


def _sgu_kernel(uv_ref, g_ref, b_ref, ws_ref, bs_ref, o_ref):
    uv = _gelu_tanh(uv_ref[...])
    u = uv[:, :GMLP_WIDTH]
    v = uv[:, GMLP_WIDTH:]
    mu = jnp.mean(v, axis=-1, keepdims=True)
    vc = v - mu
    var = jnp.mean(vc * vc, axis=-1, keepdims=True)
    vn = (vc * lax.rsqrt(var + EPS) * g_ref[...] + b_ref[...]).astype(BF16)
    r = lax.broadcasted_iota(I32, (CHUNK, CHUNK), 0)
    c = lax.broadcasted_iota(I32, (CHUNK, CHUNK), 1)
    tril = c <= r
    bs = bs_ref[...]
    for g in range(N_GROUPS):
        ws = jnp.where(tril, ws_ref[g], 0.0).astype(BF16)
        mixed = jnp.dot(ws, vn[:, g * GROUP_DIM:(g + 1) * GROUP_DIM],
                        preferred_element_type=F32) + bs[:, g:g + 1]
        o_ref[:, g * GROUP_DIM:(g + 1) * GROUP_DIM] = (
            u[:, g * GROUP_DIM:(g + 1) * GROUP_DIM] * mixed).astype(o_ref.dtype)


def _sgu(uv, ln_g, ln_b, w_s, b_s):
    m = uv.shape[0]
    bs_t = jnp.zeros((CHUNK, LANES), F32).at[:, :N_GROUPS].set(jnp.transpose(b_s))
    return pl.pallas_call(
        _sgu_kernel,
        grid=(m // CHUNK,),
        in_specs=[pl.BlockSpec((CHUNK, 2 * GMLP_WIDTH), lambda i: (i, 0)),
                  pl.BlockSpec((1, GMLP_WIDTH), lambda i: (0, 0)),
                  pl.BlockSpec((1, GMLP_WIDTH), lambda i: (0, 0)),
                  pl.BlockSpec((N_GROUPS, CHUNK, CHUNK), lambda i: (0, 0, 0)),
                  pl.BlockSpec((CHUNK, LANES), lambda i: (0, 0))],
        out_specs=pl.BlockSpec((CHUNK, GMLP_WIDTH), lambda i: (i, 0)),
        out_shape=jax.ShapeDtypeStruct((m, GMLP_WIDTH), BF16),
        compiler_params=_params(("parallel",)),
        name="chunked_sgu",
    )(uv, ln_g.reshape(1, -1), ln_b.reshape(1, -1), w_s, bs_t)


def _merge_kernel(h_ref, ya_ref, yb_ref, wga_ref, wgb_ref, wa_ref, wb_ref, o_ref):
    h = h_ref[...]
    ga = jax.nn.sigmoid(jnp.dot(h, wga_ref[...], preferred_element_type=F32))
    a = jnp.dot(ya_ref[...], wa_ref[...], preferred_element_type=F32)
    out = ga * a
    gb = jax.nn.sigmoid(jnp.dot(h, wgb_ref[...], preferred_element_type=F32))
    b = jnp.dot(yb_ref[...], wb_ref[...], preferred_element_type=F32)
    o_ref[...] = (out + gb * b).astype(o_ref.dtype)


def _merge(h, ya, yb, wga, wgb, wa, wb):
    tm, tn = TILES["branch_merge"]
    m, d = h.shape
    n = wga.shape[1]
    ka, kb = ya.shape[1], yb.shape[1]
    row = lambda i, j: (i, 0)
    colw = lambda i, j: (0, j)
    return pl.pallas_call(
        _merge_kernel,
        grid=(m // tm, n // tn),
        in_specs=[pl.BlockSpec((tm, d), row), pl.BlockSpec((tm, ka), row), pl.BlockSpec((tm, kb), row),
                  pl.BlockSpec((d, tn), colw), pl.BlockSpec((d, tn), colw),
                  pl.BlockSpec((ka, tn), colw), pl.BlockSpec((kb, tn), colw)],
        out_specs=pl.BlockSpec((tm, tn), lambda i, j: (i, j)),
        out_shape=jax.ShapeDtypeStruct((m, n), BF16),
        compiler_params=_params(("parallel", "arbitrary")),
        name="branch_merge",
    )(h, ya, yb, wga, wgb, wa, wb)


def _ple_kernel(p_ref, w_ref, g_ref, o_ref):
    acc = jnp.dot(p_ref[...].astype(BF16), w_ref[...], preferred_element_type=F32)
    ms = jnp.mean(acc * acc, axis=-1, keepdims=True)
    o_ref[...] = (acc * lax.rsqrt(ms + EPS) * g_ref[...]).astype(o_ref.dtype)


def _ple(p, w, g, tm=256):
    m, dp = p.shape
    d = w.shape[1]
    return pl.pallas_call(
        _ple_kernel,
        grid=(m // tm,),
        in_specs=[pl.BlockSpec((tm, dp), lambda i: (i, 0)),
                  pl.BlockSpec((dp, d), lambda i: (0, 0)),
                  pl.BlockSpec((1, d), lambda i: (0, 0))],
        out_specs=pl.BlockSpec((tm, d), lambda i: (i, 0)),
        out_shape=jax.ShapeDtypeStruct((m, d), F32),
        compiler_params=_params(("parallel",)),
        name="ple_embed",
    )(p, w, g.reshape(1, d))


def kernel(x, p, w_in, q_norm_g, k_norm_g, rel_bias, sgu_ln_g, sgu_ln_b, sgu_w, sgu_b, w_branch_a, w_branch_b, w_out, norm_mix_g, norm_ffn_g, w_gate_ffn, w_up_ffn, w_down_ffn, w_ple, ple_norm_g, w_ple_gate, norm_ple_g):
    batch, seq, d_model = x.shape
    depth = w_in.shape[0]
    m = batch * seq
    a_width = N_HEADS * HEAD_DIM
    sizes = (a_width, HEAD_DIM, HEAD_DIM, N_IDX_HEADS * IDX_DIM, IDX_DIM, N_IDX_HEADS,
             2 * GMLP_WIDTH, d_model, d_model)
    offs = np.concatenate([[0], np.cumsum(sizes)])
    near, cvec = _bias_tables(rel_bias, seq)

    xf = x.reshape(m, d_model)
    for i in range(depth):
        seg = lambda s: w_in[i][:, int(offs[s]):int(offs[s + 1])].astype(BF16)
        w_q, w_k, w_v, w_qi, w_ki, w_wi, w_uv, w_ga, w_gb = [seg(s) for s in range(9)]
        w_small = jnp.concatenate(
            [w_k, w_v, w_ki, jnp.pad(w_wi, ((0, 0), (0, LANES - N_IDX_HEADS)))], axis=1)

        h, k_, v_, kidx, widx = _norm_small_proj(xf, norm_mix_g[i], w_small, k_norm_g[i])
        q_hm = _headproj(h, w_q, q_norm_g[i], "q_proj", post_scale=HEAD_DIM ** -0.5 * LOG2E)
        qi_hm = _headproj(h, w_qi, None, "q_idx_proj")
        uv = _matmul(h, [w_uv], [], lambda a, e: a[0], F32, "uv_proj")
        y_a = _attention(qi_hm, kidx, jnp.transpose(widx[:, :N_IDX_HEADS]), q_hm, k_,
                         jnp.transpose(v_), near, cvec, batch, seq)
        y_b = _sgu(uv, sgu_ln_g[i], sgu_ln_b[i], sgu_w[i], sgu_b[i])
        merged = _merge(h, y_a, y_b, w_ga, w_gb,
                        w_branch_a[i].astype(BF16), w_branch_b[i].astype(BF16))
        x1 = _matmul(merged, [w_out[i].astype(BF16)], [xf],
                     lambda a, e: e[0] + a[0], F32, "out_proj")

        h2 = _rmsnorm(x1, norm_ffn_g[i])
        t = _matmul(h2, [w_gate_ffn[i].astype(BF16), w_up_ffn[i].astype(BF16)], [],
                    lambda a, e: jax.nn.silu(a[0]) * a[1], BF16, "ffn_gate_up")
        x2, x2g, ssq2 = _res_matmul(t, w_down_ffn[i].astype(BF16), x1, norm_ple_g[i], "ffn_down")

        pe = _ple(p[i].reshape(m, -1), w_ple[i].astype(BF16), ple_norm_g[i])
        xf = _matmul(x2g, [w_ple_gate[i].astype(BF16)], [x2, pe],
                     lambda a, e: e[0] + jax.nn.sigmoid(a[0]) * e[1], F32, "ple_gate", row_ssq=ssq2)
    return xf.reshape(batch, seq, d_model)
```

```python
import functools
import math

import numpy as np
import jax
import jax.numpy as jnp
from jax import lax
from jax.experimental import pallas as pl
from jax.experimental.pallas import tpu as pltpu

F32 = jnp.float32
BF16 = jnp.bfloat16
I32 = jnp.int32

N_HEADS = 16
HEAD_DIM = 128
N_IDX_HEADS = 32
IDX_DIM = 128
TOPK_MAX = 256
Q_BLOCK = 128
N_BUCKETS = 32
MAX_DISTANCE = 128
GMLP_WIDTH = 2048
N_GROUPS = 8
GROUP_DIM = GMLP_WIDTH // N_GROUPS
CHUNK = 128
EPS = 1e-6

LANES = 128
SUBLANES = 8
VMEM_LIMIT = 56 * 1024 * 1024
NEG = -(2.0 ** 100)
INT_MIN = -2 ** 31
INT_MAX = 2 ** 31 - 1
KEY_NEG_INF = -2139095041
LOG2E = math.log2(math.e)

SCORE_CHUNK = 4 * LANES
COUNT_CHUNK = SCORE_CHUNK
FAR_CHUNK = 4 * LANES
NEAR_KEYS = 2 * Q_BLOCK
HEAD_GROUP = 4
VT_ROWS = HEAD_DIM + 16
NORM_ROWS = 128
W_IN_ROWS = 128
MM_SUB_ROWS = 256

TILES = {
    "norm_kv_idx_proj": (512, 4 * LANES),
    "q_proj": (1024, 1024),
    "q_idx_proj": (1024, 1024),
    "uv_proj": (1024, 1024),
    "branch_merge": (512, 512),
    "out_proj": (1024, 512),
    "ffn_gate_up": (1024, 256),
    "ffn_down": (512, 512),
    "ple_gate": (1024, 512),
}


def _params(sem):
    return pltpu.CompilerParams(dimension_semantics=sem, vmem_limit_bytes=VMEM_LIMIT)


def _rmsnorm_kernel(x_ref, g_ref, o_ref):
    x = x_ref[...]
    ms = jnp.mean(x * x, axis=-1, keepdims=True)
    o_ref[...] = (x * lax.rsqrt(ms + EPS) * g_ref[...]).astype(o_ref.dtype)


def _rmsnorm(x, g, tm=256):
    m, d = x.shape
    return pl.pallas_call(
        _rmsnorm_kernel,
        grid=(m // tm,),
        in_specs=[pl.BlockSpec((tm, d), lambda i: (i, 0)),
                  pl.BlockSpec((1, d), lambda i: (0, 0))],
        out_specs=pl.BlockSpec((tm, d), lambda i: (i, 0)),
        out_shape=jax.ShapeDtypeStruct((m, d), BF16),
        compiler_params=_params(("parallel",)),
        name="rmsnorm",
    )(x, g.reshape(1, d))


def _lane_tiles(x):
    return [x[:, c * LANES:(c + 1) * LANES] for c in range(x.shape[1] // LANES)]


def _mm_kernel(*refs, n_w, n_x, epi, norm_dim):
    a_ref = refs[0]
    w_refs = refs[1:1 + n_w]
    x_refs = refs[1 + n_w:1 + n_w + n_x]
    if norm_dim:
        ssq_ref, o_ref, r_ref = refs[1 + n_w + n_x:]

        @pl.when(pl.program_id(1) == 0)
        def _():
            ms = jnp.sum(ssq_ref[...], axis=-1, keepdims=True) * (1.0 / norm_dim)
            r_ref[...] = jnp.broadcast_to(lax.rsqrt(ms + EPS), r_ref.shape)
    else:
        o_ref = refs[1 + n_w + n_x]
    for s in range(a_ref.shape[0] // MM_SUB_ROWS):
        rows = slice(s * MM_SUB_ROWS, (s + 1) * MM_SUB_ROWS)
        a = a_ref[rows, :]
        accs = [jnp.dot(a, w[...], preferred_element_type=F32) for w in w_refs]
        if norm_dim:
            accs = [acc * r_ref[rows, :] for acc in accs]
        o_ref[rows, :] = epi(accs, [x[rows, :] for x in x_refs]).astype(o_ref.dtype)


def _matmul(a, ws, extras, epi, out_dtype, name, row_ssq=None):
    tm, tn = TILES[name]
    m, k = a.shape
    n = ws[0].shape[1]
    normed = row_ssq is not None
    kern = functools.partial(_mm_kernel, n_w=len(ws), n_x=len(extras), epi=epi,
                             norm_dim=k if normed else 0)
    return pl.pallas_call(
        kern,
        grid=(m // tm, n // tn),
        in_specs=([pl.BlockSpec((tm, k), lambda i, j: (i, 0))]
                  + [pl.BlockSpec((k, tn), lambda i, j: (0, j)) for _ in ws]
                  + [pl.BlockSpec((tm, tn), lambda i, j: (i, j)) for _ in extras]
                  + ([pl.BlockSpec((tm, row_ssq.shape[1]), lambda i, j: (i, 0))] if normed else [])),
        out_specs=pl.BlockSpec((tm, tn), lambda i, j: (i, j)),
        out_shape=jax.ShapeDtypeStruct((m, n), out_dtype),
        scratch_shapes=[pltpu.VMEM((tm, tn), F32)] if normed else [],
        compiler_params=_params(("parallel", "arbitrary")),
        name=name,
    )(a, *ws, *extras, *([row_ssq] if normed else []))


def _res_mm_kernel(a_ref, w_ref, r_ref, g_ref, o_ref, og_ref, ssq_ref, *, nk):
    def sub_blocks(base_ref, last):
        for s in range(a_ref.shape[0] // MM_SUB_ROWS):
            rows = slice(s * MM_SUB_ROWS, (s + 1) * MM_SUB_ROWS)
            x = base_ref[rows, :] + jnp.dot(a_ref[rows, :], w_ref[...],
                                            preferred_element_type=F32)
            o_ref[rows, :] = x
            if last:
                og_ref[rows, :] = (x * g_ref[...]).astype(og_ref.dtype)
                ssq_ref[rows, :] = functools.reduce(lambda u, v: u + v, _lane_tiles(x * x))

    if nk == 1:
        sub_blocks(r_ref, True)
    else:
        @pl.when(pl.program_id(2) == 0)
        def _():
            sub_blocks(r_ref, False)

        @pl.when(pl.program_id(2) == 1)
        def _():
            sub_blocks(o_ref, True)


def _res_matmul(a, w, res, g_next, name, nk=1):
    tm, tn = TILES[name]
    m, k = a.shape
    n = w.shape[1]
    assert nk in (1, 2) and k % nk == 0
    tk = k // nk
    return pl.pallas_call(
        functools.partial(_res_mm_kernel, nk=nk),
        grid=(m // tm, n // tn, nk),
        in_specs=[pl.BlockSpec((tm, tk), lambda i, j, kk: (i, kk)),
                  pl.BlockSpec((tk, tn), lambda i, j, kk: (kk, j)),
                  pl.BlockSpec((tm, tn), lambda i, j, kk: (i, j)),
                  pl.BlockSpec((1, tn), lambda i, j, kk: (0, j))],
        out_specs=[pl.BlockSpec((tm, tn), lambda i, j, kk: (i, j)),
                   pl.BlockSpec((tm, tn), lambda i, j, kk: (i, j)),
                   pl.BlockSpec((tm, LANES), lambda i, j, kk: (i, j))],
        out_shape=[jax.ShapeDtypeStruct((m, n), F32),
                   jax.ShapeDtypeStruct((m, n), BF16),
                   jax.ShapeDtypeStruct((m, (n // tn) * LANES), F32)],
        compiler_params=_params(("parallel", "arbitrary", "arbitrary")),
        name=name,
    )(a, w, res, g_next.reshape(1, n))


def _headproj_kernel(*refs, heads, norm, post_scale):
    a_ref, w_ref = refs[0], refs[1]
    o_ref = refs[-1]
    acc = jnp.dot(a_ref[...], w_ref[...], preferred_element_type=F32)
    for hh in range(heads):
        blk = acc[:, hh * LANES:(hh + 1) * LANES]
        if norm:
            ms = jnp.mean(blk * blk, axis=-1, keepdims=True)
            blk = blk * lax.rsqrt(ms + EPS) * refs[2][...]
        if post_scale is not None:
            blk = blk * post_scale
        o_ref[hh] = blk.astype(o_ref.dtype)


def _headproj(h, w, g, name, post_scale=None):
    tm, tn = TILES[name]
    m, k = h.shape
    n_heads = w.shape[1] // LANES
    heads = tn // LANES
    norm = g is not None
    extra_specs = [pl.BlockSpec((1, LANES), lambda i, j: (0, 0))] if norm else []
    extra_args = [g.reshape(1, LANES)] if norm else []
    return pl.pallas_call(
        functools.partial(_headproj_kernel, heads=heads, norm=norm, post_scale=post_scale),
        grid=(m // tm, w.shape[1] // tn),
        in_specs=[pl.BlockSpec((tm, k), lambda i, j: (i, 0)),
                  pl.BlockSpec((k, tn), lambda i, j: (0, j))] + extra_specs,
        out_specs=pl.BlockSpec((heads, tm, LANES), lambda i, j: (j, i, 0)),
        out_shape=jax.ShapeDtypeStruct((n_heads, m, LANES), BF16),
        compiler_params=_params(("parallel", "arbitrary")),
        name=name,
    )(h, w, *extra_args)


def _small_kernel(x_ref, gx_ref, w_ref, g_ref, h_ref, k_ref, v_ref, ki_ref, wi_ref):
    def norm_rows(r, carry):
        rows = pl.ds(pl.multiple_of(r * NORM_ROWS, NORM_ROWS), NORM_ROWS)
        x = x_ref[rows, :]
        ms = jnp.mean(x * x, axis=-1, keepdims=True)
        h_ref[rows, :] = (x * lax.rsqrt(ms + EPS) * gx_ref[...]).astype(h_ref.dtype)
        return carry

    lax.fori_loop(0, x_ref.shape[0] // NORM_ROWS, norm_rows, 0)
    acc = jnp.dot(h_ref[...], w_ref[...], preferred_element_type=F32)
    kk = acc[:, 0:LANES]
    ms = jnp.mean(kk * kk, axis=-1, keepdims=True)
    k_ref[...] = (kk * lax.rsqrt(ms + EPS) * g_ref[...]).astype(k_ref.dtype)
    v_ref[...] = acc[:, LANES:2 * LANES].astype(v_ref.dtype)
    ki_ref[...] = acc[:, 2 * LANES:3 * LANES].astype(ki_ref.dtype)
    wi_ref[...] = acc[:, 3 * LANES:4 * LANES]


def _norm_small_proj(x, gx, w_small, g):
    tm, _ = TILES["norm_kv_idx_proj"]
    m, k = x.shape
    row = lambda i: (i, 0)
    return pl.pallas_call(
        _small_kernel,
        grid=(m // tm,),
        in_specs=[pl.BlockSpec((tm, k), row),
                  pl.BlockSpec((1, k), lambda i: (0, 0)),
                  pl.BlockSpec((k, 4 * LANES), lambda i: (0, 0)),
                  pl.BlockSpec((1, LANES), lambda i: (0, 0))],
        out_specs=[pl.BlockSpec((tm, k), row)] + [pl.BlockSpec((tm, LANES), row)] * 4,
        out_shape=[jax.ShapeDtypeStruct((m, k), BF16)]
                  + [jax.ShapeDtypeStruct((m, LANES), BF16)] * 3
                  + [jax.ShapeDtypeStruct((m, LANES), F32)],
        compiler_params=_params(("parallel",)),
        name="norm_kv_idx_proj",
    )(x, gx.reshape(1, k), w_small, g.reshape(1, LANES))


def _attn_kernel(qi_ref, kidx_ref, wt_ref, q_ref, k_ref, vt_ref, near_ref, cvec_ref, eye_ref, o_ref,
                 key_ref, m_ref, acc_ref, qa_ref, s_ref, lim_ref,
                 *, k_sel, idx_scale, pos_bits):
    j = pl.program_id(1)
    nt = (((1,), (1,)), ((), ()))
    n_chunks = (j + 4) // 4

    key_row = lax.broadcasted_iota(I32, (SCORE_CHUNK, Q_BLOCK), 0)
    q_col = lax.broadcasted_iota(I32, (SCORE_CHUNK, Q_BLOCK), 1)

    def score_chunk(c, carry):
        start = pl.multiple_of(c * SCORE_CHUNK, SCORE_CHUNK)
        kc = kidx_ref[pl.ds(start, SCORE_CHUNK), :]
        acc = jnp.zeros((SCORE_CHUNK, Q_BLOCK), F32)
        for hp in range(N_IDX_HEADS // 2):
            qpair = qi_ref[2 * hp:2 * hp + 2].reshape(2 * Q_BLOCK, IDX_DIM)
            d = lax.dot_general(kc, qpair, nt, preferred_element_type=F32)
            acc = acc + jnp.maximum(d[:, :Q_BLOCK], 0.0) * wt_ref[2 * hp:2 * hp + 1, :]
            acc = acc + jnp.maximum(d[:, Q_BLOCK:], 0.0) * wt_ref[2 * hp + 1:2 * hp + 2, :]
        s = acc * idx_scale
        causal = (start + key_row) <= (j * Q_BLOCK + q_col)
        s = jnp.where(causal, s, -jnp.inf)
        bits = lax.bitcast_convert_type(s, I32)
        key_ref[pl.ds(start, SCORE_CHUNK), :] = bits ^ ((bits >> 31) & 0x7FFFFFFF)
        return carry

    lax.fori_loop(0, n_chunks, score_chunk, 0)

    n_count = n_chunks
    count_row = lax.broadcasted_iota(I32, (COUNT_CHUNK, Q_BLOCK), 0)

    def count_keys(pred):
        def body(c, a):
            start = pl.multiple_of(c * COUNT_CHUNK, COUNT_CHUNK)
            hit = jnp.where(pred(key_ref[pl.ds(start, COUNT_CHUNK), :], start + count_row), 1, 0)
            return a + jnp.sum(hit.reshape(COUNT_CHUNK // SUBLANES, SUBLANES, Q_BLOCK), axis=0)

        a = lax.fori_loop(0, n_count, body, jnp.zeros((SUBLANES, Q_BLOCK), I32))
        return jnp.sum(a, axis=0, keepdims=True)

    def bit_body(i, carry):
        tu, cnt_tu = carry
        cu = tu | jnp.left_shift(jnp.int32(1), 31 - i)
        cs = cu ^ INT_MIN
        cnt = count_keys(lambda keys, pos: keys >= cs)
        take = cnt >= k_sel
        return jnp.where(take, cu, tu), jnp.where(take, cnt, cnt_tu)

    tu, cnt_ge = lax.fori_loop(
        0, 32, bit_body,
        (jnp.zeros((1, Q_BLOCK), I32), jnp.full((1, Q_BLOCK), n_count * COUNT_CHUNK, I32)))
    thr = tu ^ INT_MIN

    tied = (cnt_ge > k_sel) & (thr > KEY_NEG_INF)
    lim_ref[...] = jnp.full(lim_ref.shape, INT_MAX, I32)

    @pl.when(jnp.max(jnp.where(tied, 1, 0)) > 0)
    def _():
        n_tied_kept = k_sel - count_keys(lambda keys, pos: keys > thr)

        def pos_body(i, x):
            cand = x | jnp.left_shift(jnp.int32(1), pos_bits - 1 - i)
            cnt = count_keys(lambda keys, pos: (keys == thr) & (pos < cand))
            return jnp.where(cnt <= n_tied_kept, cand, x)

        x = lax.fori_loop(0, pos_bits, pos_body, jnp.zeros((1, Q_BLOCK), I32))
        lim_ref[...] = jnp.where(tied, x, INT_MAX)

    pos_lim = lim_ref[...]

    def selected(keys, pos):
        return (keys > thr) | ((keys == thr) & (pos < pos_lim))

    m_ref[...] = jnp.full(m_ref.shape, NEG, F32)
    acc_ref[...] = jnp.zeros(acc_ref.shape, F32)
    qa_ref[:, :HEAD_DIM] = q_ref[...].reshape(N_HEADS * Q_BLOCK, HEAD_DIM)
    qa_ref[:, HEAD_DIM:] = eye_ref[...]

    def softmax_step(h, t):
        hs = slice(h * Q_BLOCK, (h + 1) * Q_BLOCK)
        m_old = m_ref[:, hs]
        m_new = jnp.maximum(m_old, jnp.max(t, axis=0, keepdims=True))
        m_ref[:, hs] = m_new
        return jnp.exp2(t - m_new).astype(BF16), jnp.exp2(m_old - m_new)

    def attend(keys, vt, logits_of):
        groups = [slice(g * HEAD_GROUP * Q_BLOCK, (g + 1) * HEAD_GROUP * Q_BLOCK)
                  for g in range(N_HEADS // HEAD_GROUP)]
        n_keys = keys.shape[0]
        for g, gs in enumerate(groups):
            s_ref[g, 0:n_keys, :] = lax.dot_general(keys, qa_ref[gs, 0:keys.shape[1]], nt,
                                                    preferred_element_type=F32)
        for g, gs in enumerate(groups):
            pa = [softmax_step(g * HEAD_GROUP + e,
                               logits_of(g * HEAD_GROUP + e,
                                         s_ref[g, 0:n_keys, e * Q_BLOCK:(e + 1) * Q_BLOCK]))
                  for e in range(HEAD_GROUP)]
            p = jnp.concatenate([x[0] for x in pa], axis=1)
            alpha = jnp.concatenate([x[1] for x in pa], axis=1)
            pv = jnp.dot(vt, p, preferred_element_type=F32)
            acc_ref[:, gs] = acc_ref[:, gs] * alpha + pv

    far_end = (j - 1) * Q_BLOCK
    far_row = lax.broadcasted_iota(I32, (FAR_CHUNK, Q_BLOCK), 0)

    def far_body(c, carry):
        lo = c * FAR_CHUNK
        start = pl.multiple_of(jnp.minimum(lo, jnp.maximum(far_end - FAR_CHUNK, 0)), LANES)
        pos = start + far_row
        sel = selected(key_ref[pl.ds(start, FAR_CHUNK), :], pos) & (pos >= lo) & (pos < far_end)
        maskb = jnp.where(sel, 0.0, NEG).astype(BF16)
        kaug = jnp.concatenate([k_ref[pl.ds(start, FAR_CHUNK), :], maskb], axis=1)
        vt = vt_ref[:, pl.ds(start, FAR_CHUNK)]
        attend(kaug, vt, lambda h, s: s)
        return carry

    lax.fori_loop(0, (jnp.maximum(far_end, 0) + FAR_CHUNK - 1) // FAR_CHUNK, far_body, 0)
    m_ref[...] = m_ref[...] + cvec_ref[...]

    near_start = pl.multiple_of(jnp.maximum(j - 1, 0) * Q_BLOCK, LANES)
    first = jnp.where(j == 0, 1, 0)
    near_pos = near_start + lax.broadcasted_iota(I32, (NEAR_KEYS, Q_BLOCK), 0)
    sel = selected(key_ref[pl.ds(near_start, NEAR_KEYS), :], near_pos)
    maskb = jnp.where(sel, 0.0, NEG)
    k_near = k_ref[pl.ds(near_start, NEAR_KEYS), :]
    vt_near = vt_ref[:, pl.ds(near_start, NEAR_KEYS)]

    def near_logits(h, s):
        return s + near_ref[first, :, h * Q_BLOCK:(h + 1) * Q_BLOCK] + maskb

    attend(k_near, vt_near, near_logits)

    inv_l = 1.0 / acc_ref[HEAD_DIM:HEAD_DIM + 1, :]
    for h in range(N_HEADS):
        hs = slice(h * Q_BLOCK, (h + 1) * Q_BLOCK)
        o_ref[:, h * HEAD_DIM:(h + 1) * HEAD_DIM] = (
            acc_ref[0:HEAD_DIM, hs] * inv_l[:, hs]).T.astype(o_ref.dtype)


def _t5_bucket_static(n):
    max_exact = N_BUCKETS // 2
    nf = np.maximum(n, 1).astype(np.float32)
    large = max_exact + (np.log(nf / np.float32(max_exact)) / np.float32(math.log(MAX_DISTANCE / max_exact))
                         * np.float32(N_BUCKETS - max_exact)).astype(np.int32)
    large = np.minimum(large, N_BUCKETS - 1)
    return np.where(n < max_exact, n, large)


def _bias_tables(rel_bias, seq):
    buckets = _t5_bucket_static(np.arange(seq, dtype=np.int32))
    far = buckets[Q_BLOCK + 1:]
    assert (far == far[0]).all()
    bias2 = rel_bias.astype(F32) * LOG2E
    span, origin = 4 * NEAR_KEYS, 2 * NEAR_KEYS
    d = np.arange(span) - origin
    r = jnp.where(jnp.asarray(d >= 0)[None, :], bias2[buckets[np.clip(d, 0, seq - 1)]].T, NEG)
    y = jnp.tile(r, (1, NEAR_KEYS))[:, :NEAR_KEYS * (span - 1)].reshape(N_HEADS, NEAR_KEYS, span - 1)

    def table(off):
        t = y[:, :, origin + off:origin + off + Q_BLOCK]
        return jnp.transpose(t, (1, 0, 2)).reshape(NEAR_KEYS, N_HEADS * Q_BLOCK)

    near = jnp.stack([table(Q_BLOCK), table(0)])
    cvec = jnp.repeat(bias2[int(far[0])], Q_BLOCK).reshape(1, N_HEADS * Q_BLOCK)
    return near, cvec


def _attention(qi_hm, kidx, w_t, q_hm, k, v_t, near, cvec, batch, seq):
    m = batch * seq
    n_blk = seq // Q_BLOCK
    k_sel = min(TOPK_MAX, seq // 4)
    hq = N_HEADS * Q_BLOCK
    kern = functools.partial(
        _attn_kernel, k_sel=k_sel, idx_scale=IDX_DIM ** -0.5 * N_IDX_HEADS ** -0.5,
        pos_bits=seq.bit_length())
    blk = lambda b, j: (b * n_blk + j)
    v_t = jnp.concatenate([v_t, jnp.ones((VT_ROWS - HEAD_DIM, m), v_t.dtype)], axis=0)
    assert seq % COUNT_CHUNK == 0 and seq % FAR_CHUNK == 0
    eye = jnp.tile(jnp.eye(Q_BLOCK, dtype=BF16), (N_HEADS, 1))
    return pl.pallas_call(
        kern,
        grid=(batch, n_blk),
        in_specs=[pl.BlockSpec((N_IDX_HEADS, Q_BLOCK, IDX_DIM), lambda b, j: (0, blk(b, j), 0)),
                  pl.BlockSpec((seq, IDX_DIM), lambda b, j: (b, 0)),
                  pl.BlockSpec((N_IDX_HEADS, Q_BLOCK), lambda b, j: (0, blk(b, j))),
                  pl.BlockSpec((N_HEADS, Q_BLOCK, HEAD_DIM), lambda b, j: (0, blk(b, j), 0)),
                  pl.BlockSpec((seq, HEAD_DIM), lambda b, j: (b, 0)),
                  pl.BlockSpec((VT_ROWS, seq), lambda b, j: (0, b)),
                  pl.BlockSpec((2, NEAR_KEYS, hq), lambda b, j: (0, 0, 0)),
                  pl.BlockSpec((1, hq), lambda b, j: (0, 0)),
                  pl.BlockSpec((hq, Q_BLOCK), lambda b, j: (0, 0))],
        out_specs=pl.BlockSpec((Q_BLOCK, N_HEADS * HEAD_DIM), lambda b, j: (blk(b, j), 0)),
        out_shape=jax.ShapeDtypeStruct((m, N_HEADS * HEAD_DIM), BF16),
        scratch_shapes=[pltpu.VMEM((seq, Q_BLOCK), I32),
                        pltpu.VMEM((1, hq), F32),
                        pltpu.VMEM((VT_ROWS, hq), F32),
                        pltpu.VMEM((hq, HEAD_DIM + Q_BLOCK), BF16),
                        pltpu.VMEM((N_HEADS // HEAD_GROUP, FAR_CHUNK, HEAD_GROUP * Q_BLOCK), F32),
                        pltpu.VMEM((1, Q_BLOCK), I32)],
        compiler_params=_params(("parallel", "arbitrary")),
        name="dsa_attention",
    )(qi_hm, kidx, w_t, q_hm, k, v_t, near, cvec, eye)


def _gelu_tanh(x):
    c = -2.0 * math.sqrt(2.0 / math.pi) * LOG2E
    return x / (1.0 + jnp.exp2(x * (x * x * (0.044715 * c) + c)))


def _sgu_kernel(uv_ref, g_ref, b_ref, ws_ref, bs_ref, o_ref):
    uv = _gelu_tanh(uv_ref[...])
    u = uv[:, :GMLP_WIDTH]
    v = uv[:, GMLP_WIDTH:]
    mu = jnp.mean(v, axis=-1, keepdims=True)
    vc = v - mu
    var = jnp.mean(vc * vc, axis=-1, keepdims=True)
    vn = (vc * lax.rsqrt(var + EPS) * g_ref[...] + b_ref[...]).astype(BF16)
    r = lax.broadcasted_iota(I32, (CHUNK, CHUNK), 0)
    c = lax.broadcasted_iota(I32, (CHUNK, CHUNK), 1)
    tril = c <= r
    bs = bs_ref[...]
    for g in range(N_GROUPS):
        ws = jnp.where(tril, ws_ref[g], 0.0).astype(BF16)
        mixed = jnp.dot(ws, vn[:, g * GROUP_DIM:(g + 1) * GROUP_DIM],
                        preferred_element_type=F32) + bs[:, g:g + 1]
        o_ref[:, g * GROUP_DIM:(g + 1) * GROUP_DIM] = (
            u[:, g * GROUP_DIM:(g + 1) * GROUP_DIM] * mixed).astype(o_ref.dtype)


def _sgu(uv, ln_g, ln_b, w_s, b_s):
    m = uv.shape[0]
    bs_t = jnp.zeros((CHUNK, LANES), F32).at[:, :N_GROUPS].set(jnp.transpose(b_s))
    return pl.pallas_call(
        _sgu_kernel,
        grid=(m // CHUNK,),
        in_specs=[pl.BlockSpec((CHUNK, 2 * GMLP_WIDTH), lambda i: (i, 0)),
                  pl.BlockSpec((1, GMLP_WIDTH), lambda i: (0, 0)),
                  pl.BlockSpec((1, GMLP_WIDTH), lambda i: (0, 0)),
                  pl.BlockSpec((N_GROUPS, CHUNK, CHUNK), lambda i: (0, 0, 0)),
                  pl.BlockSpec((CHUNK, LANES), lambda i: (0, 0))],
        out_specs=pl.BlockSpec((CHUNK, GMLP_WIDTH), lambda i: (i, 0)),
        out_shape=jax.ShapeDtypeStruct((m, GMLP_WIDTH), BF16),
        compiler_params=_params(("parallel",)),
        name="chunked_sgu",
    )(uv, ln_g.reshape(1, -1), ln_b.reshape(1, -1), w_s, bs_t)


def _merge_kernel(h_ref, ya_ref, yb_ref, wga_ref, wgb_ref, wa_ref, wb_ref, o_ref):
    h = h_ref[...]
    ga = jax.nn.sigmoid(jnp.dot(h, wga_ref[...], preferred_element_type=F32))
    a = jnp.dot(ya_ref[...], wa_ref[...], preferred_element_type=F32)
    out = ga * a
    gb = jax.nn.sigmoid(jnp.dot(h, wgb_ref[...], preferred_element_type=F32))
    b = jnp.dot(yb_ref[...], wb_ref[...], preferred_element_type=F32)
    o_ref[...] = (out + gb * b).astype(o_ref.dtype)


def _merge(h, ya, yb, wga, wgb, wa, wb):
    tm, tn = TILES["branch_merge"]
    m, d = h.shape
    n = wga.shape[1]
    ka, kb = ya.shape[1], yb.shape[1]
    row = lambda i, j: (i, 0)
    colw = lambda i, j: (0, j)
    return pl.pallas_call(
        _merge_kernel,
        grid=(m // tm, n // tn),
        in_specs=[pl.BlockSpec((tm, d), row), pl.BlockSpec((tm, ka), row), pl.BlockSpec((tm, kb), row),
                  pl.BlockSpec((d, tn), colw), pl.BlockSpec((d, tn), colw),
                  pl.BlockSpec((ka, tn), colw), pl.BlockSpec((kb, tn), colw)],
        out_specs=pl.BlockSpec((tm, tn), lambda i, j: (i, j)),
        out_shape=jax.ShapeDtypeStruct((m, n), BF16),
        compiler_params=_params(("parallel", "arbitrary")),
        name="branch_merge",
    )(h, ya, yb, wga, wgb, wa, wb)


def _ple_kernel(p_ref, w_ref, g_ref, o_ref):
    acc = jnp.dot(p_ref[...].astype(BF16), w_ref[...], preferred_element_type=F32)
    ms = jnp.mean(acc * acc, axis=-1, keepdims=True)
    o_ref[...] = (acc * lax.rsqrt(ms + EPS) * g_ref[...]).astype(o_ref.dtype)


def _ple(p, w, g, tm=256):
    m, dp = p.shape
    d = w.shape[1]
    return pl.pallas_call(
        _ple_kernel,
        grid=(m // tm,),
        in_specs=[pl.BlockSpec((tm, dp), lambda i: (i, 0)),
                  pl.BlockSpec((dp, d), lambda i: (0, 0)),
                  pl.BlockSpec((1, d), lambda i: (0, 0))],
        out_specs=pl.BlockSpec((tm, d), lambda i: (i, 0)),
        out_shape=jax.ShapeDtypeStruct((m, d), F32),
        compiler_params=_params(("parallel",)),
        name="ple_embed",
    )(p, w, g.reshape(1, d))


def _split_w_in_kernel(w_ref, q_ref, small_ref, qi_ref, uv_ref, ga_ref, gb_ref, *, offs):
    def seg(s):
        return w_ref[:, offs[s]:offs[s + 1]].astype(BF16)

    q_ref[...] = seg(0)
    qi_ref[...] = seg(3)
    uv_ref[...] = seg(6)
    ga_ref[...] = seg(7)
    gb_ref[...] = seg(8)
    small_ref[:, 0:LANES] = seg(1)
    small_ref[:, LANES:2 * LANES] = seg(2)
    small_ref[:, 2 * LANES:3 * LANES] = seg(4)
    lane = lax.broadcasted_iota(I32, (w_ref.shape[0], LANES), 1)
    wi = w_ref[:, offs[5]:offs[5] + LANES]
    small_ref[:, 3 * LANES:4 * LANES] = jnp.where(lane < offs[6] - offs[5], wi, 0.0).astype(BF16)


def _split_w_in(w, offs, rows=W_IN_ROWS):
    k, n = w.shape
    widths = [offs[1] - offs[0], 4 * LANES, offs[4] - offs[3],
              offs[7] - offs[6], offs[8] - offs[7], offs[9] - offs[8]]
    assert offs[5] + LANES <= n and all(wd % LANES == 0 for wd in widths)
    return pl.pallas_call(
        functools.partial(_split_w_in_kernel, offs=tuple(offs)),
        grid=(k // rows,),
        in_specs=[pl.BlockSpec((rows, n), lambda i: (i, 0))],
        out_specs=[pl.BlockSpec((rows, wd), lambda i: (i, 0)) for wd in widths],
        out_shape=[jax.ShapeDtypeStruct((k, wd), BF16) for wd in widths],
        compiler_params=_params(("parallel",)),
        name="split_w_in",
    )(w)


def kernel(x, p, w_in, q_norm_g, k_norm_g, rel_bias, sgu_ln_g, sgu_ln_b, sgu_w, sgu_b, w_branch_a, w_branch_b, w_out, norm_mix_g, norm_ffn_g, w_gate_ffn, w_up_ffn, w_down_ffn, w_ple, ple_norm_g, w_ple_gate, norm_ple_g):
    batch, seq, d_model = x.shape
    depth = w_in.shape[0]
    m = batch * seq
    a_width = N_HEADS * HEAD_DIM
    sizes = (a_width, HEAD_DIM, HEAD_DIM, N_IDX_HEADS * IDX_DIM, IDX_DIM, N_IDX_HEADS,
             2 * GMLP_WIDTH, d_model, d_model)
    offs = np.concatenate([[0], np.cumsum(sizes)])
    near, cvec = _bias_tables(rel_bias, seq)

    xf = x.reshape(m, d_model)
    for i in range(depth):
        w_q, w_small, w_qi, w_uv, w_ga, w_gb = _split_w_in(w_in[i], [int(o) for o in offs])

        h, k_, v_, kidx, widx = _norm_small_proj(xf, norm_mix_g[i], w_small, k_norm_g[i])
        q_hm = _headproj(h, w_q, q_norm_g[i], "q_proj", post_scale=HEAD_DIM ** -0.5 * LOG2E)
        qi_hm = _headproj(h, w_qi, None, "q_idx_proj")
        uv = _matmul(h, [w_uv], [], lambda a, e: a[0], F32, "uv_proj")
        y_a = _attention(qi_hm, kidx, jnp.transpose(widx[:, :N_IDX_HEADS]), q_hm, k_,
                         jnp.transpose(v_), near, cvec, batch, seq)
        y_b = _sgu(uv, sgu_ln_g[i], sgu_ln_b[i], sgu_w[i], sgu_b[i])
        merged = _merge(h, y_a, y_b, w_ga, w_gb,
                        w_branch_a[i].astype(BF16), w_branch_b[i].astype(BF16))
        x1 = _matmul(merged, [w_out[i].astype(BF16)], [xf],
                     lambda a, e: e[0] + a[0], F32, "out_proj")

        h2 = _rmsnorm(x1, norm_ffn_g[i])
        t = _matmul(h2, [w_gate_ffn[i].astype(BF16), w_up_ffn[i].astype(BF16)], [],
                    lambda a, e: jax.nn.silu(a[0]) * a[1], BF16, "ffn_gate_up")
        x2, x2g, ssq2 = _res_matmul(t, w_down_ffn[i].astype(BF16), x1, norm_ple_g[i], "ffn_down")

        pe = _ple(p[i].reshape(m, -1), w_ple[i].astype(BF16), ple_norm_g[i])
        xf = _matmul(x2g, [w_ple_gate[i].astype(BF16)], [x2, pe],
                     lambda a, e: e[0] + jax.nn.sigmoid(a[0]) * e[1], F32, "ple_gate", row_ssq=ssq2)
    return xf.reshape(batch, seq, d_model)
```

```python
import functools
import math

import numpy as np
import jax
import jax.numpy as jnp
from jax import lax
from jax.experimental import pallas as pl
from jax.experimental.pallas import tpu as pltpu

F32 = jnp.float32
BF16 = jnp.bfloat16
I32 = jnp.int32

N_HEADS = 16
HEAD_DIM = 128
N_IDX_HEADS = 32
IDX_DIM = 128
TOPK_MAX = 256
Q_BLOCK = 128
N_BUCKETS = 32
MAX_DISTANCE = 128
GMLP_WIDTH = 2048
N_GROUPS = 8
GROUP_DIM = GMLP_WIDTH // N_GROUPS
CHUNK = 128
EPS = 1e-6

LANES = 128
SUBLANES = 8
VMEM_LIMIT = 56 * 1024 * 1024
NEG = -(2.0 ** 100)
INT_MIN = -2 ** 31
INT_MAX = 2 ** 31 - 1
KEY_NEG_INF = -2139095041
LOG2E = math.log2(math.e)

SCORE_CHUNK = 4 * LANES
COUNT_CHUNK = SCORE_CHUNK
FAR_CHUNK = 4 * LANES
FAR_BIG = 2 * FAR_CHUNK
NEAR_KEYS = 2 * Q_BLOCK
HEAD_GROUP = 4
VT_ROWS = HEAD_DIM + 16
NORM_ROWS = 128
MM_SUB_ROWS = 256

TILES = {
    "norm_kv_idx_proj": (512, 4 * LANES),
    "q_proj": (1024, 1024),
    "q_idx_proj": (1024, 1024),
    "uv_proj": (1024, 1024),
    "branch_merge": (512, 512),
    "out_proj": (1024, 512),
    "ffn_gate_up": (1024, 256),
    "ffn_down": (512, 512),
    "ple_gate": (1024, 512),
}


def _params(sem):
    return pltpu.CompilerParams(dimension_semantics=sem, vmem_limit_bytes=VMEM_LIMIT)


def _rmsnorm_kernel(x_ref, g_ref, o_ref):
    x = x_ref[...]
    ms = jnp.mean(x * x, axis=-1, keepdims=True)
    o_ref[...] = (x * lax.rsqrt(ms + EPS) * g_ref[...]).astype(o_ref.dtype)


def _rmsnorm(x, g, tm=256):
    m, d = x.shape
    return pl.pallas_call(
        _rmsnorm_kernel,
        grid=(m // tm,),
        in_specs=[pl.BlockSpec((tm, d), lambda i: (i, 0)),
                  pl.BlockSpec((1, d), lambda i: (0, 0))],
        out_specs=pl.BlockSpec((tm, d), lambda i: (i, 0)),
        out_shape=jax.ShapeDtypeStruct((m, d), BF16),
        compiler_params=_params(("parallel",)),
        name="rmsnorm",
    )(x, g.reshape(1, d))


def _lane_tiles(x):
    return [x[:, c * LANES:(c + 1) * LANES] for c in range(x.shape[1] // LANES)]


def _mm_kernel(*refs, n_w, n_x, epi, norm_dim):
    a_ref = refs[0]
    w_refs = refs[1:1 + n_w]
    x_refs = refs[1 + n_w:1 + n_w + n_x]
    if norm_dim:
        ssq_ref, o_ref, r_ref = refs[1 + n_w + n_x:]

        @pl.when(pl.program_id(1) == 0)
        def _():
            ms = jnp.sum(ssq_ref[...], axis=-1, keepdims=True) * (1.0 / norm_dim)
            r_ref[...] = jnp.broadcast_to(lax.rsqrt(ms + EPS), r_ref.shape)
    else:
        o_ref = refs[1 + n_w + n_x]
    for s in range(a_ref.shape[0] // MM_SUB_ROWS):
        rows = slice(s * MM_SUB_ROWS, (s + 1) * MM_SUB_ROWS)
        a = a_ref[rows, :]
        accs = [jnp.dot(a, w[...], preferred_element_type=F32) for w in w_refs]
        if norm_dim:
            accs = [acc * r_ref[rows, :] for acc in accs]
        o_ref[rows, :] = epi(accs, [x[rows, :] for x in x_refs]).astype(o_ref.dtype)


def _matmul(a, ws, extras, epi, out_dtype, name, row_ssq=None):
    tm, tn = TILES[name]
    m, k = a.shape
    n = ws[0].shape[1]
    normed = row_ssq is not None
    kern = functools.partial(_mm_kernel, n_w=len(ws), n_x=len(extras), epi=epi,
                             norm_dim=k if normed else 0)
    return pl.pallas_call(
        kern,
        grid=(m // tm, n // tn),
        in_specs=([pl.BlockSpec((tm, k), lambda i, j: (i, 0))]
                  + [pl.BlockSpec((k, tn), lambda i, j: (0, j)) for _ in ws]
                  + [pl.BlockSpec((tm, tn), lambda i, j: (i, j)) for _ in extras]
                  + ([pl.BlockSpec((tm, row_ssq.shape[1]), lambda i, j: (i, 0))] if normed else [])),
        out_specs=pl.BlockSpec((tm, tn), lambda i, j: (i, j)),
        out_shape=jax.ShapeDtypeStruct((m, n), out_dtype),
        scratch_shapes=[pltpu.VMEM((tm, tn), F32)] if normed else [],
        compiler_params=_params(("parallel", "arbitrary")),
        name=name,
    )(a, *ws, *extras, *([row_ssq] if normed else []))


def _res_mm_kernel(a_ref, w_ref, r_ref, g_ref, o_ref, og_ref, ssq_ref, *, nk):
    def sub_blocks(base_ref, last):
        for s in range(a_ref.shape[0] // MM_SUB_ROWS):
            rows = slice(s * MM_SUB_ROWS, (s + 1) * MM_SUB_ROWS)
            x = base_ref[rows, :] + jnp.dot(a_ref[rows, :], w_ref[...],
                                            preferred_element_type=F32)
            o_ref[rows, :] = x
            if last:
                og_ref[rows, :] = (x * g_ref[...]).astype(og_ref.dtype)
                ssq_ref[rows, :] = functools.reduce(lambda u, v: u + v, _lane_tiles(x * x))

    if nk == 1:
        sub_blocks(r_ref, True)
    else:
        @pl.when(pl.program_id(2) == 0)
        def _():
            sub_blocks(r_ref, False)

        @pl.when(pl.program_id(2) == 1)
        def _():
            sub_blocks(o_ref, True)


def _res_matmul(a, w, res, g_next, name, nk=1):
    tm, tn = TILES[name]
    m, k = a.shape
    n = w.shape[1]
    assert nk in (1, 2) and k % nk == 0
    tk = k // nk
    return pl.pallas_call(
        functools.partial(_res_mm_kernel, nk=nk),
        grid=(m // tm, n // tn, nk),
        in_specs=[pl.BlockSpec((tm, tk), lambda i, j, kk: (i, kk)),
                  pl.BlockSpec((tk, tn), lambda i, j, kk: (kk, j)),
                  pl.BlockSpec((tm, tn), lambda i, j, kk: (i, j)),
                  pl.BlockSpec((1, tn), lambda i, j, kk: (0, j))],
        out_specs=[pl.BlockSpec((tm, tn), lambda i, j, kk: (i, j)),
                   pl.BlockSpec((tm, tn), lambda i, j, kk: (i, j)),
                   pl.BlockSpec((tm, LANES), lambda i, j, kk: (i, j))],
        out_shape=[jax.ShapeDtypeStruct((m, n), F32),
                   jax.ShapeDtypeStruct((m, n), BF16),
                   jax.ShapeDtypeStruct((m, (n // tn) * LANES), F32)],
        compiler_params=_params(("parallel", "arbitrary", "arbitrary")),
        name=name,
    )(a, w, res, g_next.reshape(1, n))


def _headproj_kernel(*refs, heads, norm, post_scale):
    a_ref, w_ref = refs[0], refs[1]
    o_ref = refs[-1]
    acc = jnp.dot(a_ref[...], w_ref[...], preferred_element_type=F32)
    for hh in range(heads):
        blk = acc[:, hh * LANES:(hh + 1) * LANES]
        if norm:
            ms = jnp.mean(blk * blk, axis=-1, keepdims=True)
            blk = blk * lax.rsqrt(ms + EPS) * refs[2][...]
        if post_scale is not None:
            blk = blk * post_scale
        o_ref[hh] = blk.astype(o_ref.dtype)


def _headproj(h, w, g, name, post_scale=None):
    tm, tn = TILES[name]
    m, k = h.shape
    n_heads = w.shape[1] // LANES
    heads = tn // LANES
    norm = g is not None
    extra_specs = [pl.BlockSpec((1, LANES), lambda i, j: (0, 0))] if norm else []
    extra_args = [g.reshape(1, LANES)] if norm else []
    return pl.pallas_call(
        functools.partial(_headproj_kernel, heads=heads, norm=norm, post_scale=post_scale),
        grid=(m // tm, w.shape[1] // tn),
        in_specs=[pl.BlockSpec((tm, k), lambda i, j: (i, 0)),
                  pl.BlockSpec((k, tn), lambda i, j: (0, j))] + extra_specs,
        out_specs=pl.BlockSpec((heads, tm, LANES), lambda i, j: (j, i, 0)),
        out_shape=jax.ShapeDtypeStruct((n_heads, m, LANES), BF16),
        compiler_params=_params(("parallel", "arbitrary")),
        name=name,
    )(h, w, *extra_args)


def _small_kernel(x_ref, gx_ref, w_ref, g_ref, h_ref, k_ref, v_ref, ki_ref, wi_ref):
    def norm_rows(r, carry):
        rows = pl.ds(pl.multiple_of(r * NORM_ROWS, NORM_ROWS), NORM_ROWS)
        x = x_ref[rows, :]
        ms = jnp.mean(x * x, axis=-1, keepdims=True)
        h_ref[rows, :] = (x * lax.rsqrt(ms + EPS) * gx_ref[...]).astype(h_ref.dtype)
        return carry

    lax.fori_loop(0, x_ref.shape[0] // NORM_ROWS, norm_rows, 0)
    acc = jnp.dot(h_ref[...], w_ref[...], preferred_element_type=F32)
    kk = acc[:, 0:LANES]
    ms = jnp.mean(kk * kk, axis=-1, keepdims=True)
    k_ref[...] = (kk * lax.rsqrt(ms + EPS) * g_ref[...]).astype(k_ref.dtype)
    v_ref[...] = acc[:, LANES:2 * LANES].astype(v_ref.dtype)
    ki_ref[...] = acc[:, 2 * LANES:3 * LANES].astype(ki_ref.dtype)
    wi_ref[...] = acc[:, 3 * LANES:4 * LANES]


def _norm_small_proj(x, gx, w_small, g):
    tm, _ = TILES["norm_kv_idx_proj"]
    m, k = x.shape
    row = lambda i: (i, 0)
    return pl.pallas_call(
        _small_kernel,
        grid=(m // tm,),
        in_specs=[pl.BlockSpec((tm, k), row),
                  pl.BlockSpec((1, k), lambda i: (0, 0)),
                  pl.BlockSpec((k, 4 * LANES), lambda i: (0, 0)),
                  pl.BlockSpec((1, LANES), lambda i: (0, 0))],
        out_specs=[pl.BlockSpec((tm, k), row)] + [pl.BlockSpec((tm, LANES), row)] * 4,
        out_shape=[jax.ShapeDtypeStruct((m, k), BF16)]
                  + [jax.ShapeDtypeStruct((m, LANES), BF16)] * 3
                  + [jax.ShapeDtypeStruct((m, LANES), F32)],
        compiler_params=_params(("parallel",)),
        name="norm_kv_idx_proj",
    )(x, gx.reshape(1, k), w_small, g.reshape(1, LANES))


def _attn_kernel(qi_ref, kidx_ref, wt_ref, q_ref, k_ref, vt_ref, near_ref, cvec_ref, eye_ref, o_ref,
                 key_ref, m_ref, acc_ref, qa_ref, s_ref, lim_ref,
                 *, k_sel, idx_scale, pos_bits):
    j = pl.program_id(1)
    nt = (((1,), (1,)), ((), ()))
    n_chunks = (j + 4) // 4

    key_row = lax.broadcasted_iota(I32, (SCORE_CHUNK, Q_BLOCK), 0)
    q_col = lax.broadcasted_iota(I32, (SCORE_CHUNK, Q_BLOCK), 1)

    def score_chunk(c, carry):
        start = pl.multiple_of(c * SCORE_CHUNK, SCORE_CHUNK)
        kc = kidx_ref[pl.ds(start, SCORE_CHUNK), :]
        acc = jnp.zeros((SCORE_CHUNK, Q_BLOCK), F32)
        for hp in range(N_IDX_HEADS // 2):
            qpair = qi_ref[2 * hp:2 * hp + 2].reshape(2 * Q_BLOCK, IDX_DIM)
            d = lax.dot_general(kc, qpair, nt, preferred_element_type=F32)
            acc = acc + jnp.maximum(d[:, :Q_BLOCK], 0.0) * wt_ref[2 * hp:2 * hp + 1, :]
            acc = acc + jnp.maximum(d[:, Q_BLOCK:], 0.0) * wt_ref[2 * hp + 1:2 * hp + 2, :]
        s = acc * idx_scale
        causal = (start + key_row) <= (j * Q_BLOCK + q_col)
        s = jnp.where(causal, s, -jnp.inf)
        bits = lax.bitcast_convert_type(s, I32)
        key_ref[pl.ds(start, SCORE_CHUNK), :] = bits ^ ((bits >> 31) & 0x7FFFFFFF)
        return carry

    lax.fori_loop(0, n_chunks, score_chunk, 0)

    n_count = n_chunks
    count_row = lax.broadcasted_iota(I32, (COUNT_CHUNK, Q_BLOCK), 0)

    def count_keys(pred):
        def body(c, a):
            start = pl.multiple_of(c * COUNT_CHUNK, COUNT_CHUNK)
            hit = jnp.where(pred(key_ref[pl.ds(start, COUNT_CHUNK), :], start + count_row), 1, 0)
            return a + jnp.sum(hit.reshape(COUNT_CHUNK // SUBLANES, SUBLANES, Q_BLOCK), axis=0)

        a = lax.fori_loop(0, n_count, body, jnp.zeros((SUBLANES, Q_BLOCK), I32))
        return jnp.sum(a, axis=0, keepdims=True)

    def bit_body(i, carry):
        tu, cnt_tu = carry
        cu = tu | jnp.left_shift(jnp.int32(1), 31 - i)
        cs = cu ^ INT_MIN
        cnt = count_keys(lambda keys, pos: keys >= cs)
        take = cnt >= k_sel
        return jnp.where(take, cu, tu), jnp.where(take, cnt, cnt_tu)

    tu, cnt_ge = lax.fori_loop(
        0, 32, bit_body,
        (jnp.zeros((1, Q_BLOCK), I32), jnp.full((1, Q_BLOCK), n_count * COUNT_CHUNK, I32)))
    thr = tu ^ INT_MIN

    tied = (cnt_ge > k_sel) & (thr > KEY_NEG_INF)
    lim_ref[...] = jnp.full(lim_ref.shape, INT_MAX, I32)

    @pl.when(jnp.max(jnp.where(tied, 1, 0)) > 0)
    def _():
        n_tied_kept = k_sel - count_keys(lambda keys, pos: keys > thr)

        def pos_body(i, x):
            cand = x | jnp.left_shift(jnp.int32(1), pos_bits - 1 - i)
            cnt = count_keys(lambda keys, pos: (keys == thr) & (pos < cand))
            return jnp.where(cnt <= n_tied_kept, cand, x)

        x = lax.fori_loop(0, pos_bits, pos_body, jnp.zeros((1, Q_BLOCK), I32))
        lim_ref[...] = jnp.where(tied, x, INT_MAX)

    pos_lim = lim_ref[...]

    def selected(keys, pos):
        return (keys > thr) | ((keys == thr) & (pos < pos_lim))

    m_ref[...] = jnp.full(m_ref.shape, NEG, F32)
    acc_ref[...] = jnp.zeros(acc_ref.shape, F32)
    qa_ref[:, :HEAD_DIM] = q_ref[...].reshape(N_HEADS * Q_BLOCK, HEAD_DIM)
    qa_ref[:, HEAD_DIM:] = eye_ref[...]

    def softmax_step(h, t):
        hs = slice(h * Q_BLOCK, (h + 1) * Q_BLOCK)
        m_old = m_ref[:, hs]
        m_new = jnp.maximum(m_old, jnp.max(t, axis=0, keepdims=True))
        m_ref[:, hs] = m_new
        return jnp.exp2(t - m_new).astype(BF16), jnp.exp2(m_old - m_new)

    def attend(keys, vt, logits_of):
        groups = [slice(g * HEAD_GROUP * Q_BLOCK, (g + 1) * HEAD_GROUP * Q_BLOCK)
                  for g in range(N_HEADS // HEAD_GROUP)]
        n_keys = keys.shape[0]
        for g, gs in enumerate(groups):
            s_ref[g, 0:n_keys, :] = lax.dot_general(keys, qa_ref[gs, 0:keys.shape[1]], nt,
                                                    preferred_element_type=F32)
        for g, gs in enumerate(groups):
            pa = [softmax_step(g * HEAD_GROUP + e,
                               logits_of(g * HEAD_GROUP + e,
                                         s_ref[g, 0:n_keys, e * Q_BLOCK:(e + 1) * Q_BLOCK]))
                  for e in range(HEAD_GROUP)]
            p = jnp.concatenate([x[0] for x in pa], axis=1)
            alpha = jnp.concatenate([x[1] for x in pa], axis=1)
            pv = jnp.dot(vt, p, preferred_element_type=F32)
            acc_ref[:, gs] = acc_ref[:, gs] * alpha + pv

    far_end = (j - 1) * Q_BLOCK
    def far_chunk(lo, n_keys):
        start = pl.multiple_of(jnp.minimum(lo, jnp.maximum(far_end - n_keys, 0)), LANES)
        pos = start + lax.broadcasted_iota(I32, (n_keys, Q_BLOCK), 0)
        sel = selected(key_ref[pl.ds(start, n_keys), :], pos) & (pos >= lo) & (pos < far_end)
        maskb = jnp.where(sel, 0.0, NEG).astype(BF16)
        kaug = jnp.concatenate([k_ref[pl.ds(start, n_keys), :], maskb], axis=1)
        attend(kaug, vt_ref[:, pl.ds(start, n_keys)], lambda h, s: s)

    def loop(n, body):
        lax.fori_loop(0, n, lambda c, carry: (body(c), carry)[1], 0)

    n_big = jnp.maximum(far_end, 0) // FAR_BIG
    rest = jnp.maximum(far_end, 0) - n_big * FAR_BIG
    loop(n_big, lambda c: far_chunk(c * FAR_BIG, FAR_BIG))
    loop((rest + FAR_CHUNK - 1) // FAR_CHUNK,
         lambda c: far_chunk(n_big * FAR_BIG + c * FAR_CHUNK, FAR_CHUNK))
    m_ref[...] = m_ref[...] + cvec_ref[...]

    near_start = pl.multiple_of(jnp.maximum(j - 1, 0) * Q_BLOCK, LANES)
    first = jnp.where(j == 0, 1, 0)
    near_pos = near_start + lax.broadcasted_iota(I32, (NEAR_KEYS, Q_BLOCK), 0)
    sel = selected(key_ref[pl.ds(near_start, NEAR_KEYS), :], near_pos)
    maskb = jnp.where(sel, 0.0, NEG)
    k_near = k_ref[pl.ds(near_start, NEAR_KEYS), :]
    vt_near = vt_ref[:, pl.ds(near_start, NEAR_KEYS)]

    def near_logits(h, s):
        return s + near_ref[first, :, h * Q_BLOCK:(h + 1) * Q_BLOCK] + maskb

    attend(k_near, vt_near, near_logits)

    inv_l = 1.0 / acc_ref[HEAD_DIM:HEAD_DIM + 1, :]
    for h in range(N_HEADS):
        hs = slice(h * Q_BLOCK, (h + 1) * Q_BLOCK)
        o_ref[:, h * HEAD_DIM:(h + 1) * HEAD_DIM] = (
            acc_ref[0:HEAD_DIM, hs] * inv_l[:, hs]).T.astype(o_ref.dtype)


def _t5_bucket_static(n):
    max_exact = N_BUCKETS // 2
    nf = np.maximum(n, 1).astype(np.float32)
    large = max_exact + (np.log(nf / np.float32(max_exact)) / np.float32(math.log(MAX_DISTANCE / max_exact))
                         * np.float32(N_BUCKETS - max_exact)).astype(np.int32)
    large = np.minimum(large, N_BUCKETS - 1)
    return np.where(n < max_exact, n, large)


def _bias_tables(rel_bias, seq):
    buckets = _t5_bucket_static(np.arange(seq, dtype=np.int32))
    far = buckets[Q_BLOCK + 1:]
    assert (far == far[0]).all()
    bias2 = rel_bias.astype(F32) * LOG2E
    span, origin = 4 * NEAR_KEYS, 2 * NEAR_KEYS
    d = np.arange(span) - origin
    r = jnp.where(jnp.asarray(d >= 0)[None, :], bias2[buckets[np.clip(d, 0, seq - 1)]].T, NEG)
    y = jnp.tile(r, (1, NEAR_KEYS))[:, :NEAR_KEYS * (span - 1)].reshape(N_HEADS, NEAR_KEYS, span - 1)

    def table(off):
        t = y[:, :, origin + off:origin + off + Q_BLOCK]
        return jnp.transpose(t, (1, 0, 2)).reshape(NEAR_KEYS, N_HEADS * Q_BLOCK)

    near = jnp.stack([table(Q_BLOCK), table(0)])
    cvec = jnp.repeat(bias2[int(far[0])], Q_BLOCK).reshape(1, N_HEADS * Q_BLOCK)
    return near, cvec


def _attention(qi_hm, kidx, w_t, q_hm, k, v_t, near, cvec, batch, seq):
    m = batch * seq
    n_blk = seq // Q_BLOCK
    k_sel = min(TOPK_MAX, seq // 4)
    hq = N_HEADS * Q_BLOCK
    kern = functools.partial(
        _attn_kernel, k_sel=k_sel, idx_scale=IDX_DIM ** -0.5 * N_IDX_HEADS ** -0.5,
        pos_bits=seq.bit_length())
    blk = lambda b, j: (b * n_blk + j)
    v_t = jnp.concatenate([v_t, jnp.ones((VT_ROWS - HEAD_DIM, m), v_t.dtype)], axis=0)
    assert seq % COUNT_CHUNK == 0 and seq % FAR_CHUNK == 0
    eye = jnp.tile(jnp.eye(Q_BLOCK, dtype=BF16), (N_HEADS, 1))
    return pl.pallas_call(
        kern,
        grid=(batch, n_blk),
        in_specs=[pl.BlockSpec((N_IDX_HEADS, Q_BLOCK, IDX_DIM), lambda b, j: (0, blk(b, j), 0)),
                  pl.BlockSpec((seq, IDX_DIM), lambda b, j: (b, 0)),
                  pl.BlockSpec((N_IDX_HEADS, Q_BLOCK), lambda b, j: (0, blk(b, j))),
                  pl.BlockSpec((N_HEADS, Q_BLOCK, HEAD_DIM), lambda b, j: (0, blk(b, j), 0)),
                  pl.BlockSpec((seq, HEAD_DIM), lambda b, j: (b, 0)),
                  pl.BlockSpec((VT_ROWS, seq), lambda b, j: (0, b)),
                  pl.BlockSpec((2, NEAR_KEYS, hq), lambda b, j: (0, 0, 0)),
                  pl.BlockSpec((1, hq), lambda b, j: (0, 0)),
                  pl.BlockSpec((hq, Q_BLOCK), lambda b, j: (0, 0))],
        out_specs=pl.BlockSpec((Q_BLOCK, N_HEADS * HEAD_DIM), lambda b, j: (blk(b, j), 0)),
        out_shape=jax.ShapeDtypeStruct((m, N_HEADS * HEAD_DIM), BF16),
        scratch_shapes=[pltpu.VMEM((seq, Q_BLOCK), I32),
                        pltpu.VMEM((1, hq), F32),
                        pltpu.VMEM((VT_ROWS, hq), F32),
                        pltpu.VMEM((hq, HEAD_DIM + Q_BLOCK), BF16),
                        pltpu.VMEM((N_HEADS // HEAD_GROUP, FAR_BIG, HEAD_GROUP * Q_BLOCK), F32),
                        pltpu.VMEM((1, Q_BLOCK), I32)],
        compiler_params=_params(("parallel", "arbitrary")),
        name="dsa_attention",
    )(qi_hm, kidx, w_t, q_hm, k, v_t, near, cvec, eye)


def _gelu_tanh(x):
    c = -2.0 * math.sqrt(2.0 / math.pi) * LOG2E
    return x / (1.0 + jnp.exp2(x * (x * x * (0.044715 * c) + c)))


def _sgu_kernel(uv_ref, g_ref, b_ref, ws_ref, bs_ref, o_ref):
    uv = _gelu_tanh(uv_ref[...])
    u = uv[:, :GMLP_WIDTH]
    v = uv[:, GMLP_WIDTH:]
    mu = jnp.mean(v, axis=-1, keepdims=True)
    vc = v - mu
    var = jnp.mean(vc * vc, axis=-1, keepdims=True)
    vn = (vc * lax.rsqrt(var + EPS) * g_ref[...] + b_ref[...]).astype(BF16)
    r = lax.broadcasted_iota(I32, (CHUNK, CHUNK), 0)
    c = lax.broadcasted_iota(I32, (CHUNK, CHUNK), 1)
    tril = c <= r
    bs = bs_ref[...]
    for g in range(N_GROUPS):
        ws = jnp.where(tril, ws_ref[g], 0.0).astype(BF16)
        mixed = jnp.dot(ws, vn[:, g * GROUP_DIM:(g + 1) * GROUP_DIM],
                        preferred_element_type=F32) + bs[:, g:g + 1]
        o_ref[:, g * GROUP_DIM:(g + 1) * GROUP_DIM] = (
            u[:, g * GROUP_DIM:(g + 1) * GROUP_DIM] * mixed).astype(o_ref.dtype)


def _sgu(uv, ln_g, ln_b, w_s, b_s):
    m = uv.shape[0]
    bs_t = jnp.zeros((CHUNK, LANES), F32).at[:, :N_GROUPS].set(jnp.transpose(b_s))
    return pl.pallas_call(
        _sgu_kernel,
        grid=(m // CHUNK,),
        in_specs=[pl.BlockSpec((CHUNK, 2 * GMLP_WIDTH), lambda i: (i, 0)),
                  pl.BlockSpec((1, GMLP_WIDTH), lambda i: (0, 0)),
                  pl.BlockSpec((1, GMLP_WIDTH), lambda i: (0, 0)),
                  pl.BlockSpec((N_GROUPS, CHUNK, CHUNK), lambda i: (0, 0, 0)),
                  pl.BlockSpec((CHUNK, LANES), lambda i: (0, 0))],
        out_specs=pl.BlockSpec((CHUNK, GMLP_WIDTH), lambda i: (i, 0)),
        out_shape=jax.ShapeDtypeStruct((m, GMLP_WIDTH), BF16),
        compiler_params=_params(("parallel",)),
        name="chunked_sgu",
    )(uv, ln_g.reshape(1, -1), ln_b.reshape(1, -1), w_s, bs_t)


def _merge_kernel(h_ref, ya_ref, yb_ref, wga_ref, wgb_ref, wa_ref, wb_ref, o_ref):
    h = h_ref[...]
    ga = jax.nn.sigmoid(jnp.dot(h, wga_ref[...], preferred_element_type=F32))
    a = jnp.dot(ya_ref[...], wa_ref[...], preferred_element_type=F32)
    out = ga * a
    gb = jax.nn.sigmoid(jnp.dot(h, wgb_ref[...], preferred_element_type=F32))
    b = jnp.dot(yb_ref[...], wb_ref[...], preferred_element_type=F32)
    o_ref[...] = (out + gb * b).astype(o_ref.dtype)


def _merge(h, ya, yb, wga, wgb, wa, wb):
    tm, tn = TILES["branch_merge"]
    m, d = h.shape
    n = wga.shape[1]
    ka, kb = ya.shape[1], yb.shape[1]
    row = lambda i, j: (i, 0)
    colw = lambda i, j: (0, j)
    return pl.pallas_call(
        _merge_kernel,
        grid=(m // tm, n // tn),
        in_specs=[pl.BlockSpec((tm, d), row), pl.BlockSpec((tm, ka), row), pl.BlockSpec((tm, kb), row),
                  pl.BlockSpec((d, tn), colw), pl.BlockSpec((d, tn), colw),
                  pl.BlockSpec((ka, tn), colw), pl.BlockSpec((kb, tn), colw)],
        out_specs=pl.BlockSpec((tm, tn), lambda i, j: (i, j)),
        out_shape=jax.ShapeDtypeStruct((m, n), BF16),
        compiler_params=_params(("parallel", "arbitrary")),
        name="branch_merge",
    )(h, ya, yb, wga, wgb, wa, wb)


def _ple_kernel(p_ref, w_ref, g_ref, o_ref):
    acc = jnp.dot(p_ref[...].astype(BF16), w_ref[...], preferred_element_type=F32)
    ms = jnp.mean(acc * acc, axis=-1, keepdims=True)
    o_ref[...] = (acc * lax.rsqrt(ms + EPS) * g_ref[...]).astype(o_ref.dtype)


def _ple(p, w, g, tm=256):
    m, dp = p.shape
    d = w.shape[1]
    return pl.pallas_call(
        _ple_kernel,
        grid=(m // tm,),
        in_specs=[pl.BlockSpec((tm, dp), lambda i: (i, 0)),
                  pl.BlockSpec((dp, d), lambda i: (0, 0)),
                  pl.BlockSpec((1, d), lambda i: (0, 0))],
        out_specs=pl.BlockSpec((tm, d), lambda i: (i, 0)),
        out_shape=jax.ShapeDtypeStruct((m, d), F32),
        compiler_params=_params(("parallel",)),
        name="ple_embed",
    )(p, w, g.reshape(1, d))


def kernel(x, p, w_in, q_norm_g, k_norm_g, rel_bias, sgu_ln_g, sgu_ln_b, sgu_w, sgu_b, w_branch_a, w_branch_b, w_out, norm_mix_g, norm_ffn_g, w_gate_ffn, w_up_ffn, w_down_ffn, w_ple, ple_norm_g, w_ple_gate, norm_ple_g):
    batch, seq, d_model = x.shape
    depth = w_in.shape[0]
    m = batch * seq
    a_width = N_HEADS * HEAD_DIM
    sizes = (a_width, HEAD_DIM, HEAD_DIM, N_IDX_HEADS * IDX_DIM, IDX_DIM, N_IDX_HEADS,
             2 * GMLP_WIDTH, d_model, d_model)
    offs = np.concatenate([[0], np.cumsum(sizes)])
    near, cvec = _bias_tables(rel_bias, seq)

    xf = x.reshape(m, d_model)
    for i in range(depth):
        seg = lambda s: w_in[i][:, int(offs[s]):int(offs[s + 1])].astype(BF16)
        w_q, w_k, w_v, w_qi, w_ki, w_wi, w_uv, w_ga, w_gb = [seg(s) for s in range(9)]
        w_small = jnp.concatenate(
            [w_k, w_v, w_ki, jnp.pad(w_wi, ((0, 0), (0, LANES - N_IDX_HEADS)))], axis=1)

        h, k_, v_, kidx, widx = _norm_small_proj(xf, norm_mix_g[i], w_small, k_norm_g[i])
        q_hm = _headproj(h, w_q, q_norm_g[i], "q_proj", post_scale=HEAD_DIM ** -0.5 * LOG2E)
        qi_hm = _headproj(h, w_qi, None, "q_idx_proj")
        uv = _matmul(h, [w_uv], [], lambda a, e: a[0], F32, "uv_proj")
        y_a = _attention(qi_hm, kidx, jnp.transpose(widx[:, :N_IDX_HEADS]), q_hm, k_,
                         jnp.transpose(v_), near, cvec, batch, seq)
        y_b = _sgu(uv, sgu_ln_g[i], sgu_ln_b[i], sgu_w[i], sgu_b[i])
        merged = _merge(h, y_a, y_b, w_ga, w_gb,
                        w_branch_a[i].astype(BF16), w_branch_b[i].astype(BF16))
        x1 = _matmul(merged, [w_out[i].astype(BF16)], [xf],
                     lambda a, e: e[0] + a[0], F32, "out_proj")

        h2 = _rmsnorm(x1, norm_ffn_g[i])
        t = _matmul(h2, [w_gate_ffn[i].astype(BF16), w_up_ffn[i].astype(BF16)], [],
                    lambda a, e: jax.nn.silu(a[0]) * a[1], BF16, "ffn_gate_up")
        x2, x2g, ssq2 = _res_matmul(t, w_down_ffn[i].astype(BF16), x1, norm_ple_g[i], "ffn_down")

        pe = _ple(p[i].reshape(m, -1), w_ple[i].astype(BF16), ple_norm_g[i])
        xf = _matmul(x2g, [w_ple_gate[i].astype(BF16)], [x2, pe],
                     lambda a, e: e[0] + jax.nn.sigmoid(a[0]) * e[1], F32, "ple_gate", row_ssq=ssq2)
    return xf.reshape(batch, seq, d_model)
```

```python
import functools
import math

import numpy as np
import jax
import jax.numpy as jnp
from jax import lax
from jax.experimental import pallas as pl
from jax.experimental.pallas import tpu as pltpu

F32 = jnp.float32
BF16 = jnp.bfloat16
I32 = jnp.int32

N_HEADS = 16
HEAD_DIM = 128
N_IDX_HEADS = 32
IDX_DIM = 128
TOPK_MAX = 256
Q_BLOCK = 128
N_BUCKETS = 32
MAX_DISTANCE = 128
GMLP_WIDTH = 2048
N_GROUPS = 8
GROUP_DIM = GMLP_WIDTH // N_GROUPS
CHUNK = 128
EPS = 1e-6

LANES = 128
SUBLANES = 8
VMEM_LIMIT = 56 * 1024 * 1024
NEG = -(2.0 ** 100)
INT_MIN = -2 ** 31
INT_MAX = 2 ** 31 - 1
KEY_NEG_INF = -2139095041
LOG2E = math.log2(math.e)

SCORE_CHUNK = 4 * LANES
COUNT_CHUNK = SCORE_CHUNK
FAR_CHUNK = 4 * LANES
FAR_BIG = 2 * FAR_CHUNK
NEAR_KEYS = 2 * Q_BLOCK
HEAD_GROUP = 4
VT_ROWS = HEAD_DIM + 16
NORM_ROWS = 128
MM_SUB_ROWS = 256

TILES = {
    "norm_kv_idx_proj": (512, 4 * LANES),
    "q_proj": (1024, 1024),
    "q_idx_proj": (1024, 1024),
    "uv_proj": (1024, 1024),
    "branch_merge": (512, 512),
    "out_proj": (1024, 512),
    "ffn_gate_up": (1024, 256),
    "ffn_down": (512, 512),
    "ple_gate": (1024, 512),
}


def _params(sem):
    return pltpu.CompilerParams(dimension_semantics=sem, vmem_limit_bytes=VMEM_LIMIT)


def _rmsnorm_kernel(x_ref, g_ref, o_ref):
    x = x_ref[...]
    ms = jnp.mean(x * x, axis=-1, keepdims=True)
    o_ref[...] = (x * lax.rsqrt(ms + EPS) * g_ref[...]).astype(o_ref.dtype)


def _rmsnorm(x, g, tm=256):
    m, d = x.shape
    return pl.pallas_call(
        _rmsnorm_kernel,
        grid=(m // tm,),
        in_specs=[pl.BlockSpec((tm, d), lambda i: (i, 0)),
                  pl.BlockSpec((1, d), lambda i: (0, 0))],
        out_specs=pl.BlockSpec((tm, d), lambda i: (i, 0)),
        out_shape=jax.ShapeDtypeStruct((m, d), BF16),
        compiler_params=_params(("parallel",)),
        name="rmsnorm",
    )(x, g.reshape(1, d))


def _lane_tiles(x):
    return [x[:, c * LANES:(c + 1) * LANES] for c in range(x.shape[1] // LANES)]


def _mm_kernel(*refs, n_w, n_x, epi, norm_dim):
    a_ref = refs[0]
    w_refs = refs[1:1 + n_w]
    x_refs = refs[1 + n_w:1 + n_w + n_x]
    if norm_dim:
        ssq_ref, o_ref, r_ref = refs[1 + n_w + n_x:]

        @pl.when(pl.program_id(1) == 0)
        def _():
            ms = jnp.sum(ssq_ref[...], axis=-1, keepdims=True) * (1.0 / norm_dim)
            r_ref[...] = jnp.broadcast_to(lax.rsqrt(ms + EPS), r_ref.shape)
    else:
        o_ref = refs[1 + n_w + n_x]
    for s in range(a_ref.shape[0] // MM_SUB_ROWS):
        rows = slice(s * MM_SUB_ROWS, (s + 1) * MM_SUB_ROWS)
        a = a_ref[rows, :]
        accs = [jnp.dot(a, w[...], preferred_element_type=F32) for w in w_refs]
        if norm_dim:
            accs = [acc * r_ref[rows, :] for acc in accs]
        o_ref[rows, :] = epi(accs, [x[rows, :] for x in x_refs]).astype(o_ref.dtype)


def _matmul(a, ws, extras, epi, out_dtype, name, row_ssq=None):
    tm, tn = TILES[name]
    m, k = a.shape
    n = ws[0].shape[1]
    normed = row_ssq is not None
    kern = functools.partial(_mm_kernel, n_w=len(ws), n_x=len(extras), epi=epi,
                             norm_dim=k if normed else 0)
    return pl.pallas_call(
        kern,
        grid=(m // tm, n // tn),
        in_specs=([pl.BlockSpec((tm, k), lambda i, j: (i, 0))]
                  + [pl.BlockSpec((k, tn), lambda i, j: (0, j)) for _ in ws]
                  + [pl.BlockSpec((tm, tn), lambda i, j: (i, j)) for _ in extras]
                  + ([pl.BlockSpec((tm, row_ssq.shape[1]), lambda i, j: (i, 0))] if normed else [])),
        out_specs=pl.BlockSpec((tm, tn), lambda i, j: (i, j)),
        out_shape=jax.ShapeDtypeStruct((m, n), out_dtype),
        scratch_shapes=[pltpu.VMEM((tm, tn), F32)] if normed else [],
        compiler_params=_params(("parallel", "arbitrary")),
        name=name,
    )(a, *ws, *extras, *([row_ssq] if normed else []))


def _res_mm_kernel(a_ref, w_ref, r_ref, g_ref, o_ref, og_ref, ssq_ref, *, nk):
    def sub_blocks(base_ref, last):
        for s in range(a_ref.shape[0] // MM_SUB_ROWS):
            rows = slice(s * MM_SUB_ROWS, (s + 1) * MM_SUB_ROWS)
            x = base_ref[rows, :] + jnp.dot(a_ref[rows, :], w_ref[...],
                                            preferred_element_type=F32)
            o_ref[rows, :] = x
            if last:
                og_ref[rows, :] = (x * g_ref[...]).astype(og_ref.dtype)
                ssq_ref[rows, :] = functools.reduce(lambda u, v: u + v, _lane_tiles(x * x))

    if nk == 1:
        sub_blocks(r_ref, True)
    else:
        @pl.when(pl.program_id(2) == 0)
        def _():
            sub_blocks(r_ref, False)

        @pl.when(pl.program_id(2) == 1)
        def _():
            sub_blocks(o_ref, True)


def _res_matmul(a, w, res, g_next, name, nk=1):
    tm, tn = TILES[name]
    m, k = a.shape
    n = w.shape[1]
    assert nk in (1, 2) and k % nk == 0
    tk = k // nk
    return pl.pallas_call(
        functools.partial(_res_mm_kernel, nk=nk),
        grid=(m // tm, n // tn, nk),
        in_specs=[pl.BlockSpec((tm, tk), lambda i, j, kk: (i, kk)),
                  pl.BlockSpec((tk, tn), lambda i, j, kk: (kk, j)),
                  pl.BlockSpec((tm, tn), lambda i, j, kk: (i, j)),
                  pl.BlockSpec((1, tn), lambda i, j, kk: (0, j))],
        out_specs=[pl.BlockSpec((tm, tn), lambda i, j, kk: (i, j)),
                   pl.BlockSpec((tm, tn), lambda i, j, kk: (i, j)),
                   pl.BlockSpec((tm, LANES), lambda i, j, kk: (i, j))],
        out_shape=[jax.ShapeDtypeStruct((m, n), F32),
                   jax.ShapeDtypeStruct((m, n), BF16),
                   jax.ShapeDtypeStruct((m, (n // tn) * LANES), F32)],
        compiler_params=_params(("parallel", "arbitrary", "arbitrary")),
        name=name,
    )(a, w, res, g_next.reshape(1, n))


def _headproj_kernel(*refs, heads, norm, post_scale):
    a_ref, w_ref = refs[0], refs[1]
    o_ref = refs[-1]
    acc = jnp.dot(a_ref[...], w_ref[...], preferred_element_type=F32)
    for hh in range(heads):
        blk = acc[:, hh * LANES:(hh + 1) * LANES]
        if norm:
            ms = jnp.mean(blk * blk, axis=-1, keepdims=True)
            blk = blk * lax.rsqrt(ms + EPS) * refs[2][...]
        if post_scale is not None:
            blk = blk * post_scale
        o_ref[hh] = blk.astype(o_ref.dtype)


def _headproj(h, w, g, name, post_scale=None):
    tm, tn = TILES[name]
    m, k = h.shape
    n_heads = w.shape[1] // LANES
    heads = tn // LANES
    norm = g is not None
    extra_specs = [pl.BlockSpec((1, LANES), lambda i, j: (0, 0))] if norm else []
    extra_args = [g.reshape(1, LANES)] if norm else []
    return pl.pallas_call(
        functools.partial(_headproj_kernel, heads=heads, norm=norm, post_scale=post_scale),
        grid=(m // tm, w.shape[1] // tn),
        in_specs=[pl.BlockSpec((tm, k), lambda i, j: (i, 0)),
                  pl.BlockSpec((k, tn), lambda i, j: (0, j))] + extra_specs,
        out_specs=pl.BlockSpec((heads, tm, LANES), lambda i, j: (j, i, 0)),
        out_shape=jax.ShapeDtypeStruct((n_heads, m, LANES), BF16),
        compiler_params=_params(("parallel", "arbitrary")),
        name=name,
    )(h, w, *extra_args)


def _small_kernel(x_ref, gx_ref, w_ref, g_ref, h_ref, k_ref, v_ref, ki_ref, wi_ref):
    def norm_rows(r, carry):
        rows = pl.ds(pl.multiple_of(r * NORM_ROWS, NORM_ROWS), NORM_ROWS)
        x = x_ref[rows, :]
        ms = jnp.mean(x * x, axis=-1, keepdims=True)
        h_ref[rows, :] = (x * lax.rsqrt(ms + EPS) * gx_ref[...]).astype(h_ref.dtype)
        return carry

    lax.fori_loop(0, x_ref.shape[0] // NORM_ROWS, norm_rows, 0)
    acc = jnp.dot(h_ref[...], w_ref[...], preferred_element_type=F32)
    kk = acc[:, 0:LANES]
    ms = jnp.mean(kk * kk, axis=-1, keepdims=True)
    k_ref[...] = (kk * lax.rsqrt(ms + EPS) * g_ref[...]).astype(k_ref.dtype)
    v_ref[...] = acc[:, LANES:2 * LANES].astype(v_ref.dtype)
    ki_ref[...] = acc[:, 2 * LANES:3 * LANES].astype(ki_ref.dtype)
    wi_ref[...] = acc[:, 3 * LANES:4 * LANES]


def _norm_small_proj(x, gx, w_small, g):
    tm, _ = TILES["norm_kv_idx_proj"]
    m, k = x.shape
    row = lambda i: (i, 0)
    return pl.pallas_call(
        _small_kernel,
        grid=(m // tm,),
        in_specs=[pl.BlockSpec((tm, k), row),
                  pl.BlockSpec((1, k), lambda i: (0, 0)),
                  pl.BlockSpec((k, 4 * LANES), lambda i: (0, 0)),
                  pl.BlockSpec((1, LANES), lambda i: (0, 0))],
        out_specs=[pl.BlockSpec((tm, k), row)] + [pl.BlockSpec((tm, LANES), row)] * 4,
        out_shape=[jax.ShapeDtypeStruct((m, k), BF16)]
                  + [jax.ShapeDtypeStruct((m, LANES), BF16)] * 3
                  + [jax.ShapeDtypeStruct((m, LANES), F32)],
        compiler_params=_params(("parallel",)),
        name="norm_kv_idx_proj",
    )(x, gx.reshape(1, k), w_small, g.reshape(1, LANES))


def _attn_kernel(qi_ref, kidx_ref, wt_ref, q_ref, k_ref, vt_ref, near_ref, cvec_ref, eye_ref, o_ref,
                 key_ref, m_ref, acc_ref, qa_ref, s_ref, lim_ref,
                 *, k_sel, idx_scale, pos_bits):
    j = pl.program_id(1)
    nt = (((1,), (1,)), ((), ()))
    n_chunks = (j + 4) // 4

    def score_keys(start, n_keys):
        start = pl.multiple_of(start, SCORE_CHUNK)
        kc = kidx_ref[pl.ds(start, n_keys), :]
        acc = jnp.zeros((n_keys, Q_BLOCK), F32)
        for hp in range(N_IDX_HEADS // 2):
            qpair = qi_ref[2 * hp:2 * hp + 2].reshape(2 * Q_BLOCK, IDX_DIM)
            d = lax.dot_general(kc, qpair, nt, preferred_element_type=F32)
            acc = acc + jnp.maximum(d[:, :Q_BLOCK], 0.0) * wt_ref[2 * hp:2 * hp + 1, :]
            acc = acc + jnp.maximum(d[:, Q_BLOCK:], 0.0) * wt_ref[2 * hp + 1:2 * hp + 2, :]
        s = acc * idx_scale
        key_pos = start + lax.broadcasted_iota(I32, (n_keys, Q_BLOCK), 0)
        q_pos = j * Q_BLOCK + lax.broadcasted_iota(I32, (n_keys, Q_BLOCK), 1)
        s = jnp.where(key_pos <= q_pos, s, -jnp.inf)
        bits = lax.bitcast_convert_type(s, I32)
        key_ref[pl.ds(start, n_keys), :] = bits ^ ((bits >> 31) & 0x7FFFFFFF)

    def loop(n, body):
        lax.fori_loop(0, n, lambda c, carry: (body(c), carry)[1], 0)

    loop(n_chunks // 2, lambda c: score_keys(c * (2 * SCORE_CHUNK), 2 * SCORE_CHUNK))
    loop(n_chunks % 2, lambda c: score_keys((n_chunks - 1) * SCORE_CHUNK, SCORE_CHUNK))

    n_count = n_chunks
    count_row = lax.broadcasted_iota(I32, (COUNT_CHUNK, Q_BLOCK), 0)

    def count_keys(pred):
        def body(c, a):
            start = pl.multiple_of(c * COUNT_CHUNK, COUNT_CHUNK)
            hit = jnp.where(pred(key_ref[pl.ds(start, COUNT_CHUNK), :], start + count_row), 1, 0)
            return a + jnp.sum(hit.reshape(COUNT_CHUNK // SUBLANES, SUBLANES, Q_BLOCK), axis=0)

        a = lax.fori_loop(0, n_count, body, jnp.zeros((SUBLANES, Q_BLOCK), I32))
        return jnp.sum(a, axis=0, keepdims=True)

    def bit_body(i, carry):
        tu, cnt_tu = carry
        cu = tu | jnp.left_shift(jnp.int32(1), 31 - i)
        cs = cu ^ INT_MIN
        cnt = count_keys(lambda keys, pos: keys >= cs)
        take = cnt >= k_sel
        return jnp.where(take, cu, tu), jnp.where(take, cnt, cnt_tu)

    tu, cnt_ge = lax.fori_loop(
        0, 32, bit_body,
        (jnp.zeros((1, Q_BLOCK), I32), jnp.full((1, Q_BLOCK), n_count * COUNT_CHUNK, I32)))
    thr = tu ^ INT_MIN

    tied = (cnt_ge > k_sel) & (thr > KEY_NEG_INF)
    lim_ref[...] = jnp.full(lim_ref.shape, INT_MAX, I32)

    @pl.when(jnp.max(jnp.where(tied, 1, 0)) > 0)
    def _():
        n_tied_kept = k_sel - count_keys(lambda keys, pos: keys > thr)

        def pos_body(i, x):
            cand = x | jnp.left_shift(jnp.int32(1), pos_bits - 1 - i)
            cnt = count_keys(lambda keys, pos: (keys == thr) & (pos < cand))
            return jnp.where(cnt <= n_tied_kept, cand, x)

        x = lax.fori_loop(0, pos_bits, pos_body, jnp.zeros((1, Q_BLOCK), I32))
        lim_ref[...] = jnp.where(tied, x, INT_MAX)

    pos_lim = lim_ref[...]

    def selected(keys, pos):
        return (keys > thr) | ((keys == thr) & (pos < pos_lim))

    m_ref[...] = jnp.full(m_ref.shape, NEG, F32)
    acc_ref[...] = jnp.zeros(acc_ref.shape, F32)
    qa_ref[:, :HEAD_DIM] = q_ref[...].reshape(N_HEADS * Q_BLOCK, HEAD_DIM)
    qa_ref[:, HEAD_DIM:] = eye_ref[...]

    def softmax_step(h, t):
        hs = slice(h * Q_BLOCK, (h + 1) * Q_BLOCK)
        m_old = m_ref[:, hs]
        m_new = jnp.maximum(m_old, jnp.max(t, axis=0, keepdims=True))
        m_ref[:, hs] = m_new
        return jnp.exp2(t - m_new).astype(BF16), jnp.exp2(m_old - m_new)

    def attend(keys, vt, logits_of):
        groups = [slice(g * HEAD_GROUP * Q_BLOCK, (g + 1) * HEAD_GROUP * Q_BLOCK)
                  for g in range(N_HEADS // HEAD_GROUP)]
        n_keys = keys.shape[0]
        for g, gs in enumerate(groups):
            s_ref[g, 0:n_keys, :] = lax.dot_general(keys, qa_ref[gs, 0:keys.shape[1]], nt,
                                                    preferred_element_type=F32)
        for g, gs in enumerate(groups):
            pa = [softmax_step(g * HEAD_GROUP + e,
                               logits_of(g * HEAD_GROUP + e,
                                         s_ref[g, 0:n_keys, e * Q_BLOCK:(e + 1) * Q_BLOCK]))
                  for e in range(HEAD_GROUP)]
            p = jnp.concatenate([x[0] for x in pa], axis=1)
            alpha = jnp.concatenate([x[1] for x in pa], axis=1)
            pv = jnp.dot(vt, p, preferred_element_type=F32)
            acc_ref[:, gs] = acc_ref[:, gs] * alpha + pv

    far_end = (j - 1) * Q_BLOCK
    def far_chunk(lo, n_keys):
        start = pl.multiple_of(jnp.minimum(lo, jnp.maximum(far_end - n_keys, 0)), LANES)
        pos = start + lax.broadcasted_iota(I32, (n_keys, Q_BLOCK), 0)
        sel = selected(key_ref[pl.ds(start, n_keys), :], pos) & (pos >= lo) & (pos < far_end)
        maskb = jnp.where(sel, 0.0, NEG).astype(BF16)
        kaug = jnp.concatenate([k_ref[pl.ds(start, n_keys), :], maskb], axis=1)
        attend(kaug, vt_ref[:, pl.ds(start, n_keys)], lambda h, s: s)

    n_big = jnp.maximum(far_end, 0) // FAR_BIG
    rest = jnp.maximum(far_end, 0) - n_big * FAR_BIG
    loop(n_big, lambda c: far_chunk(c * FAR_BIG, FAR_BIG))
    loop((rest + FAR_CHUNK - 1) // FAR_CHUNK,
         lambda c: far_chunk(n_big * FAR_BIG + c * FAR_CHUNK, FAR_CHUNK))
    m_ref[...] = m_ref[...] + cvec_ref[...]

    near_start = pl.multiple_of(jnp.maximum(j - 1, 0) * Q_BLOCK, LANES)
    first = jnp.where(j == 0, 1, 0)
    near_pos = near_start + lax.broadcasted_iota(I32, (NEAR_KEYS, Q_BLOCK), 0)
    sel = selected(key_ref[pl.ds(near_start, NEAR_KEYS), :], near_pos)
    maskb = jnp.where(sel, 0.0, NEG)
    k_near = k_ref[pl.ds(near_start, NEAR_KEYS), :]
    vt_near = vt_ref[:, pl.ds(near_start, NEAR_KEYS)]

    def near_logits(h, s):
        return s + near_ref[first, :, h * Q_BLOCK:(h + 1) * Q_BLOCK] + maskb

    attend(k_near, vt_near, near_logits)

    inv_l = 1.0 / acc_ref[HEAD_DIM:HEAD_DIM + 1, :]
    for h in range(N_HEADS):
        hs = slice(h * Q_BLOCK, (h + 1) * Q_BLOCK)
        o_ref[:, h * HEAD_DIM:(h + 1) * HEAD_DIM] = (
            acc_ref[0:HEAD_DIM, hs] * inv_l[:, hs]).T.astype(o_ref.dtype)


def _t5_bucket_static(n):
    max_exact = N_BUCKETS // 2
    nf = np.maximum(n, 1).astype(np.float32)
    large = max_exact + (np.log(nf / np.float32(max_exact)) / np.float32(math.log(MAX_DISTANCE / max_exact))
                         * np.float32(N_BUCKETS - max_exact)).astype(np.int32)
    large = np.minimum(large, N_BUCKETS - 1)
    return np.where(n < max_exact, n, large)


def _bias_tables(rel_bias, seq):
    buckets = _t5_bucket_static(np.arange(seq, dtype=np.int32))
    far = buckets[Q_BLOCK + 1:]
    assert (far == far[0]).all()
    bias2 = rel_bias.astype(F32) * LOG2E
    span, origin = 4 * NEAR_KEYS, 2 * NEAR_KEYS
    d = np.arange(span) - origin
    r = jnp.where(jnp.asarray(d >= 0)[None, :], bias2[buckets[np.clip(d, 0, seq - 1)]].T, NEG)
    y = jnp.tile(r, (1, NEAR_KEYS))[:, :NEAR_KEYS * (span - 1)].reshape(N_HEADS, NEAR_KEYS, span - 1)

    def table(off):
        t = y[:, :, origin + off:origin + off + Q_BLOCK]
        return jnp.transpose(t, (1, 0, 2)).reshape(NEAR_KEYS, N_HEADS * Q_BLOCK)

    near = jnp.stack([table(Q_BLOCK), table(0)])
    cvec = jnp.repeat(bias2[int(far[0])], Q_BLOCK).reshape(1, N_HEADS * Q_BLOCK)
    return near, cvec


def _attention(qi_hm, kidx, w_t, q_hm, k, v_t, near, cvec, batch, seq):
    m = batch * seq
    n_blk = seq // Q_BLOCK
    k_sel = min(TOPK_MAX, seq // 4)
    hq = N_HEADS * Q_BLOCK
    kern = functools.partial(
        _attn_kernel, k_sel=k_sel, idx_scale=IDX_DIM ** -0.5 * N_IDX_HEADS ** -0.5,
        pos_bits=seq.bit_length())
    blk = lambda b, j: (b * n_blk + j)
    v_t = jnp.concatenate([v_t, jnp.ones((VT_ROWS - HEAD_DIM, m), v_t.dtype)], axis=0)
    assert seq % COUNT_CHUNK == 0 and seq % FAR_CHUNK == 0
    eye = jnp.tile(jnp.eye(Q_BLOCK, dtype=BF16), (N_HEADS, 1))
    return pl.pallas_call(
        kern,
        grid=(batch, n_blk),
        in_specs=[pl.BlockSpec((N_IDX_HEADS, Q_BLOCK, IDX_DIM), lambda b, j: (0, blk(b, j), 0)),
                  pl.BlockSpec((seq, IDX_DIM), lambda b, j: (b, 0)),
                  pl.BlockSpec((N_IDX_HEADS, Q_BLOCK), lambda b, j: (0, blk(b, j))),
                  pl.BlockSpec((N_HEADS, Q_BLOCK, HEAD_DIM), lambda b, j: (0, blk(b, j), 0)),
                  pl.BlockSpec((seq, HEAD_DIM), lambda b, j: (b, 0)),
                  pl.BlockSpec((VT_ROWS, seq), lambda b, j: (0, b)),
                  pl.BlockSpec((2, NEAR_KEYS, hq), lambda b, j: (0, 0, 0)),
                  pl.BlockSpec((1, hq), lambda b, j: (0, 0)),
                  pl.BlockSpec((hq, Q_BLOCK), lambda b, j: (0, 0))],
        out_specs=pl.BlockSpec((Q_BLOCK, N_HEADS * HEAD_DIM), lambda b, j: (blk(b, j), 0)),
        out_shape=jax.ShapeDtypeStruct((m, N_HEADS * HEAD_DIM), BF16),
        scratch_shapes=[pltpu.VMEM((seq, Q_BLOCK), I32),
                        pltpu.VMEM((1, hq), F32),
                        pltpu.VMEM((VT_ROWS, hq), F32),
                        pltpu.VMEM((hq, HEAD_DIM + Q_BLOCK), BF16),
                        pltpu.VMEM((N_HEADS // HEAD_GROUP, FAR_BIG, HEAD_GROUP * Q_BLOCK), F32),
                        pltpu.VMEM((1, Q_BLOCK), I32)],
        compiler_params=_params(("parallel", "arbitrary")),
        name="dsa_attention",
    )(qi_hm, kidx, w_t, q_hm, k, v_t, near, cvec, eye)


def _gelu_tanh(x):
    c = -2.0 * math.sqrt(2.0 / math.pi) * LOG2E
    return x / (1.0 + jnp.exp2(x * (x * x * (0.044715 * c) + c)))


def _sgu_kernel(uv_ref, g_ref, b_ref, ws_ref, bs_ref, o_ref):
    uv = _gelu_tanh(uv_ref[...])
    u = uv[:, :GMLP_WIDTH]
    v = uv[:, GMLP_WIDTH:]
    mu = jnp.mean(v, axis=-1, keepdims=True)
    vc = v - mu
    var = jnp.mean(vc * vc, axis=-1, keepdims=True)
    vn = (vc * lax.rsqrt(var + EPS) * g_ref[...] + b_ref[...]).astype(BF16)
    r = lax.broadcasted_iota(I32, (CHUNK, CHUNK), 0)
    c = lax.broadcasted_iota(I32, (CHUNK, CHUNK), 1)
    tril = c <= r
    bs = bs_ref[...]
    for g in range(N_GROUPS):
        ws = jnp.where(tril, ws_ref[g], 0.0).astype(BF16)
        mixed = jnp.dot(ws, vn[:, g * GROUP_DIM:(g + 1) * GROUP_DIM],
                        preferred_element_type=F32) + bs[:, g:g + 1]
        o_ref[:, g * GROUP_DIM:(g + 1) * GROUP_DIM] = (
            u[:, g * GROUP_DIM:(g + 1) * GROUP_DIM] * mixed).astype(o_ref.dtype)


def _sgu(uv, ln_g, ln_b, w_s, b_s):
    m = uv.shape[0]
    bs_t = jnp.zeros((CHUNK, LANES), F32).at[:, :N_GROUPS].set(jnp.transpose(b_s))
    return pl.pallas_call(
        _sgu_kernel,
        grid=(m // CHUNK,),
        in_specs=[pl.BlockSpec((CHUNK, 2 * GMLP_WIDTH), lambda i: (i, 0)),
                  pl.BlockSpec((1, GMLP_WIDTH), lambda i: (0, 0)),
                  pl.BlockSpec((1, GMLP_WIDTH), lambda i: (0, 0)),
                  pl.BlockSpec((N_GROUPS, CHUNK, CHUNK), lambda i: (0, 0, 0)),
                  pl.BlockSpec((CHUNK, LANES), lambda i: (0, 0))],
        out_specs=pl.BlockSpec((CHUNK, GMLP_WIDTH), lambda i: (i, 0)),
        out_shape=jax.ShapeDtypeStruct((m, GMLP_WIDTH), BF16),
        compiler_params=_params(("parallel",)),
        name="chunked_sgu",
    )(uv, ln_g.reshape(1, -1), ln_b.reshape(1, -1), w_s, bs_t)


def _merge_kernel(h_ref, ya_ref, yb_ref, wga_ref, wgb_ref, wa_ref, wb_ref, o_ref):
    h = h_ref[...]
    ga = jax.nn.sigmoid(jnp.dot(h, wga_ref[...], preferred_element_type=F32))
    a = jnp.dot(ya_ref[...], wa_ref[...], preferred_element_type=F32)
    out = ga * a
    gb = jax.nn.sigmoid(jnp.dot(h, wgb_ref[...], preferred_element_type=F32))
    b = jnp.dot(yb_ref[...], wb_ref[...], preferred_element_type=F32)
    o_ref[...] = (out + gb * b).astype(o_ref.dtype)


def _merge(h, ya, yb, wga, wgb, wa, wb):
    tm, tn = TILES["branch_merge"]
    m, d = h.shape
    n = wga.shape[1]
    ka, kb = ya.shape[1], yb.shape[1]
    row = lambda i, j: (i, 0)
    colw = lambda i, j: (0, j)
    return pl.pallas_call(
        _merge_kernel,
        grid=(m // tm, n // tn),
        in_specs=[pl.BlockSpec((tm, d), row), pl.BlockSpec((tm, ka), row), pl.BlockSpec((tm, kb), row),
                  pl.BlockSpec((d, tn), colw), pl.BlockSpec((d, tn), colw),
                  pl.BlockSpec((ka, tn), colw), pl.BlockSpec((kb, tn), colw)],
        out_specs=pl.BlockSpec((tm, tn), lambda i, j: (i, j)),
        out_shape=jax.ShapeDtypeStruct((m, n), BF16),
        compiler_params=_params(("parallel", "arbitrary")),
        name="branch_merge",
    )(h, ya, yb, wga, wgb, wa, wb)


def _ple_kernel(p_ref, w_ref, g_ref, o_ref):
    acc = jnp.dot(p_ref[...].astype(BF16), w_ref[...], preferred_element_type=F32)
    ms = jnp.mean(acc * acc, axis=-1, keepdims=True)
    o_ref[...] = (acc * lax.rsqrt(ms + EPS) * g_ref[...]).astype(o_ref.dtype)


def _ple(p, w, g, tm=256):
    m, dp = p.shape
    d = w.shape[1]
    return pl.pallas_call(
        _ple_kernel,
        grid=(m // tm,),
        in_specs=[pl.BlockSpec((tm, dp), lambda i: (i, 0)),
                  pl.BlockSpec((dp, d), lambda i: (0, 0)),
                  pl.BlockSpec((1, d), lambda i: (0, 0))],
        out_specs=pl.BlockSpec((tm, d), lambda i: (i, 0)),
        out_shape=jax.ShapeDtypeStruct((m, d), F32),
        compiler_params=_params(("parallel",)),
        name="ple_embed",
    )(p, w, g.reshape(1, d))


def kernel(x, p, w_in, q_norm_g, k_norm_g, rel_bias, sgu_ln_g, sgu_ln_b, sgu_w, sgu_b, w_branch_a, w_branch_b, w_out, norm_mix_g, norm_ffn_g, w_gate_ffn, w_up_ffn, w_down_ffn, w_ple, ple_norm_g, w_ple_gate, norm_ple_g):
    batch, seq, d_model = x.shape
    depth = w_in.shape[0]
    m = batch * seq
    a_width = N_HEADS * HEAD_DIM
    sizes = (a_width, HEAD_DIM, HEAD_DIM, N_IDX_HEADS * IDX_DIM, IDX_DIM, N_IDX_HEADS,
             2 * GMLP_WIDTH, d_model, d_model)
    offs = np.concatenate([[0], np.cumsum(sizes)])
    near, cvec = _bias_tables(rel_bias, seq)

    xf = x.reshape(m, d_model)
    for i in range(depth):
        seg = lambda s: w_in[i][:, int(offs[s]):int(offs[s + 1])].astype(BF16)
        w_q, w_k, w_v, w_qi, w_ki, w_wi, w_uv, w_ga, w_gb = [seg(s) for s in range(9)]
        w_small = jnp.concatenate(
            [w_k, w_v, w_ki, jnp.pad(w_wi, ((0, 0), (0, LANES - N_IDX_HEADS)))], axis=1)

        h, k_, v_, kidx, widx = _norm_small_proj(xf, norm_mix_g[i], w_small, k_norm_g[i])
        q_hm = _headproj(h, w_q, q_norm_g[i], "q_proj", post_scale=HEAD_DIM ** -0.5 * LOG2E)
        qi_hm = _headproj(h, w_qi, None, "q_idx_proj")
        uv = _matmul(h, [w_uv], [], lambda a, e: a[0], F32, "uv_proj")
        y_a = _attention(qi_hm, kidx, jnp.transpose(widx[:, :N_IDX_HEADS]), q_hm, k_,
                         jnp.transpose(v_), near, cvec, batch, seq)
        y_b = _sgu(uv, sgu_ln_g[i], sgu_ln_b[i], sgu_w[i], sgu_b[i])
        merged = _merge(h, y_a, y_b, w_ga, w_gb,
                        w_branch_a[i].astype(BF16), w_branch_b[i].astype(BF16))
        x1 = _matmul(merged, [w_out[i].astype(BF16)], [xf],
                     lambda a, e: e[0] + a[0], F32, "out_proj")

        h2 = _rmsnorm(x1, norm_ffn_g[i])
        t = _matmul(h2, [w_gate_ffn[i].astype(BF16), w_up_ffn[i].astype(BF16)], [],
                    lambda a, e: jax.nn.silu(a[0]) * a[1], BF16, "ffn_gate_up")
        x2, x2g, ssq2 = _res_matmul(t, w_down_ffn[i].astype(BF16), x1, norm_ple_g[i], "ffn_down")

        pe = _ple(p[i].reshape(m, -1), w_ple[i].astype(BF16), ple_norm_g[i])
        xf = _matmul(x2g, [w_ple_gate[i].astype(BF16)], [x2, pe],
                     lambda a, e: e[0] + jax.nn.sigmoid(a[0]) * e[1], F32, "ple_gate", row_ssq=ssq2)
    return xf.reshape(batch, seq, d_model)
```

```python
import functools
import math

import numpy as np
import jax
import jax.numpy as jnp
from jax import lax
from jax.experimental import pallas as pl
from jax.experimental.pallas import tpu as pltpu

F32 = jnp.float32
BF16 = jnp.bfloat16
I32 = jnp.int32

N_HEADS = 16
HEAD_DIM = 128
N_IDX_HEADS = 32
IDX_DIM = 128
TOPK_MAX = 256
Q_BLOCK = 128
N_BUCKETS = 32
MAX_DISTANCE = 128
GMLP_WIDTH = 2048
N_GROUPS = 8
GROUP_DIM = GMLP_WIDTH // N_GROUPS
CHUNK = 128
EPS = 1e-6

LANES = 128
SUBLANES = 8
VMEM_LIMIT = 56 * 1024 * 1024
NEG = -(2.0 ** 100)
INT_MIN = -2 ** 31
INT_MAX = 2 ** 31 - 1
KEY_NEG_INF = -2139095041
LOG2E = math.log2(math.e)

SCORE_CHUNK = 4 * LANES
COUNT_CHUNK = SCORE_CHUNK
FAR_CHUNK = 4 * LANES
FAR_BIG = 2 * FAR_CHUNK
NEAR_KEYS = 2 * Q_BLOCK
HEAD_GROUP = 4
VT_ROWS = HEAD_DIM + 16
NORM_ROWS = 128
MM_SUB_ROWS = 256

TILES = {
    "norm_kv_idx_proj": (512, 4 * LANES),
    "q_proj": (1024, 1024),
    "q_idx_proj": (1024, 1024),
    "uv_proj": (1024, 1024),
    "branch_merge": (512, 512),
    "out_proj": (1024, 512),
    "ffn_gate_up": (2048, 256),
    "ffn_down": (512, 512),
    "ple_gate": (1024, 512),
}


def _params(sem):
    return pltpu.CompilerParams(dimension_semantics=sem, vmem_limit_bytes=VMEM_LIMIT)


def _rmsnorm_kernel(x_ref, g_ref, o_ref):
    x = x_ref[...]
    ms = jnp.mean(x * x, axis=-1, keepdims=True)
    o_ref[...] = (x * lax.rsqrt(ms + EPS) * g_ref[...]).astype(o_ref.dtype)


def _rmsnorm(x, g, tm=256):
    m, d = x.shape
    return pl.pallas_call(
        _rmsnorm_kernel,
        grid=(m // tm,),
        in_specs=[pl.BlockSpec((tm, d), lambda i: (i, 0)),
                  pl.BlockSpec((1, d), lambda i: (0, 0))],
        out_specs=pl.BlockSpec((tm, d), lambda i: (i, 0)),
        out_shape=jax.ShapeDtypeStruct((m, d), BF16),
        compiler_params=_params(("parallel",)),
        name="rmsnorm",
    )(x, g.reshape(1, d))


def _lane_tiles(x):
    return [x[:, c * LANES:(c + 1) * LANES] for c in range(x.shape[1] // LANES)]


def _mm_kernel(*refs, n_w, n_x, epi, norm_dim):
    a_ref = refs[0]
    w_refs = refs[1:1 + n_w]
    x_refs = refs[1 + n_w:1 + n_w + n_x]
    if norm_dim:
        ssq_ref, o_ref, r_ref = refs[1 + n_w + n_x:]

        @pl.when(pl.program_id(1) == 0)
        def _():
            ms = jnp.sum(ssq_ref[...], axis=-1, keepdims=True) * (1.0 / norm_dim)
            r_ref[...] = jnp.broadcast_to(lax.rsqrt(ms + EPS), r_ref.shape)
    else:
        o_ref = refs[1 + n_w + n_x]
    for s in range(a_ref.shape[0] // MM_SUB_ROWS):
        rows = slice(s * MM_SUB_ROWS, (s + 1) * MM_SUB_ROWS)
        a = a_ref[rows, :]
        accs = [jnp.dot(a, w[...], preferred_element_type=F32) for w in w_refs]
        if norm_dim:
            accs = [acc * r_ref[rows, :] for acc in accs]
        o_ref[rows, :] = epi(accs, [x[rows, :] for x in x_refs]).astype(o_ref.dtype)


def _matmul(a, ws, extras, epi, out_dtype, name, row_ssq=None):
    tm, tn = TILES[name]
    m, k = a.shape
    n = ws[0].shape[1]
    normed = row_ssq is not None
    kern = functools.partial(_mm_kernel, n_w=len(ws), n_x=len(extras), epi=epi,
                             norm_dim=k if normed else 0)
    return pl.pallas_call(
        kern,
        grid=(m // tm, n // tn),
        in_specs=([pl.BlockSpec((tm, k), lambda i, j: (i, 0))]
                  + [pl.BlockSpec((k, tn), lambda i, j: (0, j)) for _ in ws]
                  + [pl.BlockSpec((tm, tn), lambda i, j: (i, j)) for _ in extras]
                  + ([pl.BlockSpec((tm, row_ssq.shape[1]), lambda i, j: (i, 0))] if normed else [])),
        out_specs=pl.BlockSpec((tm, tn), lambda i, j: (i, j)),
        out_shape=jax.ShapeDtypeStruct((m, n), out_dtype),
        scratch_shapes=[pltpu.VMEM((tm, tn), F32)] if normed else [],
        compiler_params=_params(("parallel", "arbitrary")),
        name=name,
    )(a, *ws, *extras, *([row_ssq] if normed else []))


def _res_mm_kernel(a_ref, w_ref, r_ref, g_ref, o_ref, og_ref, ssq_ref, *, nk):
    def sub_blocks(base_ref, last):
        for s in range(a_ref.shape[0] // MM_SUB_ROWS):
            rows = slice(s * MM_SUB_ROWS, (s + 1) * MM_SUB_ROWS)
            x = base_ref[rows, :] + jnp.dot(a_ref[rows, :], w_ref[...],
                                            preferred_element_type=F32)
            o_ref[rows, :] = x
            if last:
                og_ref[rows, :] = (x * g_ref[...]).astype(og_ref.dtype)
                ssq_ref[rows, :] = functools.reduce(lambda u, v: u + v, _lane_tiles(x * x))

    if nk == 1:
        sub_blocks(r_ref, True)
    else:
        @pl.when(pl.program_id(2) == 0)
        def _():
            sub_blocks(r_ref, False)

        @pl.when(pl.program_id(2) == 1)
        def _():
            sub_blocks(o_ref, True)


def _res_matmul(a, w, res, g_next, name, nk=1):
    tm, tn = TILES[name]
    m, k = a.shape
    n = w.shape[1]
    assert nk in (1, 2) and k % nk == 0
    tk = k // nk
    return pl.pallas_call(
        functools.partial(_res_mm_kernel, nk=nk),
        grid=(m // tm, n // tn, nk),
        in_specs=[pl.BlockSpec((tm, tk), lambda i, j, kk: (i, kk)),
                  pl.BlockSpec((tk, tn), lambda i, j, kk: (kk, j)),
                  pl.BlockSpec((tm, tn), lambda i, j, kk: (i, j)),
                  pl.BlockSpec((1, tn), lambda i, j, kk: (0, j))],
        out_specs=[pl.BlockSpec((tm, tn), lambda i, j, kk: (i, j)),
                   pl.BlockSpec((tm, tn), lambda i, j, kk: (i, j)),
                   pl.BlockSpec((tm, LANES), lambda i, j, kk: (i, j))],
        out_shape=[jax.ShapeDtypeStruct((m, n), F32),
                   jax.ShapeDtypeStruct((m, n), BF16),
                   jax.ShapeDtypeStruct((m, (n // tn) * LANES), F32)],
        compiler_params=_params(("parallel", "arbitrary", "arbitrary")),
        name=name,
    )(a, w, res, g_next.reshape(1, n))


def _headproj_kernel(*refs, heads, norm, post_scale):
    a_ref, w_ref = refs[0], refs[1]
    o_ref = refs[-1]
    acc = jnp.dot(a_ref[...], w_ref[...], preferred_element_type=F32)
    for hh in range(heads):
        blk = acc[:, hh * LANES:(hh + 1) * LANES]
        if norm:
            ms = jnp.mean(blk * blk, axis=-1, keepdims=True)
            blk = blk * lax.rsqrt(ms + EPS) * refs[2][...]
        if post_scale is not None:
            blk = blk * post_scale
        o_ref[hh] = blk.astype(o_ref.dtype)


def _headproj(h, w, g, name, post_scale=None):
    tm, tn = TILES[name]
    m, k = h.shape
    n_heads = w.shape[1] // LANES
    heads = tn // LANES
    norm = g is not None
    extra_specs = [pl.BlockSpec((1, LANES), lambda i, j: (0, 0))] if norm else []
    extra_args = [g.reshape(1, LANES)] if norm else []
    return pl.pallas_call(
        functools.partial(_headproj_kernel, heads=heads, norm=norm, post_scale=post_scale),
        grid=(m // tm, w.shape[1] // tn),
        in_specs=[pl.BlockSpec((tm, k), lambda i, j: (i, 0)),
                  pl.BlockSpec((k, tn), lambda i, j: (0, j))] + extra_specs,
        out_specs=pl.BlockSpec((heads, tm, LANES), lambda i, j: (j, i, 0)),
        out_shape=jax.ShapeDtypeStruct((n_heads, m, LANES), BF16),
        compiler_params=_params(("parallel", "arbitrary")),
        name=name,
    )(h, w, *extra_args)


def _small_kernel(x_ref, gx_ref, w_ref, g_ref, h_ref, k_ref, v_ref, ki_ref, wi_ref):
    def norm_rows(r, carry):
        rows = pl.ds(pl.multiple_of(r * NORM_ROWS, NORM_ROWS), NORM_ROWS)
        x = x_ref[rows, :]
        ms = jnp.mean(x * x, axis=-1, keepdims=True)
        h_ref[rows, :] = (x * lax.rsqrt(ms + EPS) * gx_ref[...]).astype(h_ref.dtype)
        return carry

    lax.fori_loop(0, x_ref.shape[0] // NORM_ROWS, norm_rows, 0)
    acc = jnp.dot(h_ref[...], w_ref[...], preferred_element_type=F32)
    kk = acc[:, 0:LANES]
    ms = jnp.mean(kk * kk, axis=-1, keepdims=True)
    k_ref[...] = (kk * lax.rsqrt(ms + EPS) * g_ref[...]).astype(k_ref.dtype)
    v_ref[...] = acc[:, LANES:2 * LANES].astype(v_ref.dtype)
    ki_ref[...] = acc[:, 2 * LANES:3 * LANES].astype(ki_ref.dtype)
    wi_ref[...] = acc[:, 3 * LANES:4 * LANES]


def _norm_small_proj(x, gx, w_small, g):
    tm, _ = TILES["norm_kv_idx_proj"]
    m, k = x.shape
    row = lambda i: (i, 0)
    return pl.pallas_call(
        _small_kernel,
        grid=(m // tm,),
        in_specs=[pl.BlockSpec((tm, k), row),
                  pl.BlockSpec((1, k), lambda i: (0, 0)),
                  pl.BlockSpec((k, 4 * LANES), lambda i: (0, 0)),
                  pl.BlockSpec((1, LANES), lambda i: (0, 0))],
        out_specs=[pl.BlockSpec((tm, k), row)] + [pl.BlockSpec((tm, LANES), row)] * 4,
        out_shape=[jax.ShapeDtypeStruct((m, k), BF16)]
                  + [jax.ShapeDtypeStruct((m, LANES), BF16)] * 3
                  + [jax.ShapeDtypeStruct((m, LANES), F32)],
        compiler_params=_params(("parallel",)),
        name="norm_kv_idx_proj",
    )(x, gx.reshape(1, k), w_small, g.reshape(1, LANES))


def _attn_kernel(qi_ref, kidx_ref, wt_ref, q_ref, k_ref, vt_ref, near_ref, cvec_ref, eye_ref, o_ref,
                 key_ref, m_ref, acc_ref, qa_ref, s_ref, lim_ref,
                 *, k_sel, idx_scale, pos_bits):
    j = pl.program_id(1)
    nt = (((1,), (1,)), ((), ()))
    n_chunks = (j + 4) // 4

    def score_keys(start, n_keys):
        start = pl.multiple_of(start, SCORE_CHUNK)
        kc = kidx_ref[pl.ds(start, n_keys), :]
        acc = jnp.zeros((n_keys, Q_BLOCK), F32)
        for hp in range(N_IDX_HEADS // 2):
            qpair = qi_ref[2 * hp:2 * hp + 2].reshape(2 * Q_BLOCK, IDX_DIM)
            d = lax.dot_general(kc, qpair, nt, preferred_element_type=F32)
            acc = acc + jnp.maximum(d[:, :Q_BLOCK], 0.0) * wt_ref[2 * hp:2 * hp + 1, :]
            acc = acc + jnp.maximum(d[:, Q_BLOCK:], 0.0) * wt_ref[2 * hp + 1:2 * hp + 2, :]
        s = acc * idx_scale
        key_pos = start + lax.broadcasted_iota(I32, (n_keys, Q_BLOCK), 0)
        q_pos = j * Q_BLOCK + lax.broadcasted_iota(I32, (n_keys, Q_BLOCK), 1)
        s = jnp.where(key_pos <= q_pos, s, -jnp.inf)
        bits = lax.bitcast_convert_type(s, I32)
        key_ref[pl.ds(start, n_keys), :] = bits ^ ((bits >> 31) & 0x7FFFFFFF)

    def loop(n, body):
        lax.fori_loop(0, n, lambda c, carry: (body(c), carry)[1], 0)

    loop(n_chunks // 2, lambda c: score_keys(c * (2 * SCORE_CHUNK), 2 * SCORE_CHUNK))
    loop(n_chunks % 2, lambda c: score_keys((n_chunks - 1) * SCORE_CHUNK, SCORE_CHUNK))

    n_count = n_chunks
    count_row = lax.broadcasted_iota(I32, (COUNT_CHUNK, Q_BLOCK), 0)

    def count_keys(pred):
        def body(c, a):
            start = pl.multiple_of(c * COUNT_CHUNK, COUNT_CHUNK)
            hit = jnp.where(pred(key_ref[pl.ds(start, COUNT_CHUNK), :], start + count_row), 1, 0)
            return a + jnp.sum(hit.reshape(COUNT_CHUNK // SUBLANES, SUBLANES, Q_BLOCK), axis=0)

        a = lax.fori_loop(0, n_count, body, jnp.zeros((SUBLANES, Q_BLOCK), I32))
        return jnp.sum(a, axis=0, keepdims=True)

    def bit_body(i, carry):
        tu, cnt_tu = carry
        cu = tu | jnp.left_shift(jnp.int32(1), 31 - i)
        cs = cu ^ INT_MIN
        cnt = count_keys(lambda keys, pos: keys >= cs)
        take = cnt >= k_sel
        return jnp.where(take, cu, tu), jnp.where(take, cnt, cnt_tu)

    tu, cnt_ge = lax.fori_loop(
        0, 32, bit_body,
        (jnp.zeros((1, Q_BLOCK), I32), jnp.full((1, Q_BLOCK), n_count * COUNT_CHUNK, I32)))
    thr = tu ^ INT_MIN

    tied = (cnt_ge > k_sel) & (thr > KEY_NEG_INF)
    lim_ref[...] = jnp.full(lim_ref.shape, INT_MAX, I32)

    @pl.when(jnp.max(jnp.where(tied, 1, 0)) > 0)
    def _():
        n_tied_kept = k_sel - count_keys(lambda keys, pos: keys > thr)

        def pos_body(i, x):
            cand = x | jnp.left_shift(jnp.int32(1), pos_bits - 1 - i)
            cnt = count_keys(lambda keys, pos: (keys == thr) & (pos < cand))
            return jnp.where(cnt <= n_tied_kept, cand, x)

        x = lax.fori_loop(0, pos_bits, pos_body, jnp.zeros((1, Q_BLOCK), I32))
        lim_ref[...] = jnp.where(tied, x, INT_MAX)

    pos_lim = lim_ref[...]

    def selected(keys, pos):
        return (keys > thr) | ((keys == thr) & (pos < pos_lim))

    m_ref[...] = jnp.full(m_ref.shape, NEG, F32)
    acc_ref[...] = jnp.zeros(acc_ref.shape, F32)
    qa_ref[:, :HEAD_DIM] = q_ref[...].reshape(N_HEADS * Q_BLOCK, HEAD_DIM)
    qa_ref[:, HEAD_DIM:] = eye_ref[...]

    def softmax_step(h, t):
        hs = slice(h * Q_BLOCK, (h + 1) * Q_BLOCK)
        m_old = m_ref[:, hs]
        m_new = jnp.maximum(m_old, jnp.max(t, axis=0, keepdims=True))
        m_ref[:, hs] = m_new
        return jnp.exp2(t - m_new).astype(BF16), jnp.exp2(m_old - m_new)

    def attend(keys, vt, logits_of):
        groups = [slice(g * HEAD_GROUP * Q_BLOCK, (g + 1) * HEAD_GROUP * Q_BLOCK)
                  for g in range(N_HEADS // HEAD_GROUP)]
        n_keys = keys.shape[0]
        for g, gs in enumerate(groups):
            s_ref[g, 0:n_keys, :] = lax.dot_general(keys, qa_ref[gs, 0:keys.shape[1]], nt,
                                                    preferred_element_type=F32)
        for g, gs in enumerate(groups):
            pa = [softmax_step(g * HEAD_GROUP + e,
                               logits_of(g * HEAD_GROUP + e,
                                         s_ref[g, 0:n_keys, e * Q_BLOCK:(e + 1) * Q_BLOCK]))
                  for e in range(HEAD_GROUP)]
            p = jnp.concatenate([x[0] for x in pa], axis=1)
            alpha = jnp.concatenate([x[1] for x in pa], axis=1)
            pv = jnp.dot(vt, p, preferred_element_type=F32)
            acc_ref[:, gs] = acc_ref[:, gs] * alpha + pv

    far_end = (j - 1) * Q_BLOCK
    def far_chunk(lo, n_keys):
        start = pl.multiple_of(jnp.minimum(lo, jnp.maximum(far_end - n_keys, 0)), LANES)
        pos = start + lax.broadcasted_iota(I32, (n_keys, Q_BLOCK), 0)
        sel = selected(key_ref[pl.ds(start, n_keys), :], pos) & (pos >= lo) & (pos < far_end)
        maskb = jnp.where(sel, 0.0, NEG).astype(BF16)
        kaug = jnp.concatenate([k_ref[pl.ds(start, n_keys), :], maskb], axis=1)
        attend(kaug, vt_ref[:, pl.ds(start, n_keys)], lambda h, s: s)

    n_big = jnp.maximum(far_end, 0) // FAR_BIG
    rest = jnp.maximum(far_end, 0) - n_big * FAR_BIG
    loop(n_big, lambda c: far_chunk(c * FAR_BIG, FAR_BIG))
    loop((rest + FAR_CHUNK - 1) // FAR_CHUNK,
         lambda c: far_chunk(n_big * FAR_BIG + c * FAR_CHUNK, FAR_CHUNK))
    m_ref[...] = m_ref[...] + cvec_ref[...]

    near_start = pl.multiple_of(jnp.maximum(j - 1, 0) * Q_BLOCK, LANES)
    first = jnp.where(j == 0, 1, 0)
    near_pos = near_start + lax.broadcasted_iota(I32, (NEAR_KEYS, Q_BLOCK), 0)
    sel = selected(key_ref[pl.ds(near_start, NEAR_KEYS), :], near_pos)
    maskb = jnp.where(sel, 0.0, NEG)
    k_near = k_ref[pl.ds(near_start, NEAR_KEYS), :]
    vt_near = vt_ref[:, pl.ds(near_start, NEAR_KEYS)]

    def near_logits(h, s):
        return s + near_ref[first, :, h * Q_BLOCK:(h + 1) * Q_BLOCK] + maskb

    attend(k_near, vt_near, near_logits)

    inv_l = 1.0 / acc_ref[HEAD_DIM:HEAD_DIM + 1, :]
    for h in range(N_HEADS):
        hs = slice(h * Q_BLOCK, (h + 1) * Q_BLOCK)
        o_ref[:, h * HEAD_DIM:(h + 1) * HEAD_DIM] = (
            acc_ref[0:HEAD_DIM, hs] * inv_l[:, hs]).T.astype(o_ref.dtype)


def _t5_bucket_static(n):
    max_exact = N_BUCKETS // 2
    nf = np.maximum(n, 1).astype(np.float32)
    large = max_exact + (np.log(nf / np.float32(max_exact)) / np.float32(math.log(MAX_DISTANCE / max_exact))
                         * np.float32(N_BUCKETS - max_exact)).astype(np.int32)
    large = np.minimum(large, N_BUCKETS - 1)
    return np.where(n < max_exact, n, large)


def _bias_tables(rel_bias, seq):
    buckets = _t5_bucket_static(np.arange(seq, dtype=np.int32))
    far = buckets[Q_BLOCK + 1:]
    assert (far == far[0]).all()
    bias2 = rel_bias.astype(F32) * LOG2E
    span, origin = 4 * NEAR_KEYS, 2 * NEAR_KEYS
    d = np.arange(span) - origin
    r = jnp.where(jnp.asarray(d >= 0)[None, :], bias2[buckets[np.clip(d, 0, seq - 1)]].T, NEG)
    y = jnp.tile(r, (1, NEAR_KEYS))[:, :NEAR_KEYS * (span - 1)].reshape(N_HEADS, NEAR_KEYS, span - 1)

    def table(off):
        t = y[:, :, origin + off:origin + off + Q_BLOCK]
        return jnp.transpose(t, (1, 0, 2)).reshape(NEAR_KEYS, N_HEADS * Q_BLOCK)

    near = jnp.stack([table(Q_BLOCK), table(0)])
    cvec = jnp.repeat(bias2[int(far[0])], Q_BLOCK).reshape(1, N_HEADS * Q_BLOCK)
    return near, cvec


def _attention(qi_hm, kidx, w_t, q_hm, k, v_t, near, cvec, batch, seq):
    m = batch * seq
    n_blk = seq // Q_BLOCK
    k_sel = min(TOPK_MAX, seq // 4)
    hq = N_HEADS * Q_BLOCK
    kern = functools.partial(
        _attn_kernel, k_sel=k_sel, idx_scale=IDX_DIM ** -0.5 * N_IDX_HEADS ** -0.5,
        pos_bits=seq.bit_length())
    blk = lambda b, j: (b * n_blk + j)
    v_t = jnp.concatenate([v_t, jnp.ones((VT_ROWS - HEAD_DIM, m), v_t.dtype)], axis=0)
    assert seq % COUNT_CHUNK == 0 and seq % FAR_CHUNK == 0
    eye = jnp.tile(jnp.eye(Q_BLOCK, dtype=BF16), (N_HEADS, 1))
    return pl.pallas_call(
        kern,
        grid=(batch, n_blk),
        in_specs=[pl.BlockSpec((N_IDX_HEADS, Q_BLOCK, IDX_DIM), lambda b, j: (0, blk(b, j), 0)),
                  pl.BlockSpec((seq, IDX_DIM), lambda b, j: (b, 0)),
                  pl.BlockSpec((N_IDX_HEADS, Q_BLOCK), lambda b, j: (0, blk(b, j))),
                  pl.BlockSpec((N_HEADS, Q_BLOCK, HEAD_DIM), lambda b, j: (0, blk(b, j), 0)),
                  pl.BlockSpec((seq, HEAD_DIM), lambda b, j: (b, 0)),
                  pl.BlockSpec((VT_ROWS, seq), lambda b, j: (0, b)),
                  pl.BlockSpec((2, NEAR_KEYS, hq), lambda b, j: (0, 0, 0)),
                  pl.BlockSpec((1, hq), lambda b, j: (0, 0)),
                  pl.BlockSpec((hq, Q_BLOCK), lambda b, j: (0, 0))],
        out_specs=pl.BlockSpec((Q_BLOCK, N_HEADS * HEAD_DIM), lambda b, j: (blk(b, j), 0)),
        out_shape=jax.ShapeDtypeStruct((m, N_HEADS * HEAD_DIM), BF16),
        scratch_shapes=[pltpu.VMEM((seq, Q_BLOCK), I32),
                        pltpu.VMEM((1, hq), F32),
                        pltpu.VMEM((VT_ROWS, hq), F32),
                        pltpu.VMEM((hq, HEAD_DIM + Q_BLOCK), BF16),
                        pltpu.VMEM((N_HEADS // HEAD_GROUP, FAR_BIG, HEAD_GROUP * Q_BLOCK), F32),
                        pltpu.VMEM((1, Q_BLOCK), I32)],
        compiler_params=_params(("parallel", "arbitrary")),
        name="dsa_attention",
    )(qi_hm, kidx, w_t, q_hm, k, v_t, near, cvec, eye)


def _gelu_tanh(x):
    c = -2.0 * math.sqrt(2.0 / math.pi) * LOG2E
    return x / (1.0 + jnp.exp2(x * (x * x * (0.044715 * c) + c)))


def _sgu_kernel(uv_ref, g_ref, b_ref, ws_ref, bs_ref, o_ref):
    uv = _gelu_tanh(uv_ref[...])
    u = uv[:, :GMLP_WIDTH]
    v = uv[:, GMLP_WIDTH:]
    mu = jnp.mean(v, axis=-1, keepdims=True)
    vc = v - mu
    var = jnp.mean(vc * vc, axis=-1, keepdims=True)
    vn = (vc * lax.rsqrt(var + EPS) * g_ref[...] + b_ref[...]).astype(BF16)
    r = lax.broadcasted_iota(I32, (CHUNK, CHUNK), 0)
    c = lax.broadcasted_iota(I32, (CHUNK, CHUNK), 1)
    tril = c <= r
    bs = bs_ref[...]
    for g in range(N_GROUPS):
        ws = jnp.where(tril, ws_ref[g], 0.0).astype(BF16)
        mixed = jnp.dot(ws, vn[:, g * GROUP_DIM:(g + 1) * GROUP_DIM],
                        preferred_element_type=F32) + bs[:, g:g + 1]
        o_ref[:, g * GROUP_DIM:(g + 1) * GROUP_DIM] = (
            u[:, g * GROUP_DIM:(g + 1) * GROUP_DIM] * mixed).astype(o_ref.dtype)


def _sgu(uv, ln_g, ln_b, w_s, b_s):
    m = uv.shape[0]
    bs_t = jnp.zeros((CHUNK, LANES), F32).at[:, :N_GROUPS].set(jnp.transpose(b_s))
    return pl.pallas_call(
        _sgu_kernel,
        grid=(m // CHUNK,),
        in_specs=[pl.BlockSpec((CHUNK, 2 * GMLP_WIDTH), lambda i: (i, 0)),
                  pl.BlockSpec((1, GMLP_WIDTH), lambda i: (0, 0)),
                  pl.BlockSpec((1, GMLP_WIDTH), lambda i: (0, 0)),
                  pl.BlockSpec((N_GROUPS, CHUNK, CHUNK), lambda i: (0, 0, 0)),
                  pl.BlockSpec((CHUNK, LANES), lambda i: (0, 0))],
        out_specs=pl.BlockSpec((CHUNK, GMLP_WIDTH), lambda i: (i, 0)),
        out_shape=jax.ShapeDtypeStruct((m, GMLP_WIDTH), BF16),
        compiler_params=_params(("parallel",)),
        name="chunked_sgu",
    )(uv, ln_g.reshape(1, -1), ln_b.reshape(1, -1), w_s, bs_t)


def _merge_kernel(h_ref, ya_ref, yb_ref, wga_ref, wgb_ref, wa_ref, wb_ref, o_ref):
    h = h_ref[...]
    ga = jax.nn.sigmoid(jnp.dot(h, wga_ref[...], preferred_element_type=F32))
    a = jnp.dot(ya_ref[...], wa_ref[...], preferred_element_type=F32)
    out = ga * a
    gb = jax.nn.sigmoid(jnp.dot(h, wgb_ref[...], preferred_element_type=F32))
    b = jnp.dot(yb_ref[...], wb_ref[...], preferred_element_type=F32)
    o_ref[...] = (out + gb * b).astype(o_ref.dtype)


def _merge(h, ya, yb, wga, wgb, wa, wb):
    tm, tn = TILES["branch_merge"]
    m, d = h.shape
    n = wga.shape[1]
    ka, kb = ya.shape[1], yb.shape[1]
    row = lambda i, j: (i, 0)
    colw = lambda i, j: (0, j)
    return pl.pallas_call(
        _merge_kernel,
        grid=(m // tm, n // tn),
        in_specs=[pl.BlockSpec((tm, d), row), pl.BlockSpec((tm, ka), row), pl.BlockSpec((tm, kb), row),
                  pl.BlockSpec((d, tn), colw), pl.BlockSpec((d, tn), colw),
                  pl.BlockSpec((ka, tn), colw), pl.BlockSpec((kb, tn), colw)],
        out_specs=pl.BlockSpec((tm, tn), lambda i, j: (i, j)),
        out_shape=jax.ShapeDtypeStruct((m, n), BF16),
        compiler_params=_params(("parallel", "arbitrary")),
        name="branch_merge",
    )(h, ya, yb, wga, wgb, wa, wb)


def _ple_kernel(p_ref, w_ref, g_ref, o_ref):
    acc = jnp.dot(p_ref[...].astype(BF16), w_ref[...], preferred_element_type=F32)
    ms = jnp.mean(acc * acc, axis=-1, keepdims=True)
    o_ref[...] = (acc * lax.rsqrt(ms + EPS) * g_ref[...]).astype(o_ref.dtype)


def _ple(p, w, g, tm=256):
    m, dp = p.shape
    d = w.shape[1]
    return pl.pallas_call(
        _ple_kernel,
        grid=(m // tm,),
        in_specs=[pl.BlockSpec((tm, dp), lambda i: (i, 0)),
                  pl.BlockSpec((dp, d), lambda i: (0, 0)),
                  pl.BlockSpec((1, d), lambda i: (0, 0))],
        out_specs=pl.BlockSpec((tm, d), lambda i: (i, 0)),
        out_shape=jax.ShapeDtypeStruct((m, d), F32),
        compiler_params=_params(("parallel",)),
        name="ple_embed",
    )(p, w, g.reshape(1, d))


def kernel(x, p, w_in, q_norm_g, k_norm_g, rel_bias, sgu_ln_g, sgu_ln_b, sgu_w, sgu_b, w_branch_a, w_branch_b, w_out, norm_mix_g, norm_ffn_g, w_gate_ffn, w_up_ffn, w_down_ffn, w_ple, ple_norm_g, w_ple_gate, norm_ple_g):
    batch, seq, d_model = x.shape
    depth = w_in.shape[0]
    m = batch * seq
    a_width = N_HEADS * HEAD_DIM
    sizes = (a_width, HEAD_DIM, HEAD_DIM, N_IDX_HEADS * IDX_DIM, IDX_DIM, N_IDX_HEADS,
             2 * GMLP_WIDTH, d_model, d_model)
    offs = np.concatenate([[0], np.cumsum(sizes)])
    near, cvec = _bias_tables(rel_bias, seq)

    xf = x.reshape(m, d_model)
    for i in range(depth):
        seg = lambda s: w_in[i][:, int(offs[s]):int(offs[s + 1])].astype(BF16)
        w_q, w_k, w_v, w_qi, w_ki, w_wi, w_uv, w_ga, w_gb = [seg(s) for s in range(9)]
        w_small = jnp.concatenate(
            [w_k, w_v, w_ki, jnp.pad(w_wi, ((0, 0), (0, LANES - N_IDX_HEADS)))], axis=1)

        h, k_, v_, kidx, widx = _norm_small_proj(xf, norm_mix_g[i], w_small, k_norm_g[i])
        q_hm = _headproj(h, w_q, q_norm_g[i], "q_proj", post_scale=HEAD_DIM ** -0.5 * LOG2E)
        qi_hm = _headproj(h, w_qi, None, "q_idx_proj")
        uv = _matmul(h, [w_uv], [], lambda a, e: a[0], F32, "uv_proj")
        y_a = _attention(qi_hm, kidx, jnp.transpose(widx[:, :N_IDX_HEADS]), q_hm, k_,
                         jnp.transpose(v_), near, cvec, batch, seq)
        y_b = _sgu(uv, sgu_ln_g[i], sgu_ln_b[i], sgu_w[i], sgu_b[i])
        merged = _merge(h, y_a, y_b, w_ga, w_gb,
                        w_branch_a[i].astype(BF16), w_branch_b[i].astype(BF16))
        x1 = _matmul(merged, [w_out[i].astype(BF16)], [xf],
                     lambda a, e: e[0] + a[0], F32, "out_proj")

        h2 = _rmsnorm(x1, norm_ffn_g[i])
        t = _matmul(h2, [w_gate_ffn[i].astype(BF16), w_up_ffn[i].astype(BF16)], [],
                    lambda a, e: jax.nn.silu(a[0]) * a[1], BF16, "ffn_gate_up")
        x2, x2g, ssq2 = _res_matmul(t, w_down_ffn[i].astype(BF16), x1, norm_ple_g[i], "ffn_down")

        pe = _ple(p[i].reshape(m, -1), w_ple[i].astype(BF16), ple_norm_g[i])
        xf = _matmul(x2g, [w_ple_gate[i].astype(BF16)], [x2, pe],
                     lambda a, e: e[0] + jax.nn.sigmoid(a[0]) * e[1], F32, "ple_gate", row_ssq=ssq2)
    return xf.reshape(batch, seq, d_model)
```

```python
import functools
import math

import numpy as np
import jax
import jax.numpy as jnp
from jax import lax
from jax.experimental import pallas as pl
from jax.experimental.pallas import tpu as pltpu

F32 = jnp.float32
BF16 = jnp.bfloat16
I32 = jnp.int32

N_HEADS = 16
HEAD_DIM = 128
N_IDX_HEADS = 32
IDX_DIM = 128
TOPK_MAX = 256
Q_BLOCK = 128
N_BUCKETS = 32
MAX_DISTANCE = 128
GMLP_WIDTH = 2048
N_GROUPS = 8
GROUP_DIM = GMLP_WIDTH // N_GROUPS
CHUNK = 128
EPS = 1e-6

LANES = 128
SUBLANES = 8
VMEM_LIMIT = 56 * 1024 * 1024
NEG = -(2.0 ** 100)
INT_MIN = -2 ** 31
INT_MAX = 2 ** 31 - 1
KEY_NEG_INF = -2139095041
LOG2E = math.log2(math.e)

SCORE_CHUNK = 4 * LANES
COUNT_CHUNK = SCORE_CHUNK
FAR_CHUNK = 4 * LANES
FAR_BIG = 2 * FAR_CHUNK
NEAR_KEYS = 2 * Q_BLOCK
HEAD_GROUP = 4
VT_ROWS = HEAD_DIM + 16
NORM_ROWS = 128
MM_SUB_ROWS = 256

TILES = {
    "norm_kv_idx_proj": (512, 4 * LANES),
    "q_proj": (1024, 1024),
    "q_idx_proj": (1024, 1024),
    "uv_proj": (1024, 1024),
    "branch_merge": (512, 512),
    "out_proj": (1024, 1024),
    "ffn_gate_up": (2048, 256),
    "ffn_down": (512, 512),
    "ple_gate": (1024, 512),
}


def _params(sem):
    return pltpu.CompilerParams(dimension_semantics=sem, vmem_limit_bytes=VMEM_LIMIT)


def _rmsnorm_kernel(x_ref, g_ref, o_ref):
    x = x_ref[...]
    ms = jnp.mean(x * x, axis=-1, keepdims=True)
    o_ref[...] = (x * lax.rsqrt(ms + EPS) * g_ref[...]).astype(o_ref.dtype)


def _rmsnorm(x, g, tm=256):
    m, d = x.shape
    return pl.pallas_call(
        _rmsnorm_kernel,
        grid=(m // tm,),
        in_specs=[pl.BlockSpec((tm, d), lambda i: (i, 0)),
                  pl.BlockSpec((1, d), lambda i: (0, 0))],
        out_specs=pl.BlockSpec((tm, d), lambda i: (i, 0)),
        out_shape=jax.ShapeDtypeStruct((m, d), BF16),
        compiler_params=_params(("parallel",)),
        name="rmsnorm",
    )(x, g.reshape(1, d))


def _lane_tiles(x):
    return [x[:, c * LANES:(c + 1) * LANES] for c in range(x.shape[1] // LANES)]


def _mm_kernel(*refs, n_w, n_x, epi, norm_dim):
    a_ref = refs[0]
    w_refs = refs[1:1 + n_w]
    x_refs = refs[1 + n_w:1 + n_w + n_x]
    if norm_dim:
        ssq_ref, o_ref, r_ref = refs[1 + n_w + n_x:]

        @pl.when(pl.program_id(1) == 0)
        def _():
            ms = jnp.sum(ssq_ref[...], axis=-1, keepdims=True) * (1.0 / norm_dim)
            r_ref[...] = jnp.broadcast_to(lax.rsqrt(ms + EPS), r_ref.shape)
    else:
        o_ref = refs[1 + n_w + n_x]
    for s in range(a_ref.shape[0] // MM_SUB_ROWS):
        rows = slice(s * MM_SUB_ROWS, (s + 1) * MM_SUB_ROWS)
        a = a_ref[rows, :]
        accs = [jnp.dot(a, w[...], preferred_element_type=F32) for w in w_refs]
        if norm_dim:
            accs = [acc * r_ref[rows, :] for acc in accs]
        o_ref[rows, :] = epi(accs, [x[rows, :] for x in x_refs]).astype(o_ref.dtype)


def _matmul(a, ws, extras, epi, out_dtype, name, row_ssq=None):
    tm, tn = TILES[name]
    m, k = a.shape
    n = ws[0].shape[1]
    normed = row_ssq is not None
    kern = functools.partial(_mm_kernel, n_w=len(ws), n_x=len(extras), epi=epi,
                             norm_dim=k if normed else 0)
    return pl.pallas_call(
        kern,
        grid=(m // tm, n // tn),
        in_specs=([pl.BlockSpec((tm, k), lambda i, j: (i, 0))]
                  + [pl.BlockSpec((k, tn), lambda i, j: (0, j)) for _ in ws]
                  + [pl.BlockSpec((tm, tn), lambda i, j: (i, j)) for _ in extras]
                  + ([pl.BlockSpec((tm, row_ssq.shape[1]), lambda i, j: (i, 0))] if normed else [])),
        out_specs=pl.BlockSpec((tm, tn), lambda i, j: (i, j)),
        out_shape=jax.ShapeDtypeStruct((m, n), out_dtype),
        scratch_shapes=[pltpu.VMEM((tm, tn), F32)] if normed else [],
        compiler_params=_params(("parallel", "arbitrary")),
        name=name,
    )(a, *ws, *extras, *([row_ssq] if normed else []))


def _res_mm_kernel(a_ref, w_ref, r_ref, g_ref, o_ref, og_ref, ssq_ref, *, nk):
    def sub_blocks(base_ref, last):
        for s in range(a_ref.shape[0] // MM_SUB_ROWS):
            rows = slice(s * MM_SUB_ROWS, (s + 1) * MM_SUB_ROWS)
            x = base_ref[rows, :] + jnp.dot(a_ref[rows, :], w_ref[...],
                                            preferred_element_type=F32)
            o_ref[rows, :] = x
            if last:
                og_ref[rows, :] = (x * g_ref[...]).astype(og_ref.dtype)
                ssq_ref[rows, :] = functools.reduce(lambda u, v: u + v, _lane_tiles(x * x))

    if nk == 1:
        sub_blocks(r_ref, True)
    else:
        @pl.when(pl.program_id(2) == 0)
        def _():
            sub_blocks(r_ref, False)

        @pl.when(pl.program_id(2) == 1)
        def _():
            sub_blocks(o_ref, True)


def _res_matmul(a, w, res, g_next, name, nk=1):
    tm, tn = TILES[name]
    m, k = a.shape
    n = w.shape[1]
    assert nk in (1, 2) and k % nk == 0
    tk = k // nk
    return pl.pallas_call(
        functools.partial(_res_mm_kernel, nk=nk),
        grid=(m // tm, n // tn, nk),
        in_specs=[pl.BlockSpec((tm, tk), lambda i, j, kk: (i, kk)),
                  pl.BlockSpec((tk, tn), lambda i, j, kk: (kk, j)),
                  pl.BlockSpec((tm, tn), lambda i, j, kk: (i, j)),
                  pl.BlockSpec((1, tn), lambda i, j, kk: (0, j))],
        out_specs=[pl.BlockSpec((tm, tn), lambda i, j, kk: (i, j)),
                   pl.BlockSpec((tm, tn), lambda i, j, kk: (i, j)),
                   pl.BlockSpec((tm, LANES), lambda i, j, kk: (i, j))],
        out_shape=[jax.ShapeDtypeStruct((m, n), F32),
                   jax.ShapeDtypeStruct((m, n), BF16),
                   jax.ShapeDtypeStruct((m, (n // tn) * LANES), F32)],
        compiler_params=_params(("parallel", "arbitrary", "arbitrary")),
        name=name,
    )(a, w, res, g_next.reshape(1, n))


def _headproj_kernel(*refs, heads, norm, post_scale):
    a_ref, w_ref = refs[0], refs[1]
    o_ref = refs[-1]
    acc = jnp.dot(a_ref[...], w_ref[...], preferred_element_type=F32)
    for hh in range(heads):
        blk = acc[:, hh * LANES:(hh + 1) * LANES]
        if norm:
            ms = jnp.mean(blk * blk, axis=-1, keepdims=True)
            blk = blk * lax.rsqrt(ms + EPS) * refs[2][...]
        if post_scale is not None:
            blk = blk * post_scale
        o_ref[hh] = blk.astype(o_ref.dtype)


def _headproj(h, w, g, name, post_scale=None):
    tm, tn = TILES[name]
    m, k = h.shape
    n_heads = w.shape[1] // LANES
    heads = tn // LANES
    norm = g is not None
    extra_specs = [pl.BlockSpec((1, LANES), lambda i, j: (0, 0))] if norm else []
    extra_args = [g.reshape(1, LANES)] if norm else []
    return pl.pallas_call(
        functools.partial(_headproj_kernel, heads=heads, norm=norm, post_scale=post_scale),
        grid=(m // tm, w.shape[1] // tn),
        in_specs=[pl.BlockSpec((tm, k), lambda i, j: (i, 0)),
                  pl.BlockSpec((k, tn), lambda i, j: (0, j))] + extra_specs,
        out_specs=pl.BlockSpec((heads, tm, LANES), lambda i, j: (j, i, 0)),
        out_shape=jax.ShapeDtypeStruct((n_heads, m, LANES), BF16),
        compiler_params=_params(("parallel", "arbitrary")),
        name=name,
    )(h, w, *extra_args)


def _small_kernel(x_ref, gx_ref, w_ref, g_ref, h_ref, k_ref, v_ref, ki_ref, wi_ref):
    def norm_rows(r, carry):
        rows = pl.ds(pl.multiple_of(r * NORM_ROWS, NORM_ROWS), NORM_ROWS)
        x = x_ref[rows, :]
        ms = jnp.mean(x * x, axis=-1, keepdims=True)
        h_ref[rows, :] = (x * lax.rsqrt(ms + EPS) * gx_ref[...]).astype(h_ref.dtype)
        return carry

    lax.fori_loop(0, x_ref.shape[0] // NORM_ROWS, norm_rows, 0)
    acc = jnp.dot(h_ref[...], w_ref[...], preferred_element_type=F32)
    kk = acc[:, 0:LANES]
    ms = jnp.mean(kk * kk, axis=-1, keepdims=True)
    k_ref[...] = (kk * lax.rsqrt(ms + EPS) * g_ref[...]).astype(k_ref.dtype)
    v_ref[...] = acc[:, LANES:2 * LANES].astype(v_ref.dtype)
    ki_ref[...] = acc[:, 2 * LANES:3 * LANES].astype(ki_ref.dtype)
    wi_ref[...] = acc[:, 3 * LANES:4 * LANES]


def _norm_small_proj(x, gx, w_small, g):
    tm, _ = TILES["norm_kv_idx_proj"]
    m, k = x.shape
    row = lambda i: (i, 0)
    return pl.pallas_call(
        _small_kernel,
        grid=(m // tm,),
        in_specs=[pl.BlockSpec((tm, k), row),
                  pl.BlockSpec((1, k), lambda i: (0, 0)),
                  pl.BlockSpec((k, 4 * LANES), lambda i: (0, 0)),
                  pl.BlockSpec((1, LANES), lambda i: (0, 0))],
        out_specs=[pl.BlockSpec((tm, k), row)] + [pl.BlockSpec((tm, LANES), row)] * 4,
        out_shape=[jax.ShapeDtypeStruct((m, k), BF16)]
                  + [jax.ShapeDtypeStruct((m, LANES), BF16)] * 3
                  + [jax.ShapeDtypeStruct((m, LANES), F32)],
        compiler_params=_params(("parallel",)),
        name="norm_kv_idx_proj",
    )(x, gx.reshape(1, k), w_small, g.reshape(1, LANES))


def _attn_kernel(qi_ref, kidx_ref, wt_ref, q_ref, k_ref, vt_ref, near_ref, cvec_ref, eye_ref, o_ref,
                 key_ref, m_ref, acc_ref, qa_ref, s_ref, lim_ref,
                 *, k_sel, idx_scale, pos_bits):
    j = pl.program_id(1)
    nt = (((1,), (1,)), ((), ()))
    n_chunks = (j + 4) // 4

    def score_keys(start, n_keys):
        start = pl.multiple_of(start, SCORE_CHUNK)
        kc = kidx_ref[pl.ds(start, n_keys), :]
        acc = jnp.zeros((n_keys, Q_BLOCK), F32)
        for hp in range(N_IDX_HEADS // 2):
            qpair = qi_ref[2 * hp:2 * hp + 2].reshape(2 * Q_BLOCK, IDX_DIM)
            d = lax.dot_general(kc, qpair, nt, preferred_element_type=F32)
            acc = acc + jnp.maximum(d[:, :Q_BLOCK], 0.0) * wt_ref[2 * hp:2 * hp + 1, :]
            acc = acc + jnp.maximum(d[:, Q_BLOCK:], 0.0) * wt_ref[2 * hp + 1:2 * hp + 2, :]
        s = acc * idx_scale
        key_pos = start + lax.broadcasted_iota(I32, (n_keys, Q_BLOCK), 0)
        q_pos = j * Q_BLOCK + lax.broadcasted_iota(I32, (n_keys, Q_BLOCK), 1)
        s = jnp.where(key_pos <= q_pos, s, -jnp.inf)
        bits = lax.bitcast_convert_type(s, I32)
        key_ref[pl.ds(start, n_keys), :] = bits ^ ((bits >> 31) & 0x7FFFFFFF)

    def loop(n, body):
        lax.fori_loop(0, n, lambda c, carry: (body(c), carry)[1], 0)

    loop(n_chunks // 2, lambda c: score_keys(c * (2 * SCORE_CHUNK), 2 * SCORE_CHUNK))
    loop(n_chunks % 2, lambda c: score_keys((n_chunks - 1) * SCORE_CHUNK, SCORE_CHUNK))

    n_count = n_chunks
    count_row = lax.broadcasted_iota(I32, (COUNT_CHUNK, Q_BLOCK), 0)

    def count_keys(pred):
        def body(c, a):
            start = pl.multiple_of(c * COUNT_CHUNK, COUNT_CHUNK)
            hit = jnp.where(pred(key_ref[pl.ds(start, COUNT_CHUNK), :], start + count_row), 1, 0)
            return a + jnp.sum(hit.reshape(COUNT_CHUNK // SUBLANES, SUBLANES, Q_BLOCK), axis=0)

        a = lax.fori_loop(0, n_count, body, jnp.zeros((SUBLANES, Q_BLOCK), I32))
        return jnp.sum(a, axis=0, keepdims=True)

    def bit_body(i, carry):
        tu, cnt_tu = carry
        cu = tu | jnp.left_shift(jnp.int32(1), 31 - i)
        cs = cu ^ INT_MIN
        cnt = count_keys(lambda keys, pos: keys >= cs)
        take = cnt >= k_sel
        return jnp.where(take, cu, tu), jnp.where(take, cnt, cnt_tu)

    tu, cnt_ge = lax.fori_loop(
        0, 32, bit_body,
        (jnp.zeros((1, Q_BLOCK), I32), jnp.full((1, Q_BLOCK), n_count * COUNT_CHUNK, I32)))
    thr = tu ^ INT_MIN

    tied = (cnt_ge > k_sel) & (thr > KEY_NEG_INF)
    lim_ref[...] = jnp.full(lim_ref.shape, INT_MAX, I32)

    @pl.when(jnp.max(jnp.where(tied, 1, 0)) > 0)
    def _():
        n_tied_kept = k_sel - count_keys(lambda keys, pos: keys > thr)

        def pos_body(i, x):
            cand = x | jnp.left_shift(jnp.int32(1), pos_bits - 1 - i)
            cnt = count_keys(lambda keys, pos: (keys == thr) & (pos < cand))
            return jnp.where(cnt <= n_tied_kept, cand, x)

        x = lax.fori_loop(0, pos_bits, pos_body, jnp.zeros((1, Q_BLOCK), I32))
        lim_ref[...] = jnp.where(tied, x, INT_MAX)

    pos_lim = lim_ref[...]

    def selected(keys, pos):
        return (keys > thr) | ((keys == thr) & (pos < pos_lim))

    m_ref[...] = jnp.full(m_ref.shape, NEG, F32)
    acc_ref[...] = jnp.zeros(acc_ref.shape, F32)
    qa_ref[:, :HEAD_DIM] = q_ref[...].reshape(N_HEADS * Q_BLOCK, HEAD_DIM)
    qa_ref[:, HEAD_DIM:] = eye_ref[...]

    def softmax_step(h, t):
        hs = slice(h * Q_BLOCK, (h + 1) * Q_BLOCK)
        m_old = m_ref[:, hs]
        m_new = jnp.maximum(m_old, jnp.max(t, axis=0, keepdims=True))
        m_ref[:, hs] = m_new
        return jnp.exp2(t - m_new).astype(BF16), jnp.exp2(m_old - m_new)

    def attend(keys, vt, logits_of):
        groups = [slice(g * HEAD_GROUP * Q_BLOCK, (g + 1) * HEAD_GROUP * Q_BLOCK)
                  for g in range(N_HEADS // HEAD_GROUP)]
        n_keys = keys.shape[0]
        for g, gs in enumerate(groups):
            s_ref[g, 0:n_keys, :] = lax.dot_general(keys, qa_ref[gs, 0:keys.shape[1]], nt,
                                                    preferred_element_type=F32)
        for g, gs in enumerate(groups):
            pa = [softmax_step(g * HEAD_GROUP + e,
                               logits_of(g * HEAD_GROUP + e,
                                         s_ref[g, 0:n_keys, e * Q_BLOCK:(e + 1) * Q_BLOCK]))
                  for e in range(HEAD_GROUP)]
            p = jnp.concatenate([x[0] for x in pa], axis=1)
            alpha = jnp.concatenate([x[1] for x in pa], axis=1)
            pv = jnp.dot(vt, p, preferred_element_type=F32)
            acc_ref[:, gs] = acc_ref[:, gs] * alpha + pv

    far_end = (j - 1) * Q_BLOCK
    def far_chunk(lo, n_keys):
        start = pl.multiple_of(jnp.minimum(lo, jnp.maximum(far_end - n_keys, 0)), LANES)
        pos = start + lax.broadcasted_iota(I32, (n_keys, Q_BLOCK), 0)
        sel = selected(key_ref[pl.ds(start, n_keys), :], pos) & (pos >= lo) & (pos < far_end)
        maskb = jnp.where(sel, 0.0, NEG).astype(BF16)
        kaug = jnp.concatenate([k_ref[pl.ds(start, n_keys), :], maskb], axis=1)
        attend(kaug, vt_ref[:, pl.ds(start, n_keys)], lambda h, s: s)

    n_big = jnp.maximum(far_end, 0) // FAR_BIG
    rest = jnp.maximum(far_end, 0) - n_big * FAR_BIG
    loop(n_big, lambda c: far_chunk(c * FAR_BIG, FAR_BIG))
    loop((rest + FAR_CHUNK - 1) // FAR_CHUNK,
         lambda c: far_chunk(n_big * FAR_BIG + c * FAR_CHUNK, FAR_CHUNK))
    m_ref[...] = m_ref[...] + cvec_ref[...]

    near_start = pl.multiple_of(jnp.maximum(j - 1, 0) * Q_BLOCK, LANES)
    first = jnp.where(j == 0, 1, 0)
    near_pos = near_start + lax.broadcasted_iota(I32, (NEAR_KEYS, Q_BLOCK), 0)
    sel = selected(key_ref[pl.ds(near_start, NEAR_KEYS), :], near_pos)
    maskb = jnp.where(sel, 0.0, NEG)
    k_near = k_ref[pl.ds(near_start, NEAR_KEYS), :]
    vt_near = vt_ref[:, pl.ds(near_start, NEAR_KEYS)]

    def near_logits(h, s):
        return s + near_ref[first, :, h * Q_BLOCK:(h + 1) * Q_BLOCK] + maskb

    attend(k_near, vt_near, near_logits)

    inv_l = 1.0 / acc_ref[HEAD_DIM:HEAD_DIM + 1, :]
    for h in range(N_HEADS):
        hs = slice(h * Q_BLOCK, (h + 1) * Q_BLOCK)
        o_ref[:, h * HEAD_DIM:(h + 1) * HEAD_DIM] = (
            acc_ref[0:HEAD_DIM, hs] * inv_l[:, hs]).T.astype(o_ref.dtype)


def _t5_bucket_static(n):
    max_exact = N_BUCKETS // 2
    nf = np.maximum(n, 1).astype(np.float32)
    large = max_exact + (np.log(nf / np.float32(max_exact)) / np.float32(math.log(MAX_DISTANCE / max_exact))
                         * np.float32(N_BUCKETS - max_exact)).astype(np.int32)
    large = np.minimum(large, N_BUCKETS - 1)
    return np.where(n < max_exact, n, large)


def _bias_tables(rel_bias, seq):
    buckets = _t5_bucket_static(np.arange(seq, dtype=np.int32))
    far = buckets[Q_BLOCK + 1:]
    assert (far == far[0]).all()
    bias2 = rel_bias.astype(F32) * LOG2E
    span, origin = 4 * NEAR_KEYS, 2 * NEAR_KEYS
    d = np.arange(span) - origin
    r = jnp.where(jnp.asarray(d >= 0)[None, :], bias2[buckets[np.clip(d, 0, seq - 1)]].T, NEG)
    y = jnp.tile(r, (1, NEAR_KEYS))[:, :NEAR_KEYS * (span - 1)].reshape(N_HEADS, NEAR_KEYS, span - 1)

    def table(off):
        t = y[:, :, origin + off:origin + off + Q_BLOCK]
        return jnp.transpose(t, (1, 0, 2)).reshape(NEAR_KEYS, N_HEADS * Q_BLOCK)

    near = jnp.stack([table(Q_BLOCK), table(0)])
    cvec = jnp.repeat(bias2[int(far[0])], Q_BLOCK).reshape(1, N_HEADS * Q_BLOCK)
    return near, cvec


def _attention(qi_hm, kidx, w_t, q_hm, k, v_t, near, cvec, batch, seq):
    m = batch * seq
    n_blk = seq // Q_BLOCK
    k_sel = min(TOPK_MAX, seq // 4)
    hq = N_HEADS * Q_BLOCK
    kern = functools.partial(
        _attn_kernel, k_sel=k_sel, idx_scale=IDX_DIM ** -0.5 * N_IDX_HEADS ** -0.5,
        pos_bits=seq.bit_length())
    blk = lambda b, j: (b * n_blk + j)
    v_t = jnp.concatenate([v_t, jnp.ones((VT_ROWS - HEAD_DIM, m), v_t.dtype)], axis=0)
    assert seq % COUNT_CHUNK == 0 and seq % FAR_CHUNK == 0
    eye = jnp.tile(jnp.eye(Q_BLOCK, dtype=BF16), (N_HEADS, 1))
    return pl.pallas_call(
        kern,
        grid=(batch, n_blk),
        in_specs=[pl.BlockSpec((N_IDX_HEADS, Q_BLOCK, IDX_DIM), lambda b, j: (0, blk(b, j), 0)),
                  pl.BlockSpec((seq, IDX_DIM), lambda b, j: (b, 0)),
                  pl.BlockSpec((N_IDX_HEADS, Q_BLOCK), lambda b, j: (0, blk(b, j))),
                  pl.BlockSpec((N_HEADS, Q_BLOCK, HEAD_DIM), lambda b, j: (0, blk(b, j), 0)),
                  pl.BlockSpec((seq, HEAD_DIM), lambda b, j: (b, 0)),
                  pl.BlockSpec((VT_ROWS, seq), lambda b, j: (0, b)),
                  pl.BlockSpec((2, NEAR_KEYS, hq), lambda b, j: (0, 0, 0)),
                  pl.BlockSpec((1, hq), lambda b, j: (0, 0)),
                  pl.BlockSpec((hq, Q_BLOCK), lambda b, j: (0, 0))],
        out_specs=pl.BlockSpec((Q_BLOCK, N_HEADS * HEAD_DIM), lambda b, j: (blk(b, j), 0)),
        out_shape=jax.ShapeDtypeStruct((m, N_HEADS * HEAD_DIM), BF16),
        scratch_shapes=[pltpu.VMEM((seq, Q_BLOCK), I32),
                        pltpu.VMEM((1, hq), F32),
                        pltpu.VMEM((VT_ROWS, hq), F32),
                        pltpu.VMEM((hq, HEAD_DIM + Q_BLOCK), BF16),
                        pltpu.VMEM((N_HEADS // HEAD_GROUP, FAR_BIG, HEAD_GROUP * Q_BLOCK), F32),
                        pltpu.VMEM((1, Q_BLOCK), I32)],
        compiler_params=_params(("parallel", "arbitrary")),
        name="dsa_attention",
    )(qi_hm, kidx, w_t, q_hm, k, v_t, near, cvec, eye)


def _gelu_tanh(x):
    c = -2.0 * math.sqrt(2.0 / math.pi) * LOG2E
    return x / (1.0 + jnp.exp2(x * (x * x * (0.044715 * c) + c)))


def _sgu_kernel(uv_ref, g_ref, b_ref, ws_ref, bs_ref, o_ref):
    uv = _gelu_tanh(uv_ref[...])
    u = uv[:, :GMLP_WIDTH]
    v = uv[:, GMLP_WIDTH:]
    mu = jnp.mean(v, axis=-1, keepdims=True)
    vc = v - mu
    var = jnp.mean(vc * vc, axis=-1, keepdims=True)
    vn = (vc * lax.rsqrt(var + EPS) * g_ref[...] + b_ref[...]).astype(BF16)
    r = lax.broadcasted_iota(I32, (CHUNK, CHUNK), 0)
    c = lax.broadcasted_iota(I32, (CHUNK, CHUNK), 1)
    tril = c <= r
    bs = bs_ref[...]
    for g in range(N_GROUPS):
        ws = jnp.where(tril, ws_ref[g], 0.0).astype(BF16)
        mixed = jnp.dot(ws, vn[:, g * GROUP_DIM:(g + 1) * GROUP_DIM],
                        preferred_element_type=F32) + bs[:, g:g + 1]
        o_ref[:, g * GROUP_DIM:(g + 1) * GROUP_DIM] = (
            u[:, g * GROUP_DIM:(g + 1) * GROUP_DIM] * mixed).astype(o_ref.dtype)


def _sgu(uv, ln_g, ln_b, w_s, b_s):
    m = uv.shape[0]
    bs_t = jnp.zeros((CHUNK, LANES), F32).at[:, :N_GROUPS].set(jnp.transpose(b_s))
    return pl.pallas_call(
        _sgu_kernel,
        grid=(m // CHUNK,),
        in_specs=[pl.BlockSpec((CHUNK, 2 * GMLP_WIDTH), lambda i: (i, 0)),
                  pl.BlockSpec((1, GMLP_WIDTH), lambda i: (0, 0)),
                  pl.BlockSpec((1, GMLP_WIDTH), lambda i: (0, 0)),
                  pl.BlockSpec((N_GROUPS, CHUNK, CHUNK), lambda i: (0, 0, 0)),
                  pl.BlockSpec((CHUNK, LANES), lambda i: (0, 0))],
        out_specs=pl.BlockSpec((CHUNK, GMLP_WIDTH), lambda i: (i, 0)),
        out_shape=jax.ShapeDtypeStruct((m, GMLP_WIDTH), BF16),
        compiler_params=_params(("parallel",)),
        name="chunked_sgu",
    )(uv, ln_g.reshape(1, -1), ln_b.reshape(1, -1), w_s, bs_t)


def _merge_kernel(h_ref, ya_ref, yb_ref, wga_ref, wgb_ref, wa_ref, wb_ref, o_ref):
    h = h_ref[...]
    ga = jax.nn.sigmoid(jnp.dot(h, wga_ref[...], preferred_element_type=F32))
    a = jnp.dot(ya_ref[...], wa_ref[...], preferred_element_type=F32)
    out = ga * a
    gb = jax.nn.sigmoid(jnp.dot(h, wgb_ref[...], preferred_element_type=F32))
    b = jnp.dot(yb_ref[...], wb_ref[...], preferred_element_type=F32)
    o_ref[...] = (out + gb * b).astype(o_ref.dtype)


def _merge(h, ya, yb, wga, wgb, wa, wb):
    tm, tn = TILES["branch_merge"]
    m, d = h.shape
    n = wga.shape[1]
    ka, kb = ya.shape[1], yb.shape[1]
    row = lambda i, j: (i, 0)
    colw = lambda i, j: (0, j)
    return pl.pallas_call(
        _merge_kernel,
        grid=(m // tm, n // tn),
        in_specs=[pl.BlockSpec((tm, d), row), pl.BlockSpec((tm, ka), row), pl.BlockSpec((tm, kb), row),
                  pl.BlockSpec((d, tn), colw), pl.BlockSpec((d, tn), colw),
                  pl.BlockSpec((ka, tn), colw), pl.BlockSpec((kb, tn), colw)],
        out_specs=pl.BlockSpec((tm, tn), lambda i, j: (i, j)),
        out_shape=jax.ShapeDtypeStruct((m, n), BF16),
        compiler_params=_params(("parallel", "arbitrary")),
        name="branch_merge",
    )(h, ya, yb, wga, wgb, wa, wb)


def _ple_kernel(p_ref, w_ref, g_ref, o_ref):
    acc = jnp.dot(p_ref[...].astype(BF16), w_ref[...], preferred_element_type=F32)
    ms = jnp.mean(acc * acc, axis=-1, keepdims=True)
    o_ref[...] = (acc * lax.rsqrt(ms + EPS) * g_ref[...]).astype(o_ref.dtype)


def _ple(p, w, g, tm=256):
    m, dp = p.shape
    d = w.shape[1]
    return pl.pallas_call(
        _ple_kernel,
        grid=(m // tm,),
        in_specs=[pl.BlockSpec((tm, dp), lambda i: (i, 0)),
                  pl.BlockSpec((dp, d), lambda i: (0, 0)),
                  pl.BlockSpec((1, d), lambda i: (0, 0))],
        out_specs=pl.BlockSpec((tm, d), lambda i: (i, 0)),
        out_shape=jax.ShapeDtypeStruct((m, d), F32),
        compiler_params=_params(("parallel",)),
        name="ple_embed",
    )(p, w, g.reshape(1, d))


def kernel(x, p, w_in, q_norm_g, k_norm_g, rel_bias, sgu_ln_g, sgu_ln_b, sgu_w, sgu_b, w_branch_a, w_branch_b, w_out, norm_mix_g, norm_ffn_g, w_gate_ffn, w_up_ffn, w_down_ffn, w_ple, ple_norm_g, w_ple_gate, norm_ple_g):
    batch, seq, d_model = x.shape
    depth = w_in.shape[0]
    m = batch * seq
    a_width = N_HEADS * HEAD_DIM
    sizes = (a_width, HEAD_DIM, HEAD_DIM, N_IDX_HEADS * IDX_DIM, IDX_DIM, N_IDX_HEADS,
             2 * GMLP_WIDTH, d_model, d_model)
    offs = np.concatenate([[0], np.cumsum(sizes)])
    near, cvec = _bias_tables(rel_bias, seq)

    xf = x.reshape(m, d_model)
    for i in range(depth):
        seg = lambda s: w_in[i][:, int(offs[s]):int(offs[s + 1])].astype(BF16)
        w_q, w_k, w_v, w_qi, w_ki, w_wi, w_uv, w_ga, w_gb = [seg(s) for s in range(9)]
        w_small = jnp.concatenate(
            [w_k, w_v, w_ki, jnp.pad(w_wi, ((0, 0), (0, LANES - N_IDX_HEADS)))], axis=1)

        h, k_, v_, kidx, widx = _norm_small_proj(xf, norm_mix_g[i], w_small, k_norm_g[i])
        q_hm = _headproj(h, w_q, q_norm_g[i], "q_proj", post_scale=HEAD_DIM ** -0.5 * LOG2E)
        qi_hm = _headproj(h, w_qi, None, "q_idx_proj")
        uv = _matmul(h, [w_uv], [], lambda a, e: a[0], F32, "uv_proj")
        y_a = _attention(qi_hm, kidx, jnp.transpose(widx[:, :N_IDX_HEADS]), q_hm, k_,
                         jnp.transpose(v_), near, cvec, batch, seq)
        y_b = _sgu(uv, sgu_ln_g[i], sgu_ln_b[i], sgu_w[i], sgu_b[i])
        merged = _merge(h, y_a, y_b, w_ga, w_gb,
                        w_branch_a[i].astype(BF16), w_branch_b[i].astype(BF16))
        x1 = _matmul(merged, [w_out[i].astype(BF16)], [xf],
                     lambda a, e: e[0] + a[0], F32, "out_proj")

        h2 = _rmsnorm(x1, norm_ffn_g[i])
        t = _matmul(h2, [w_gate_ffn[i].astype(BF16), w_up_ffn[i].astype(BF16)], [],
                    lambda a, e: jax.nn.silu(a[0]) * a[1], BF16, "ffn_gate_up")
        x2, x2g, ssq2 = _res_matmul(t, w_down_ffn[i].astype(BF16), x1, norm_ple_g[i], "ffn_down")

        pe = _ple(p[i].reshape(m, -1), w_ple[i].astype(BF16), ple_norm_g[i])
        xf = _matmul(x2g, [w_ple_gate[i].astype(BF16)], [x2, pe],
                     lambda a, e: e[0] + jax.nn.sigmoid(a[0]) * e[1], F32, "ple_gate", row_ssq=ssq2)
    return xf.reshape(batch, seq, d_model)
```
